```python
import jax, jax.numpy as jnp
from jax import lax
import numpy as np

D_MODEL = 1024
BATCH = 4
SEQ = 4096
DEPTH = 4

CHUNK = 64
N_EVEN = (DEPTH + 1) // 2
N_ODD = DEPTH // 2
EPS = 1e-6

POOL_GROUPS = 4
POOL_CH = D_MODEL // 8
POOL_WIDTH = POOL_GROUPS * POOL_CH
POOL_WINDOWS = (2, 4, 8, 16)

MLA_HEADS = 8
MLA_NOPE = D_MODEL // 16
MLA_ROPE = D_MODEL // 32
MLA_V = D_MODEL // 16
MLA_Q_RANK = D_MODEL // 4
MLA_KV_RANK = D_MODEL // 8
ROPE_THETA = 10000.0
Q_BLOCK = 128
EVEN_IN = POOL_WIDTH + MLA_Q_RANK + MLA_KV_RANK + MLA_ROPE
EVEN_MIX = POOL_WIDTH + MLA_HEADS * MLA_V

CA_HEADS = 8
CA_HEAD_DIM = D_MODEL // 16
CA_LEFT_CHUNKS = 8
REL_CLIP = 128

SGU_GROUPS = 4
SGU_CH = D_MODEL // 8
SGU_WIDTH = SGU_GROUPS * SGU_CH
SGU_CHUNK = 128
ODD_IN = 3 * CA_HEADS * CA_HEAD_DIM + 2 * SGU_WIDTH
ODD_MIX = CA_HEADS * CA_HEAD_DIM + SGU_WIDTH

D_FF = 2816
N_EXPERTS = 8
TOP_K = 2
EXPERT_FF = 3584

kernel_name = 'hybrid_chunk_causal_pool_mla_relattn_sgu_moe'


def rms_norm(x, g):
    xf = x.astype(jnp.float32)
    y = xf * lax.rsqrt(jnp.mean(xf * xf, axis=-1, keepdims=True) + EPS)
    return (y * g.astype(jnp.float32)).astype(x.dtype)


def layer_norm(x, g, b):
    xf = x.astype(jnp.float32)
    mu = jnp.mean(xf, axis=-1, keepdims=True)
    xc = xf - mu
    y = xc * lax.rsqrt(jnp.mean(xc * xc, axis=-1, keepdims=True) + EPS)
    return (y * g.astype(jnp.float32) + b.astype(jnp.float32)).astype(x.dtype)


def swiglu(x, w_gate, w_up, w_down):
    return (jax.nn.silu(x @ w_gate) * (x @ w_up)) @ w_down


def rope_tables(s):
    inv = 1.0 / (ROPE_THETA ** (jnp.arange(0, MLA_ROPE, 2, dtype=jnp.float32) / MLA_ROPE))
    ang = jnp.arange(s, dtype=jnp.float32)[:, None] * inv[None, :]
    return jnp.cos(ang), jnp.sin(ang)


def apply_rope(x, cos, sin):
    x1, x2 = jnp.split(x, 2, axis=-1)
    return jnp.concatenate([x1 * cos - x2 * sin, x2 * cos + x1 * sin], axis=-1).astype(x.dtype)


def multiscale_pool(a, w_pool, scale):
    b, s, g, c = a.shape
    af = a.astype(jnp.float32)
    cs = jnp.concatenate([jnp.zeros((b, 1, g, c), jnp.float32), jnp.cumsum(af, axis=1)], axis=1)
    t = jnp.arange(s)[:, None]
    win = jnp.array(POOL_WINDOWS, dtype=jnp.int32)[None, :]
    lo = jnp.maximum(t + 1 - win, 0)
    lo_sum = cs[:, lo, jnp.arange(g)[None, :]]
    count = (t + 1 - lo).astype(jnp.float32)[None, :, :, None]
    pooled = ((cs[:, 1:] - lo_sum) / count - af).astype(a.dtype)
    y = jnp.einsum('bsgc,gcd->bsgd', pooled, w_pool) * scale
    return y.reshape(b, s, g * c)


def mla_attention(q, k, v):
    b, s, h, dq = q.shape
    nqb = s // Q_BLOCK
    scale = dq ** -0.5
    k_chunk = jnp.arange(s) // CHUNK
    q_blocks = q.reshape(b, nqb, Q_BLOCK, h, dq).transpose(1, 0, 2, 3, 4)

    def block(args):
        qb, start = args
        q_chunk = (start + jnp.arange(Q_BLOCK)) // CHUNK
        sc = jnp.einsum('bqhd,bkhd->bhqk', qb, k, preferred_element_type=jnp.float32) * scale
        sc = jnp.where(k_chunk[None, :] <= q_chunk[:, None], sc, -jnp.inf)
        p = jax.nn.softmax(sc, axis=-1).astype(v.dtype)
        return jnp.einsum('bhqk,bkhd->bqhd', p, v)

    out = lax.map(block, (q_blocks, jnp.arange(nqb) * Q_BLOCK))
    return out.transpose(1, 0, 2, 3, 4).reshape(b, s, h, v.shape[-1])


def chunk_band_attention(q, k, v, rel_bias):
    b, s, h, d = q.shape
    nc = s // CHUNK
    band = (CA_LEFT_CHUNKS + 1) * CHUNK
    qc = q.reshape(b, nc, CHUNK, h, d)

    def gather_band(t):
        tp = jnp.pad(t.reshape(b, nc, CHUNK, h, d), ((0, 0), (CA_LEFT_CHUNKS, 0), (0, 0), (0, 0), (0, 0)))
        return jnp.concatenate([tp[:, j:j + nc] for j in range(CA_LEFT_CHUNKS + 1)], axis=2)

    kb, vb = gather_band(k), gather_band(v)
    sc = jnp.einsum('bnqhd,bnkhd->bhnqk', qc, kb, preferred_element_type=jnp.float32) * (d ** -0.5)
    rel = jnp.arange(CHUNK)[:, None] + CA_LEFT_CHUNKS * CHUNK - jnp.arange(band)[None, :]
    bias = rel_bias[:, jnp.clip(rel, -REL_CLIP, REL_CLIP) + REL_CLIP].astype(jnp.float32)
    valid = (jnp.arange(nc)[:, None] - CA_LEFT_CHUNKS + jnp.arange(band)[None, :] // CHUNK) >= 0
    sc = jnp.where(valid[None, None, :, None, :], sc + bias[:, None], -jnp.inf)
    p = jax.nn.softmax(sc, axis=-1).astype(vb.dtype)
    out = jnp.einsum('bhnqk,bnkhd->bnqhd', p, vb)
    return out.reshape(b, s, h, d)


def spatial_gating(u, v, norm_g, norm_b, w_s, b_s):
    b, s, g, c = v.shape
    n = s // SGU_CHUNK
    vn = layer_norm(v, norm_g, norm_b).reshape(b, n, SGU_CHUNK, g, c)
    causal = jnp.tril(jnp.ones((SGU_CHUNK, SGU_CHUNK), dtype=bool))
    w = jnp.where(causal[None], w_s, jnp.zeros_like(w_s))
    mixed = jnp.einsum('gts,bnsgc->bntgc', w, vn) + b_s.T[None, None, :, :, None]
    return u * mixed.reshape(b, s, g, c)


def moe_swiglu(h, router, w_gate, w_up, w_down):
    b, s, d = h.shape
    t = h.reshape(b * s, d)
    logits = (t @ router).astype(jnp.float32)
    top_val, top_idx = lax.top_k(logits, TOP_K)
    top_w = jax.nn.softmax(top_val, axis=-1)
    gates = jnp.sum(jax.nn.one_hot(top_idx, N_EXPERTS, dtype=jnp.float32) * top_w[..., None], axis=1)
    out = jnp.zeros((b * s, d), jnp.float32)
    for e in range(N_EXPERTS):
        out = out + gates[:, e:e + 1] * swiglu(t, w_gate[e], w_up[e], w_down[e]).astype(jnp.float32)
    return out.astype(h.dtype).reshape(b, s, d)


def even_mixer(h, w_in, pool_w, pool_scale, q_norm, w_q_up, kv_norm, w_kv_up, w_out, cos, sin):
    b, s, _ = h.shape
    z = h @ w_in
    a, cq, ckv, kr = jnp.split(z, [POOL_WIDTH, POOL_WIDTH + MLA_Q_RANK, POOL_WIDTH + MLA_Q_RANK + MLA_KV_RANK], axis=-1)
    pool = multiscale_pool(a.reshape(b, s, POOL_GROUPS, POOL_CH), pool_w, pool_scale)
    q = (rms_norm(cq, q_norm) @ w_q_up).reshape(b, s, MLA_HEADS, MLA_NOPE + MLA_ROPE)
    q_nope, q_rope = jnp.split(q, [MLA_NOPE], axis=-1)
    kv = (rms_norm(ckv, kv_norm) @ w_kv_up).reshape(b, s, MLA_HEADS, MLA_NOPE + MLA_V)
    k_nope, v = jnp.split(kv, [MLA_NOPE], axis=-1)
    q_rope = apply_rope(q_rope, cos[None, :, None], sin[None, :, None])
    k_rope = apply_rope(kr, cos[None], sin[None])
    qf = jnp.concatenate([q_nope, q_rope], axis=-1)
    kf = jnp.concatenate([k_nope, jnp.broadcast_to(k_rope[:, :, None], (b, s, MLA_HEADS, MLA_ROPE))], axis=-1)
    att = mla_attention(qf, kf, v).reshape(b, s, MLA_HEADS * MLA_V)
    return jnp.concatenate([pool, att], axis=-1) @ w_out


def odd_mixer(h, w_in, rel_bias, sgu_norm_g, sgu_norm_b, sgu_w, sgu_b, w_out):
    b, s, _ = h.shape
    z = h @ w_in
    qkv, uv = jnp.split(z, [3 * CA_HEADS * CA_HEAD_DIM], axis=-1)
    q, k, v = [t.reshape(b, s, CA_HEADS, CA_HEAD_DIM) for t in jnp.split(qkv, 3, axis=-1)]
    att = chunk_band_attention(q, k, v, rel_bias).reshape(b, s, CA_HEADS * CA_HEAD_DIM)
    u, gv = jnp.split(jax.nn.gelu(uv), 2, axis=-1)
    sg = spatial_gating(u.reshape(b, s, SGU_GROUPS, SGU_CH), gv.reshape(b, s, SGU_GROUPS, SGU_CH),
                        sgu_norm_g, sgu_norm_b, sgu_w, sgu_b).reshape(b, s, SGU_WIDTH)
    return jnp.concatenate([att, sg], axis=-1) @ w_out


def setup_inputs(seed: int = 0) -> dict:
    key = jax.random.key(seed)
    ks = iter(jax.random.split(key, 32))
    nrm = lambda shape, scale: jax.random.normal(next(ks), shape, jnp.float32) * scale
    gain = lambda shape: 1.0 + nrm(shape, 0.1)
    E, O, D = N_EVEN, N_ODD, D_MODEL
    res = (2 * DEPTH) ** -0.5
    return {
        'x': nrm((BATCH, SEQ, D), 1.0),
        'ev_norm_mix': gain((E, D)),
        'ev_w_in': nrm((E, D, EVEN_IN), D ** -0.5),
        'ev_pool_w': nrm((E, POOL_GROUPS, POOL_CH, POOL_CH), POOL_CH ** -0.5),
        'ev_pool_scale': gain((E, POOL_GROUPS, POOL_CH)),
        'ev_q_norm': gain((E, MLA_Q_RANK)),
        'ev_w_q_up': nrm((E, MLA_Q_RANK, MLA_HEADS * (MLA_NOPE + MLA_ROPE)), MLA_Q_RANK ** -0.5),
        'ev_kv_norm': gain((E, MLA_KV_RANK)),
        'ev_w_kv_up': nrm((E, MLA_KV_RANK, MLA_HEADS * (MLA_NOPE + MLA_V)), MLA_KV_RANK ** -0.5),
        'ev_w_out': nrm((E, EVEN_MIX, D), EVEN_MIX ** -0.5 * res),
        'ev_norm_ffn': gain((E, D)),
        'ev_ffn_w_gate': nrm((E, D, D_FF), D ** -0.5),
        'ev_ffn_w_up': nrm((E, D, D_FF), D ** -0.5),
        'ev_ffn_w_down': nrm((E, D_FF, D), D_FF ** -0.5 * res),
        'od_norm_mix': gain((O, D)),
        'od_w_in': nrm((O, D, ODD_IN), D ** -0.5),
        'od_rel_bias': nrm((O, CA_HEADS, 2 * REL_CLIP + 1), 0.5),
        'od_sgu_norm_g': gain((O, SGU_GROUPS, SGU_CH)),
        'od_sgu_norm_b': nrm((O, SGU_GROUPS, SGU_CH), 0.02),
        'od_sgu_w': nrm((O, SGU_GROUPS, SGU_CHUNK, SGU_CHUNK), SGU_CHUNK ** -0.5),
        'od_sgu_b': gain((O, SGU_GROUPS, SGU_CHUNK)),
        'od_w_out': nrm((O, ODD_MIX, D), ODD_MIX ** -0.5 * res),
        'od_norm_ffn': gain((O, D)),
        'od_router': nrm((O, D, N_EXPERTS), D ** -0.5),
        'od_moe_w_gate': nrm((O, N_EXPERTS, D, EXPERT_FF), D ** -0.5),
        'od_moe_w_up': nrm((O, N_EXPERTS, D, EXPERT_FF), D ** -0.5),
        'od_moe_w_down': nrm((O, N_EXPERTS, EXPERT_FF, D), EXPERT_FF ** -0.5 * res),
        'final_norm': gain((D,)),
    }


def reference(x, ev_norm_mix, ev_w_in, ev_pool_w, ev_pool_scale, ev_q_norm, ev_w_q_up, ev_kv_norm,
              ev_w_kv_up, ev_w_out, ev_norm_ffn, ev_ffn_w_gate, ev_ffn_w_up, ev_ffn_w_down,
              od_norm_mix, od_w_in, od_rel_bias, od_sgu_norm_g, od_sgu_norm_b, od_sgu_w, od_sgu_b,
              od_w_out, od_norm_ffn, od_router, od_moe_w_gate, od_moe_w_up, od_moe_w_down, final_norm):
    cos, sin = rope_tables(x.shape[1])
    for layer in range(DEPTH):
        i = layer // 2
        if layer % 2 == 0:
            h = rms_norm(x, ev_norm_mix[i])
            x = x + even_mixer(h, ev_w_in[i], ev_pool_w[i], ev_pool_scale[i], ev_q_norm[i], ev_w_q_up[i],
                               ev_kv_norm[i], ev_w_kv_up[i], ev_w_out[i], cos, sin)
            h = rms_norm(x, ev_norm_ffn[i])
            x = x + swiglu(h, ev_ffn_w_gate[i], ev_ffn_w_up[i], ev_ffn_w_down[i])
        else:
            h = rms_norm(x, od_norm_mix[i])
            x = x + odd_mixer(h, od_w_in[i], od_rel_bias[i], od_sgu_norm_g[i], od_sgu_norm_b[i],
                              od_sgu_w[i], od_sgu_b[i], od_w_out[i])
            h = rms_norm(x, od_norm_ffn[i])
            x = x + moe_swiglu(h, od_router[i], od_moe_w_gate[i], od_moe_w_up[i], od_moe_w_down[i])
    return rms_norm(x, final_norm)
```

```python
import functools

import jax
import jax.numpy as jnp
from jax import lax
from jax.experimental import pallas as pl
from jax.experimental.pallas import tpu as pltpu

F32 = jnp.float32
BF16 = jnp.bfloat16
I32 = jnp.int32

D_MODEL = 1024
CHUNK = 64
EPS = 1e-6

POOL_GROUPS = 4
POOL_CH = 128
POOL_WIDTH = 512
POOL_WINDOWS = (2, 4, 8, 16)
POOL_HALO = 16

MLA_HEADS = 8
MLA_NOPE = 64
MLA_ROPE = 32
MLA_V = 64
MLA_Q_RANK = 256
MLA_KV_RANK = 128
ROPE_THETA = 10000.0
HEAD_PAD = 128

CA_HEADS = 8
CA_HEAD_DIM = 64
CA_LEFT_CHUNKS = 8
REL_CLIP = 128
CA_WIDTH = CA_HEADS * CA_HEAD_DIM

SGU_GROUPS = 4
SGU_CH = 128
SGU_WIDTH = 512
SGU_CHUNK = 128

D_FF = 2816
N_EXPERTS = 8
EXPERT_FF = 3584

NEG = -1e30

TM = 512
TQ = 256
MOE_TM = 512
MOE_TF = 512
VMEM_LIMIT = 56 * 1024 * 1024


def _cparams(sem):
    return pltpu.CompilerParams(dimension_semantics=sem, vmem_limit_bytes=VMEM_LIMIT)


def _rms(x, g):
    return x * lax.rsqrt(jnp.mean(x * x, axis=-1, keepdims=True) + EPS) * g


def _dot(a, b):
    return jnp.dot(a, b, preferred_element_type=F32)


def _dot_nt(a, b):
    return lax.dot_general(a, b, (((1,), (1,)), ((), ())), preferred_element_type=F32)


def _const_spec(shape):
    nd = len(shape)
    return pl.BlockSpec(shape, lambda *_: (0,) * nd, pipeline_mode=pl.Buffered(1))


def _even_in_body(x_ref, g_ref, win_ref, qn_ref, wq_ref, kvn_ref, wkv_ref, cq_ref, sq_ref, ck_ref, sk_ref,
                  a_ref, q_ref, k_ref, v_ref):
    h = _rms(x_ref[...], g_ref[...]).astype(BF16)
    z = _dot(h, win_ref[...])
    a_ref[...] = z[:, :POOL_WIDTH]
    o = POOL_WIDTH
    cq = z[:, o:o + MLA_Q_RANK]
    o += MLA_Q_RANK
    ckv = z[:, o:o + MLA_KV_RANK]
    o += MLA_KV_RANK
    kr = z[:, o:o + HEAD_PAD]
    kr_rot = z[:, o + HEAD_PAD:o + 2 * HEAD_PAD]

    hw = MLA_HEADS * HEAD_PAD
    qq = _dot(_rms(cq, qn_ref[...]).astype(BF16), wq_ref[...])
    cq_t, sq_t = cq_ref[...], sq_ref[...]
    for h_i in range(MLA_HEADS):
        sl = slice(h_i * HEAD_PAD, (h_i + 1) * HEAD_PAD)
        sl_rot = slice(hw + h_i * HEAD_PAD, hw + (h_i + 1) * HEAD_PAD)
        q_ref[:, sl] = (qq[:, sl] * cq_t + qq[:, sl_rot] * sq_t).astype(BF16)

    kv = _dot(_rms(ckv, kvn_ref[...]).astype(BF16), wkv_ref[...])
    k_rope = kr * ck_ref[...] + kr_rot * sk_ref[...]
    for h_i in range(MLA_HEADS):
        sl = slice(h_i * HEAD_PAD, (h_i + 1) * HEAD_PAD)
        k_ref[:, sl] = (kv[:, sl] + k_rope).astype(BF16)
    v_ref[...] = kv[:, hw:].astype(BF16)


def _even_in(x, g, win, qn, wq, kvn, wkv, tabs, seq):
    t = x.shape[0]
    hw = MLA_HEADS * HEAD_PAD
    row = lambda w: pl.BlockSpec((TM, w), lambda i: (i, 0))
    tab = pl.BlockSpec((TM, HEAD_PAD), lambda i: (i % (seq // TM), 0))
    return pl.pallas_call(
        _even_in_body,
        grid=(t // TM,),
        in_specs=[row(D_MODEL), _const_spec(g.shape), _const_spec(win.shape), _const_spec(qn.shape),
                  _const_spec(wq.shape), _const_spec(kvn.shape), _const_spec(wkv.shape), tab, tab, tab, tab],
        out_specs=[row(POOL_WIDTH), row(hw), row(hw), row(MLA_HEADS * MLA_V)],
        out_shape=[jax.ShapeDtypeStruct((t, POOL_WIDTH), F32), jax.ShapeDtypeStruct((t, hw), BF16),
                   jax.ShapeDtypeStruct((t, hw), BF16), jax.ShapeDtypeStruct((t, MLA_HEADS * MLA_V), BF16)],
        compiler_params=_cparams(("parallel",)),
        name="even_in",
    )(x, g, win, qn, wq, kvn, wkv, *tabs)


def _mla_body(q_ref, k_ref, v_ref, o_ref):
    qi = pl.program_id(2)
    r = lax.broadcasted_iota(I32, (TQ, TQ), 0) // CHUNK
    c = lax.broadcasted_iota(I32, (TQ, TQ), 1) // CHUNK
    diag_ok = c <= r
    lane = lax.broadcasted_iota(I32, (TQ, 2 * MLA_V), 1)
    outs = []
    for hh in range(2):
        hs = slice(hh * HEAD_PAD, (hh + 1) * HEAD_PAD)
        q = q_ref[:, hs]

        def step(j, carry, masked, hs=hs, q=q):
            m, l, acc = carry
            rows = pl.ds(pl.multiple_of(j * TQ, TQ), TQ)
            s = _dot_nt(q, k_ref[rows, hs])
            if masked:
                s = jnp.where(diag_ok, s, NEG)
            m_new = jnp.maximum(m, jnp.max(s, axis=-1, keepdims=True))
            alpha = jnp.exp(m - m_new)
            p = jnp.exp(s - m_new)
            l = alpha * l + jnp.sum(p, axis=-1, keepdims=True)
            acc = alpha * acc + _dot(p.astype(BF16), v_ref[rows, :])
            return m_new, l, acc

        init = (jnp.full((TQ, 1), NEG, F32), jnp.zeros((TQ, 1), F32), jnp.zeros((TQ, 2 * MLA_V), F32))
        carry = lax.fori_loop(0, qi, functools.partial(step, masked=False), init)
        _, l, acc = step(qi, carry, True)
        outs.append(acc / l)
    o_ref[...] = jnp.where(lane < MLA_V, outs[0], outs[1]).astype(BF16)


def _mla_attention(q, k, v, batch, seq):
    t = q.shape[0]
    nq = seq // TQ
    pairs = MLA_HEADS // 2
    return pl.pallas_call(
        _mla_body,
        grid=(batch, pairs, nq),
        in_specs=[pl.BlockSpec((TQ, 2 * HEAD_PAD), lambda b, p, i: (b * nq + i, p)),
                  pl.BlockSpec((seq, 2 * HEAD_PAD), lambda b, p, i: (b, p)),
                  pl.BlockSpec((seq, 2 * MLA_V), lambda b, p, i: (b, p))],
        out_specs=pl.BlockSpec((TQ, 2 * MLA_V), lambda b, p, i: (b * nq + i, p)),
        out_shape=jax.ShapeDtypeStruct((t, MLA_HEADS * MLA_V), BF16),
        compiler_params=_cparams(("parallel", "parallel", "arbitrary")),
        name="mla_attention",
    )(q, k, v)


def _even_out_body(x_ref, a_ref, halo_ref, att_ref, pw_ref, ps_ref, wo_ref, g_ref, wg_ref, wu_ref, wd_ref,
                   o_ref, ext_ref, *, seq):
    pos0 = (pl.program_id(0) * TM) % seq
    ext_ref[0:POOL_HALO, :] = jnp.where(pos0 == 0, 0.0, halo_ref[...])
    ext_ref[POOL_HALO:, :] = a_ref[...]
    pos = lax.broadcasted_iota(I32, (TM, 1), 0) + pos0
    y = x_ref[...] + _dot(att_ref[...], wo_ref[POOL_WIDTH:, :])
    for g_i, win in enumerate(POOL_WINDOWS):
        gs = slice(g_i * POOL_CH, (g_i + 1) * POOL_CH)
        a_g = ext_ref[POOL_HALO:, gs]
        ssum = a_g
        for j in range(1, win):
            ssum = ssum + ext_ref[POOL_HALO - j:POOL_HALO - j + TM, gs]
        count = jnp.minimum(pos + 1, win).astype(F32)
        pooled = (ssum / count - a_g).astype(BF16)
        mixed = (_dot(pooled, pw_ref[g_i]) * ps_ref[:, gs]).astype(BF16)
        y = y + _dot(mixed, wo_ref[gs, :])
    h = _rms(y, g_ref[...]).astype(BF16)
    gate = _dot(h, wg_ref[...])
    up = _dot(h, wu_ref[...])
    act = (gate * jax.nn.sigmoid(gate) * up).astype(BF16)
    o_ref[...] = y + _dot(act, wd_ref[...])


def _even_out(x, a, att, pw, ps, wo, g, wg, wu, wd, seq):
    t = x.shape[0]
    row = lambda w: pl.BlockSpec((TM, w), lambda i: (i, 0))
    halo = pl.BlockSpec((POOL_HALO, POOL_WIDTH), lambda i: (jnp.maximum(i * (TM // POOL_HALO) - 1, 0), 0))
    return pl.pallas_call(
        functools.partial(_even_out_body, seq=seq),
        grid=(t // TM,),
        in_specs=[row(D_MODEL), row(POOL_WIDTH), halo, row(MLA_HEADS * MLA_V), _const_spec(pw.shape),
                  _const_spec(ps.shape), _const_spec(wo.shape), _const_spec(g.shape), _const_spec(wg.shape),
                  _const_spec(wu.shape), _const_spec(wd.shape)],
        out_specs=row(D_MODEL),
        out_shape=jax.ShapeDtypeStruct((t, D_MODEL), F32),
        scratch_shapes=[pltpu.VMEM((TM + POOL_HALO, POOL_WIDTH), F32)],
        compiler_params=_cparams(("parallel",)),
        name="even_out",
    )(x, a, a, att, pw, ps, wo, g, wg, wu, wd)


def _odd_in_body(x_ref, g_ref, win_ref, ng_ref, nb_ref, sw_ref, sb_ref, qkv_ref, sg_ref):
    h = _rms(x_ref[...], g_ref[...]).astype(BF16)
    z = _dot(h, win_ref[...])
    qkv_ref[...] = z[:, :3 * CA_WIDTH].astype(BF16)
    uv = jax.nn.gelu(z[:, 3 * CA_WIDTH:])
    for g_i in range(SGU_GROUPS):
        gs = slice(g_i * SGU_CH, (g_i + 1) * SGU_CH)
        u = uv[:, gs]
        gv = uv[:, SGU_WIDTH + g_i * SGU_CH:SGU_WIDTH + (g_i + 1) * SGU_CH]
        mu = jnp.mean(gv, axis=-1, keepdims=True)
        xc = gv - mu
        vn = xc * lax.rsqrt(jnp.mean(xc * xc, axis=-1, keepdims=True) + EPS) * ng_ref[:, gs] + nb_ref[:, gs]
        vn = vn.astype(BF16)
        w = sw_ref[g_i]
        for c_i in range(TM // SGU_CHUNK):
            rs = slice(c_i * SGU_CHUNK, (c_i + 1) * SGU_CHUNK)
            mixed = _dot(w, vn[rs, :]) + sb_ref[g_i]
            sg_ref[rs, gs] = (u[rs, :] * mixed).astype(BF16)


def _odd_in(x, g, win, ng, nb, sw, sb):
    t = x.shape[0]
    row = lambda w: pl.BlockSpec((TM, w), lambda i: (i, 0))
    return pl.pallas_call(
        _odd_in_body,
        grid=(t // TM,),
        in_specs=[row(D_MODEL), _const_spec(g.shape), _const_spec(win.shape), _const_spec(ng.shape),
                  _const_spec(nb.shape), _const_spec(sw.shape), _const_spec(sb.shape)],
        out_specs=[row(3 * CA_WIDTH), row(SGU_WIDTH)],
        out_shape=[jax.ShapeDtypeStruct((t, 3 * CA_WIDTH), BF16), jax.ShapeDtypeStruct((t, SGU_WIDTH), BF16)],
        compiler_params=_cparams(("parallel",)),
        name="odd_in",
    )(x, g, win, ng, nb, sw, sb)


BAND_BLOCKS = 3


def _band_body(q_ref, k0_ref, k1_ref, k2_ref, v0_ref, v1_ref, v2_ref, bias_ref, o_ref):
    qi = pl.program_id(2)
    lane = lax.broadcasted_iota(I32, (TQ, 2 * CA_HEAD_DIM), 1)
    low = lane < CA_HEAD_DIM
    q_pair = q_ref[...]
    k_refs = (k0_ref, k1_ref, k2_ref)
    v_refs = (v0_ref, v1_ref, v2_ref)
    outs = []
    for hh in range(2):
        q = jnp.where(low if hh == 0 else jnp.logical_not(low), q_pair, jnp.zeros_like(q_pair))
        s = []
        for jj in range(BAND_BLOCKS):
            before_start = qi + jj < BAND_BLOCKS - 1
            pen = jnp.where(before_start, NEG, 0.0)
            s.append(_dot_nt(q, k_refs[jj][...]) + bias_ref[hh, :, jj * TQ:(jj + 1) * TQ] + pen)
        m = functools.reduce(jnp.maximum, [jnp.max(x, axis=-1, keepdims=True) for x in s])
        p = [jnp.exp(x - m) for x in s]
        l = functools.reduce(lambda a, b: a + b, [jnp.sum(x, axis=-1, keepdims=True) for x in p])
        acc = functools.reduce(lambda a, b: a + b,
                               [_dot(p[jj].astype(BF16), v_refs[jj][...]) for jj in range(BAND_BLOCKS)])
        outs.append(acc / l)
    o_ref[...] = jnp.where(low, outs[0], outs[1]).astype(BF16)


def _band_attention(qkv, bias, batch, seq):
    t = qkv.shape[0]
    nq = seq // TQ
    pairs = CA_HEADS // 2
    w = 2 * CA_HEAD_DIM

    def kv_spec(jj, col0):
        return pl.BlockSpec((TQ, w), lambda b, p, i: (b * nq + jnp.maximum(i - (BAND_BLOCKS - 1) + jj, 0), col0 + p))

    return pl.pallas_call(
        _band_body,
        grid=(batch, pairs, nq),
        in_specs=[pl.BlockSpec((TQ, w), lambda b, p, i: (b * nq + i, p))]
                 + [kv_spec(jj, pairs) for jj in range(BAND_BLOCKS)]
                 + [kv_spec(jj, 2 * pairs) for jj in range(BAND_BLOCKS)]
                 + [pl.BlockSpec((2, TQ, BAND_BLOCKS * TQ), lambda b, p, i: (p, 0, 0))],
        out_specs=pl.BlockSpec((TQ, w), lambda b, p, i: (b * nq + i, p)),
        out_shape=jax.ShapeDtypeStruct((t, CA_WIDTH), BF16),
        compiler_params=_cparams(("parallel", "parallel", "arbitrary")),
        name="band_attention",
    )(qkv, qkv, qkv, qkv, qkv, qkv, qkv, bias)


ROUTE_LANES = 128


def _odd_out_body(x_ref, att_ref, sg_ref, wo_ref, g_ref, wr_ref, tri_ref, x1_ref, h_ref, route_ref, cnt_ref, run_ref):
    @pl.when(pl.program_id(0) == 0)
    def _():
        run_ref[...] = jnp.zeros_like(run_ref)

    y = x_ref[...] + _dot(att_ref[...], wo_ref[:CA_WIDTH, :]) + _dot(sg_ref[...], wo_ref[CA_WIDTH:, :])
    x1_ref[...] = y
    h = _rms(y, g_ref[...])
    h_ref[...] = h
    logits = jnp.dot(h, wr_ref[...], preferred_element_type=F32, precision=lax.Precision.HIGHEST)
    lane = lax.broadcasted_iota(I32, (TM, ROUTE_LANES), 1)
    logits = jnp.where(lane < N_EXPERTS, logits, NEG)
    m1 = jnp.max(logits, axis=-1, keepdims=True)
    e1 = jnp.min(jnp.where(logits == m1, lane, ROUTE_LANES), axis=-1, keepdims=True)
    rest = jnp.where(lane == e1, NEG, logits)
    m2 = jnp.max(rest, axis=-1, keepdims=True)
    e2 = jnp.min(jnp.where(rest == m2, lane, ROUTE_LANES), axis=-1, keepdims=True)
    ex = jnp.exp(m2 - m1)
    w1 = 1.0 / (1.0 + ex)
    w2 = ex / (1.0 + ex)
    hot1 = (lane == e1).astype(F32)
    hot2 = (lane == e2).astype(F32)
    sel = hot1 + hot2
    before = _dot(tri_ref[...], sel.astype(BF16)) + run_ref[0:1, :]
    rank1 = jnp.sum(hot1 * before, axis=-1, keepdims=True)
    rank2 = jnp.sum(hot2 * before, axis=-1, keepdims=True)
    rec = jnp.where(lane == 0, e1.astype(F32), 0.0)
    rec = jnp.where(lane == 1, e2.astype(F32), rec)
    rec = jnp.where(lane == 2, rank1, rec)
    rec = jnp.where(lane == 3, rank2, rec)
    rec = jnp.where(lane == 4, w1, rec)
    rec = jnp.where(lane == 5, w2, rec)
    route_ref[...] = rec
    total = run_ref[0:1, :] + jnp.sum(sel, axis=0, keepdims=True)
    run_ref[...] = jnp.broadcast_to(total, run_ref.shape)
    cnt_ref[...] = jnp.broadcast_to(total, cnt_ref.shape)


def _odd_out(x, att, sg, wo, g, wr, tri):
    t = x.shape[0]
    row = lambda w: pl.BlockSpec((TM, w), lambda i: (i, 0))
    return pl.pallas_call(
        _odd_out_body,
        grid=(t // TM,),
        in_specs=[row(D_MODEL), row(CA_WIDTH), row(SGU_WIDTH), _const_spec(wo.shape), _const_spec(g.shape),
                  _const_spec(wr.shape), _const_spec(tri.shape)],
        out_specs=[row(D_MODEL), row(D_MODEL), row(ROUTE_LANES), pl.BlockSpec((8, ROUTE_LANES), lambda i: (0, 0))],
        out_shape=[jax.ShapeDtypeStruct((t, D_MODEL), F32), jax.ShapeDtypeStruct((t, D_MODEL), F32),
                   jax.ShapeDtypeStruct((t, ROUTE_LANES), F32), jax.ShapeDtypeStruct((8, ROUTE_LANES), F32)],
        scratch_shapes=[pltpu.VMEM((8, ROUTE_LANES), F32)],
        compiler_params=_cparams(("arbitrary",)),
        name="odd_out_router",
    )(x, att, sg, wo, g, wr, tri)


def _moe_body(tile_e_ref, tile_cnt_ref, dst_ref, h_hbm, wg_ref, wu_ref, wd_ref, o_hbm,
              xf_ref, xb_ref, acc_ref, gsem, ssem, *, tokens, n_ff):
    i = pl.program_id(0)
    j = pl.program_id(1)
    cnt = tile_cnt_ref[i]
    base = i * MOE_TM

    def gather_copy(r):
        tok = dst_ref[base + r] % tokens
        return pltpu.make_async_copy(h_hbm.at[pl.ds(tok, 1), :], xf_ref.at[pl.ds(r, 1), :], gsem)

    def scatter_copy(r):
        return pltpu.make_async_copy(acc_ref.at[pl.ds(r, 1), :], o_hbm.at[pl.ds(dst_ref[base + r], 1), :], ssem)

    @pl.when(jnp.logical_and(i == 0, j == 0))
    def _():
        xf_ref[...] = jnp.zeros_like(xf_ref)

    @pl.when(cnt > 0)
    def _():
        @pl.when(j == 0)
        def _():
            lax.fori_loop(0, cnt, lambda r, c: (gather_copy(r).start(), c)[1], 0)
            lax.fori_loop(0, cnt, lambda r, c: (gather_copy(r).wait(), c)[1], 0)
            xb_ref[...] = xf_ref[...].astype(BF16)
            acc_ref[...] = jnp.zeros_like(acc_ref)

        xb = xb_ref[...]
        gate = _dot(xb, wg_ref[0])
        up = _dot(xb, wu_ref[0])
        act = (gate * jax.nn.sigmoid(gate) * up).astype(BF16)
        acc_ref[...] += _dot(act, wd_ref[0])

        @pl.when(j == n_ff - 1)
        def _():
            lax.fori_loop(0, cnt, lambda r, c: (scatter_copy(r).start(), c)[1], 0)
            lax.fori_loop(0, cnt, lambda r, c: (scatter_copy(r).wait(), c)[1], 0)


def _moe(h, tile_e, tile_cnt, dst, wg, wu, wd):
    tokens = h.shape[0]
    n_tiles = tile_e.shape[0]
    n_ff = EXPERT_FF // MOE_TF

    def ff_idx(i, j, tile_cnt_ref):
        return jnp.where(tile_cnt_ref[i] > 0, j, n_ff - 1)

    grid_spec = pltpu.PrefetchScalarGridSpec(
        num_scalar_prefetch=3,
        grid=(n_tiles, n_ff),
        in_specs=[pl.BlockSpec(memory_space=pl.ANY),
                  pl.BlockSpec((1, D_MODEL, MOE_TF), lambda i, j, te, tc, ds: (te[i], 0, ff_idx(i, j, tc))),
                  pl.BlockSpec((1, D_MODEL, MOE_TF), lambda i, j, te, tc, ds: (te[i], 0, ff_idx(i, j, tc))),
                  pl.BlockSpec((1, MOE_TF, D_MODEL), lambda i, j, te, tc, ds: (te[i], ff_idx(i, j, tc), 0))],
        out_specs=pl.BlockSpec(memory_space=pl.ANY),
        scratch_shapes=[pltpu.VMEM((MOE_TM, D_MODEL), F32), pltpu.VMEM((MOE_TM, D_MODEL), BF16),
                        pltpu.VMEM((MOE_TM, D_MODEL), F32), pltpu.SemaphoreType.DMA, pltpu.SemaphoreType.DMA],
    )
    return pl.pallas_call(
        functools.partial(_moe_body, tokens=tokens, n_ff=n_ff),
        grid_spec=grid_spec,
        out_shape=jax.ShapeDtypeStruct((2 * tokens, D_MODEL), F32),
        compiler_params=_cparams(("arbitrary", "arbitrary")),
        name="moe_experts",
    )(tile_e, tile_cnt, dst, h, wg, wu, wd)


def _moe_plan(route, counts, tokens):
    n_tiles = 2 * tokens // MOE_TM + N_EXPERTS
    cnt = counts[0, :N_EXPERTS].astype(I32)
    e = route[:, 0:2].astype(I32)
    rank = route[:, 2:4].astype(I32)
    tiles_per_e = (cnt + MOE_TM - 1) // MOE_TM
    tile_end = jnp.cumsum(tiles_per_e)
    tile_start = tile_end - tiles_per_e
    pos = (tile_start * MOE_TM)[e] + rank
    slot_row = jnp.arange(tokens, dtype=I32)[:, None] + jnp.array([0, tokens], I32)[None, :]
    dst = jnp.zeros((n_tiles * MOE_TM,), I32).at[pos.reshape(-1)].set(slot_row.reshape(-1))
    tile = jnp.arange(n_tiles, dtype=I32)
    n_used = tile_end[-1]
    last = jnp.maximum(n_used - 1, 0)
    tile_e = jnp.searchsorted(tile_end, jnp.minimum(tile, last), side="right").astype(I32)
    tile_e = jnp.minimum(tile_e, N_EXPERTS - 1)
    tile_cnt = jnp.clip(cnt[tile_e] - (tile - tile_start[tile_e]) * MOE_TM, 0, MOE_TM)
    tile_cnt = jnp.where(tile < n_used, tile_cnt, 0).astype(I32)
    return tile_e, tile_cnt, dst


def _combine_body(x_ref, oa_ref, ob_ref, route_ref, g_ref, o_ref, *, final):
    w1 = route_ref[:, 4:5]
    w2 = route_ref[:, 5:6]
    y = x_ref[...] + (w1 * oa_ref[...] + w2 * ob_ref[...])
    o_ref[...] = _rms(y, g_ref[...]) if final else y


def _combine(x, o2, route, g, final):
    t = x.shape[0]
    nt = t // TM
    return pl.pallas_call(
        functools.partial(_combine_body, final=final),
        grid=(nt,),
        in_specs=[pl.BlockSpec((TM, D_MODEL), lambda i: (i, 0)), pl.BlockSpec((TM, D_MODEL), lambda i: (i, 0)),
                  pl.BlockSpec((TM, D_MODEL), lambda i: (i + nt, 0)), pl.BlockSpec((TM, ROUTE_LANES), lambda i: (i, 0)),
                  _const_spec(g.shape)],
        out_specs=pl.BlockSpec((TM, D_MODEL), lambda i: (i, 0)),
        out_shape=jax.ShapeDtypeStruct((t, D_MODEL), F32),
        compiler_params=_cparams(("parallel",)),
        name="moe_combine",
    )(x, o2, o2, route, g)


def _rot_half(w):
    half = w.shape[-1] // 2
    return jnp.concatenate([-w[..., half:], w[..., :half]], axis=-1)


def _rope_tables(seq):
    inv = 1.0 / (ROPE_THETA ** (jnp.arange(0, MLA_ROPE, 2, dtype=F32) / MLA_ROPE))
    ang = jnp.arange(seq, dtype=F32)[:, None] * inv[None, :]
    cos, sin = jnp.cos(ang), jnp.sin(ang)
    pad = HEAD_PAD - MLA_NOPE - MLA_ROPE
    c = jnp.concatenate([jnp.ones((seq, MLA_NOPE), F32), cos, cos, jnp.zeros((seq, pad), F32)], axis=1)
    s = jnp.concatenate([jnp.zeros((seq, MLA_NOPE), F32), sin, sin, jnp.zeros((seq, pad), F32)], axis=1)
    scale = (MLA_NOPE + MLA_ROPE) ** -0.5
    return c * scale, s * scale, c, s


def _even_weights(w_in, w_q_up, w_kv_up):
    d = w_in.shape[0]
    pad = HEAD_PAD - MLA_NOPE - MLA_ROPE
    kr = w_in[:, POOL_WIDTH + MLA_Q_RANK + MLA_KV_RANK:]
    place = lambda m: jnp.concatenate([jnp.zeros((d, MLA_NOPE), F32), m, jnp.zeros((d, pad), F32)], axis=1)
    win = jnp.concatenate([w_in[:, :POOL_WIDTH + MLA_Q_RANK + MLA_KV_RANK], place(kr), place(_rot_half(kr))], axis=1)
    wq = w_q_up.reshape(MLA_Q_RANK, MLA_HEADS, MLA_NOPE + MLA_ROPE)
    nope, rope = wq[..., :MLA_NOPE], wq[..., MLA_NOPE:]
    zq = lambda n: jnp.zeros((MLA_Q_RANK, MLA_HEADS, n), F32)
    wq_main = jnp.concatenate([nope, rope, zq(pad)], axis=-1).reshape(MLA_Q_RANK, -1)
    wq_rot = jnp.concatenate([zq(MLA_NOPE), _rot_half(rope), zq(pad)], axis=-1).reshape(MLA_Q_RANK, -1)
    wkv = w_kv_up.reshape(MLA_KV_RANK, MLA_HEADS, MLA_NOPE + MLA_V)
    k_pad = jnp.concatenate([wkv[..., :MLA_NOPE], jnp.zeros((MLA_KV_RANK, MLA_HEADS, HEAD_PAD - MLA_NOPE), F32)],
                            axis=-1).reshape(MLA_KV_RANK, -1)
    v_c = wkv[..., MLA_NOPE:].reshape(MLA_KV_RANK, -1)
    return (win.astype(BF16), jnp.concatenate([wq_main, wq_rot], axis=1).astype(BF16),
            jnp.concatenate([k_pad, v_c], axis=1).astype(BF16))


def _band_bias(rel_bias):
    band = BAND_BLOCKS * TQ
    qpos = jnp.arange(TQ)[:, None] + (BAND_BLOCKS - 1) * TQ
    kpos = jnp.arange(band)[None, :]
    rel = qpos - kpos
    bias = rel_bias[:, jnp.clip(rel, -REL_CLIP, REL_CLIP) + REL_CLIP].astype(F32)
    dc = qpos // CHUNK - kpos // CHUNK
    valid = jnp.logical_and(dc >= 0, dc <= CA_LEFT_CHUNKS)
    return jnp.where(valid[None], bias, NEG)


def kernel(x, ev_norm_mix, ev_w_in, ev_pool_w, ev_pool_scale, ev_q_norm, ev_w_q_up, ev_kv_norm, ev_w_kv_up, ev_w_out, ev_norm_ffn, ev_ffn_w_gate, ev_ffn_w_up, ev_ffn_w_down, od_norm_mix, od_w_in, od_rel_bias, od_sgu_norm_g, od_sgu_norm_b, od_sgu_w, od_sgu_b, od_w_out, od_norm_ffn, od_router, od_moe_w_gate, od_moe_w_up, od_moe_w_down, final_norm):
    batch, seq, d = x.shape
    tokens = batch * seq
    assert d == D_MODEL and seq % TM == 0 and TM % SGU_CHUNK == 0 and TM % TQ == 0
    depth = ev_w_in.shape[0] + od_w_in.shape[0]
    xt = x.reshape(tokens, d)
    tabs = _rope_tables(seq)
    tri = jnp.tril(jnp.ones((TM, TM), F32), k=-1).astype(BF16)
    causal = jnp.tril(jnp.ones((SGU_CHUNK, SGU_CHUNK), bool))
    vec = lambda v: v.reshape(1, -1).astype(F32)

    for layer in range(depth):
        i = layer // 2
        if layer % 2 == 0:
            win, wq, wkv = _even_weights(ev_w_in[i], ev_w_q_up[i], ev_w_kv_up[i])
            a, q, k, v = _even_in(xt, vec(ev_norm_mix[i]), win, vec(ev_q_norm[i]), wq, vec(ev_kv_norm[i]), wkv,
                                  tabs, seq)
            att = _mla_attention(q, k, v, batch, seq)
            xt = _even_out(xt, a, att, ev_pool_w[i].astype(BF16), vec(ev_pool_scale[i]), ev_w_out[i].astype(BF16),
                           vec(ev_norm_ffn[i]), ev_ffn_w_gate[i].astype(BF16), ev_ffn_w_up[i].astype(BF16),
                           ev_ffn_w_down[i].astype(BF16), seq)
        else:
            w_in = od_w_in[i]
            win = jnp.concatenate([w_in[:, :CA_WIDTH] * (CA_HEAD_DIM ** -0.5), w_in[:, CA_WIDTH:]], axis=1).astype(BF16)
            sw = jnp.where(causal[None], od_sgu_w[i], 0.0).astype(BF16)
            sb = jnp.broadcast_to(od_sgu_b[i][:, :, None], (SGU_GROUPS, SGU_CHUNK, SGU_CH)).astype(F32)
            qkv, sg = _odd_in(xt, vec(od_norm_mix[i]), win, vec(od_sgu_norm_g[i]), vec(od_sgu_norm_b[i]), sw, sb)
            att = _band_attention(qkv, _band_bias(od_rel_bias[i]), batch, seq)
            wr = jnp.concatenate([od_router[i], jnp.zeros((d, ROUTE_LANES - N_EXPERTS), F32)], axis=1)
            x1, h, route, counts = _odd_out(xt, att, sg, od_w_out[i].astype(BF16), vec(od_norm_ffn[i]), wr, tri)
            tile_e, tile_cnt, dst = _moe_plan(route, counts, tokens)
            o2 = _moe(h, tile_e, tile_cnt, dst, od_moe_w_gate[i].astype(BF16), od_moe_w_up[i].astype(BF16),
                      od_moe_w_down[i].astype(BF16))
            last = layer == depth - 1
            xt = _combine(x1, o2, route, vec(final_norm), last)
    return xt.reshape(batch, seq, d)
```

```python
import functools
import math

import jax
import jax.numpy as jnp
from jax import lax
from jax.experimental import pallas as pl
from jax.experimental.pallas import tpu as pltpu

F32 = jnp.float32
BF16 = jnp.bfloat16
I32 = jnp.int32

D_MODEL = 1024
CHUNK = 64
EPS = 1e-6

POOL_GROUPS = 4
POOL_CH = 128
POOL_WIDTH = 512
POOL_WINDOWS = (2, 4, 8, 16)
POOL_HALO = 16

MLA_HEADS = 8
MLA_NOPE = 64
MLA_ROPE = 32
MLA_V = 64
MLA_Q_RANK = 256
MLA_KV_RANK = 128
ROPE_THETA = 10000.0
HEAD_PAD = 128

CA_HEADS = 8
CA_HEAD_DIM = 64
CA_LEFT_CHUNKS = 8
REL_CLIP = 128
CA_WIDTH = CA_HEADS * CA_HEAD_DIM

SGU_GROUPS = 4
SGU_CH = 128
SGU_WIDTH = 512
SGU_CHUNK = 128

D_FF = 2816
N_EXPERTS = 8
EXPERT_FF = 3584

NEG = -1e30

TM = 512
TQ = 256
TK = 512
MOE_TF = 512
MOE_FF_STEPS = EXPERT_FF // MOE_TF
MOE_TM = 80 * MOE_FF_STEPS
VMEM_LIMIT = 56 * 1024 * 1024


def _cparams(sem):
    return pltpu.CompilerParams(dimension_semantics=sem, vmem_limit_bytes=VMEM_LIMIT)


def _rms(x, g):
    return x * lax.rsqrt(jnp.mean(x * x, axis=-1, keepdims=True) + EPS) * g


def _dot(a, b):
    return jnp.dot(a, b, preferred_element_type=F32)


def _dot_nt(a, b):
    return lax.dot_general(a, b, (((1,), (1,)), ((), ())), preferred_element_type=F32)


def _const_spec(shape):
    nd = len(shape)
    return pl.BlockSpec(shape, lambda *_: (0,) * nd, pipeline_mode=pl.Buffered(1))


def _even_in_body(x_ref, g_ref, win_ref, qn_ref, wq_ref, kvn_ref, wkv_ref, cq_ref, sq_ref, ck_ref, sk_ref,
                  a_ref, q_ref, k_ref, v_ref):
    h = _rms(x_ref[...], g_ref[...]).astype(BF16)
    z = _dot(h, win_ref[...])
    a_ref[...] = z[:, :POOL_WIDTH]
    o = POOL_WIDTH
    cq = z[:, o:o + MLA_Q_RANK]
    o += MLA_Q_RANK
    ckv = z[:, o:o + MLA_KV_RANK]
    o += MLA_KV_RANK
    kr = z[:, o:o + HEAD_PAD]
    kr_rot = z[:, o + HEAD_PAD:o + 2 * HEAD_PAD]

    hw = MLA_HEADS * HEAD_PAD
    qq = _dot(_rms(cq, qn_ref[...]).astype(BF16), wq_ref[...])
    cq_t, sq_t = cq_ref[...], sq_ref[...]
    for h_i in range(MLA_HEADS):
        sl = slice(h_i * HEAD_PAD, (h_i + 1) * HEAD_PAD)
        sl_rot = slice(hw + h_i * HEAD_PAD, hw + (h_i + 1) * HEAD_PAD)
        q_ref[:, sl] = (qq[:, sl] * cq_t + qq[:, sl_rot] * sq_t).astype(BF16)

    kv = _dot(_rms(ckv, kvn_ref[...]).astype(BF16), wkv_ref[...])
    k_rope = kr * ck_ref[...] + kr_rot * sk_ref[...]
    for h_i in range(MLA_HEADS):
        sl = slice(h_i * HEAD_PAD, (h_i + 1) * HEAD_PAD)
        k_ref[:, sl] = (kv[:, sl] + k_rope).astype(BF16)
    v_ref[...] = kv[:, hw:].astype(BF16)


def _even_in(x, g, win, qn, wq, kvn, wkv, tabs, seq):
    t = x.shape[0]
    hw = MLA_HEADS * HEAD_PAD
    row = lambda w: pl.BlockSpec((TM, w), lambda i: (i, 0))
    tab = pl.BlockSpec((TM, HEAD_PAD), lambda i: (i % (seq // TM), 0))
    return pl.pallas_call(
        _even_in_body,
        grid=(t // TM,),
        in_specs=[row(D_MODEL), _const_spec(g.shape), _const_spec(win.shape), _const_spec(qn.shape),
                  _const_spec(wq.shape), _const_spec(kvn.shape), _const_spec(wkv.shape), tab, tab, tab, tab],
        out_specs=[row(POOL_WIDTH), row(hw), row(hw), row(MLA_HEADS * MLA_V)],
        out_shape=[jax.ShapeDtypeStruct((t, POOL_WIDTH), F32), jax.ShapeDtypeStruct((t, hw), BF16),
                   jax.ShapeDtypeStruct((t, hw), BF16), jax.ShapeDtypeStruct((t, MLA_HEADS * MLA_V), BF16)],
        compiler_params=_cparams(("parallel",)),
        name="even_in",
    )(x, g, win, qn, wq, kvn, wkv, *tabs)


def _mla_body(q_ref, k_ref, v_ref, o_ref, s_ref, mx_ref, l_ref, acc_ref):
    qi = pl.program_id(2)
    last = (qi * TQ) // TK
    halves = (slice(0, HEAD_PAD), slice(HEAD_PAD, 2 * HEAD_PAD))
    lane_blocks = [slice(c * 128, (c + 1) * 128) for c in range(TK // 128)]
    q_chunk = lax.broadcasted_iota(I32, (TQ, TK), 0) // CHUNK + qi * (TQ // CHUNK)
    k_chunk = lax.broadcasted_iota(I32, (TQ, TK), 1) // CHUNK + last * (TK // CHUNK)
    visible = k_chunk <= q_chunk

    mx_ref[...] = jnp.full(mx_ref.shape, NEG, F32)
    l_ref[...] = jnp.zeros_like(l_ref)
    acc_ref[...] = jnp.zeros_like(acc_ref)

    def scores(j, masked):
        rows = pl.ds(pl.multiple_of(j * TK, TK), TK)
        for hh, hs in enumerate(halves):
            s = _dot_nt(q_ref[:, hs], k_ref[rows, hs])
            if masked:
                s = jnp.where(visible, s, NEG)
            s_ref[hh, j] = s
            mx_ref[hh] = functools.reduce(jnp.maximum, [s[:, lb] for lb in lane_blocks], mx_ref[hh])

    lax.fori_loop(0, last, lambda j, c: (scores(j, False), c)[1], 0)
    scores(last, True)

    for hh in range(2):
        m = jnp.max(mx_ref[hh], axis=-1, keepdims=True)
        mx_ref[hh] = jnp.broadcast_to(m, (TQ, 128))

    def weighted(j, c):
        rows = pl.ds(pl.multiple_of(j * TK, TK), TK)
        for hh in range(2):
            m = mx_ref[hh]
            p = [jnp.exp2(s_ref[hh, j, :, lb] - m) for lb in lane_blocks]
            l_ref[hh] += functools.reduce(lambda a, b: a + b, p)
            p = jnp.concatenate(p, axis=1).astype(BF16)
            acc_ref[hh] += _dot(p, v_ref[rows, :])
        return c

    lax.fori_loop(0, last + 1, weighted, 0)

    outs = [acc_ref[hh] / jnp.sum(l_ref[hh], axis=-1, keepdims=True) for hh in range(2)]
    lane = lax.broadcasted_iota(I32, (TQ, 2 * MLA_V), 1)
    o_ref[...] = jnp.where(lane < MLA_V, outs[0], outs[1]).astype(BF16)


def _mla_attention(q, k, v, batch, seq):
    t = q.shape[0]
    nq = seq // TQ
    pairs = MLA_HEADS // 2
    return pl.pallas_call(
        _mla_body,
        grid=(batch, pairs, nq),
        in_specs=[pl.BlockSpec((TQ, 2 * HEAD_PAD), lambda b, p, i: (b * nq + i, p)),
                  pl.BlockSpec((seq, 2 * HEAD_PAD), lambda b, p, i: (b, p)),
                  pl.BlockSpec((seq, 2 * MLA_V), lambda b, p, i: (b, p))],
        out_specs=pl.BlockSpec((TQ, 2 * MLA_V), lambda b, p, i: (b * nq + i, p)),
        out_shape=jax.ShapeDtypeStruct((t, MLA_HEADS * MLA_V), BF16),
        scratch_shapes=[pltpu.VMEM((2, seq // TK, TQ, TK), F32), pltpu.VMEM((2, TQ, 128), F32),
                        pltpu.VMEM((2, TQ, 128), F32), pltpu.VMEM((2, TQ, 2 * MLA_V), F32)],
        compiler_params=_cparams(("parallel", "parallel", "arbitrary")),
        name="mla_attention",
    )(q, k, v)


def _even_out_body(x_ref, a_ref, halo_ref, att_ref, pw_ref, ps_ref, wo_ref, g_ref, wg_ref, wu_ref, wd_ref,
                   o_ref, ext_ref, *, seq):
    pos0 = (pl.program_id(0) * TM) % seq
    ext_ref[0:POOL_HALO, :] = jnp.where(pos0 == 0, 0.0, halo_ref[...])
    ext_ref[POOL_HALO:, :] = a_ref[...]
    pos = lax.broadcasted_iota(I32, (TM, 1), 0) + pos0
    y = x_ref[...] + _dot(att_ref[...], wo_ref[POOL_WIDTH:, :])
    for g_i, win in enumerate(POOL_WINDOWS):
        gs = slice(g_i * POOL_CH, (g_i + 1) * POOL_CH)
        a_g = ext_ref[POOL_HALO:, gs]
        ssum = a_g
        for j in range(1, win):
            ssum = ssum + ext_ref[POOL_HALO - j:POOL_HALO - j + TM, gs]
        count = jnp.minimum(pos + 1, win).astype(F32)
        pooled = (ssum / count - a_g).astype(BF16)
        mixed = (_dot(pooled, pw_ref[g_i]) * ps_ref[:, gs]).astype(BF16)
        y = y + _dot(mixed, wo_ref[gs, :])
    h = _rms(y, g_ref[...]).astype(BF16)
    gate = _dot(h, wg_ref[...])
    up = _dot(h, wu_ref[...])
    act = (gate * jax.nn.sigmoid(gate) * up).astype(BF16)
    o_ref[...] = y + _dot(act, wd_ref[...])


def _even_out(x, a, att, pw, ps, wo, g, wg, wu, wd, seq):
    t = x.shape[0]
    row = lambda w: pl.BlockSpec((TM, w), lambda i: (i, 0))
    halo = pl.BlockSpec((POOL_HALO, POOL_WIDTH), lambda i: (jnp.maximum(i * (TM // POOL_HALO) - 1, 0), 0))
    return pl.pallas_call(
        functools.partial(_even_out_body, seq=seq),
        grid=(t // TM,),
        in_specs=[row(D_MODEL), row(POOL_WIDTH), halo, row(MLA_HEADS * MLA_V), _const_spec(pw.shape),
                  _const_spec(ps.shape), _const_spec(wo.shape), _const_spec(g.shape), _const_spec(wg.shape),
                  _const_spec(wu.shape), _const_spec(wd.shape)],
        out_specs=row(D_MODEL),
        out_shape=jax.ShapeDtypeStruct((t, D_MODEL), F32),
        scratch_shapes=[pltpu.VMEM((TM + POOL_HALO, POOL_WIDTH), F32)],
        compiler_params=_cparams(("parallel",)),
        name="even_out",
    )(x, a, a, att, pw, ps, wo, g, wg, wu, wd)


def _odd_in_body(x_ref, g_ref, win_ref, ng_ref, nb_ref, sw_ref, sb_ref, qkv_ref, sg_ref):
    h = _rms(x_ref[...], g_ref[...]).astype(BF16)
    z = _dot(h, win_ref[...])
    qkv_ref[...] = z[:, :3 * CA_WIDTH].astype(BF16)
    uv = jax.nn.gelu(z[:, 3 * CA_WIDTH:])
    for g_i in range(SGU_GROUPS):
        gs = slice(g_i * SGU_CH, (g_i + 1) * SGU_CH)
        u = uv[:, gs]
        gv = uv[:, SGU_WIDTH + g_i * SGU_CH:SGU_WIDTH + (g_i + 1) * SGU_CH]
        mu = jnp.mean(gv, axis=-1, keepdims=True)
        xc = gv - mu
        vn = xc * lax.rsqrt(jnp.mean(xc * xc, axis=-1, keepdims=True) + EPS) * ng_ref[:, gs] + nb_ref[:, gs]
        vn = vn.astype(BF16)
        w = sw_ref[g_i]
        for c_i in range(TM // SGU_CHUNK):
            rs = slice(c_i * SGU_CHUNK, (c_i + 1) * SGU_CHUNK)
            mixed = _dot(w, vn[rs, :]) + sb_ref[g_i]
            sg_ref[rs, gs] = (u[rs, :] * mixed).astype(BF16)


def _odd_in(x, g, win, ng, nb, sw, sb):
    t = x.shape[0]
    row = lambda w: pl.BlockSpec((TM, w), lambda i: (i, 0))
    return pl.pallas_call(
        _odd_in_body,
        grid=(t // TM,),
        in_specs=[row(D_MODEL), _const_spec(g.shape), _const_spec(win.shape), _const_spec(ng.shape),
                  _const_spec(nb.shape), _const_spec(sw.shape), _const_spec(sb.shape)],
        out_specs=[row(3 * CA_WIDTH), row(SGU_WIDTH)],
        out_shape=[jax.ShapeDtypeStruct((t, 3 * CA_WIDTH), BF16), jax.ShapeDtypeStruct((t, SGU_WIDTH), BF16)],
        compiler_params=_cparams(("parallel",)),
        name="odd_in",
    )(x, g, win, ng, nb, sw, sb)


BAND_BLOCKS = 3


def _band_body(q_ref, k0_ref, k1_ref, k2_ref, v0_ref, v1_ref, v2_ref, bias_ref, o_ref):
    qi = pl.program_id(2)
    lane = lax.broadcasted_iota(I32, (TQ, 2 * CA_HEAD_DIM), 1)
    low = lane < CA_HEAD_DIM
    q_pair = q_ref[...]
    k_refs = (k0_ref, k1_ref, k2_ref)
    v_refs = (v0_ref, v1_ref, v2_ref)
    outs = []
    for hh in range(2):
        q = jnp.where(low if hh == 0 else jnp.logical_not(low), q_pair, jnp.zeros_like(q_pair))
        s = []
        for jj in range(BAND_BLOCKS):
            before_start = qi + jj < BAND_BLOCKS - 1
            pen = jnp.where(before_start, NEG, 0.0)
            s.append(_dot_nt(q, k_refs[jj][...]) + bias_ref[hh, :, jj * TQ:(jj + 1) * TQ] + pen)
        m = functools.reduce(jnp.maximum, [jnp.max(x, axis=-1, keepdims=True) for x in s])
        p = [jnp.exp(x - m) for x in s]
        l = functools.reduce(lambda a, b: a + b, [jnp.sum(x, axis=-1, keepdims=True) for x in p])
        acc = functools.reduce(lambda a, b: a + b,
                               [_dot(p[jj].astype(BF16), v_refs[jj][...]) for jj in range(BAND_BLOCKS)])
        outs.append(acc / l)
    o_ref[...] = jnp.where(low, outs[0], outs[1]).astype(BF16)


def _band_attention(qkv, bias, batch, seq):
    t = qkv.shape[0]
    nq = seq // TQ
    pairs = CA_HEADS // 2
    w = 2 * CA_HEAD_DIM

    def kv_spec(jj, col0):
        return pl.BlockSpec((TQ, w), lambda b, p, i: (b * nq + jnp.maximum(i - (BAND_BLOCKS - 1) + jj, 0), col0 + p))

    return pl.pallas_call(
        _band_body,
        grid=(batch, pairs, nq),
        in_specs=[pl.BlockSpec((TQ, w), lambda b, p, i: (b * nq + i, p))]
                 + [kv_spec(jj, pairs) for jj in range(BAND_BLOCKS)]
                 + [kv_spec(jj, 2 * pairs) for jj in range(BAND_BLOCKS)]
                 + [pl.BlockSpec((2, TQ, BAND_BLOCKS * TQ), lambda b, p, i: (p, 0, 0))],
        out_specs=pl.BlockSpec((TQ, w), lambda b, p, i: (b * nq + i, p)),
        out_shape=jax.ShapeDtypeStruct((t, CA_WIDTH), BF16),
        compiler_params=_cparams(("parallel", "parallel", "arbitrary")),
        name="band_attention",
    )(qkv, qkv, qkv, qkv, qkv, qkv, qkv, bias)


ROUTE_LANES = 128


def _odd_out_body(x_ref, att_ref, sg_ref, wo_ref, g_ref, wr_ref, tri_ref, x1_ref, h_ref, route_ref, cnt_ref, run_ref):
    @pl.when(pl.program_id(0) == 0)
    def _():
        run_ref[...] = jnp.zeros_like(run_ref)

    y = x_ref[...] + _dot(att_ref[...], wo_ref[:CA_WIDTH, :]) + _dot(sg_ref[...], wo_ref[CA_WIDTH:, :])
    x1_ref[...] = y
    h = _rms(y, g_ref[...])
    h_ref[...] = h
    logits = jnp.dot(h, wr_ref[...], preferred_element_type=F32, precision=lax.Precision.HIGHEST)
    lane = lax.broadcasted_iota(I32, (TM, ROUTE_LANES), 1)
    logits = jnp.where(lane < N_EXPERTS, logits, NEG)
    m1 = jnp.max(logits, axis=-1, keepdims=True)
    e1 = jnp.min(jnp.where(logits == m1, lane, ROUTE_LANES), axis=-1, keepdims=True)
    rest = jnp.where(lane == e1, NEG, logits)
    m2 = jnp.max(rest, axis=-1, keepdims=True)
    e2 = jnp.min(jnp.where(rest == m2, lane, ROUTE_LANES), axis=-1, keepdims=True)
    ex = jnp.exp(m2 - m1)
    w1 = 1.0 / (1.0 + ex)
    w2 = ex / (1.0 + ex)
    hot1 = (lane == e1).astype(F32)
    hot2 = (lane == e2).astype(F32)
    sel = hot1 + hot2
    before = _dot(tri_ref[...], sel.astype(BF16)) + run_ref[0:1, :]
    rank1 = jnp.sum(hot1 * before, axis=-1, keepdims=True)
    rank2 = jnp.sum(hot2 * before, axis=-1, keepdims=True)
    rec = jnp.where(lane == 0, e1.astype(F32), 0.0)
    rec = jnp.where(lane == 1, e2.astype(F32), rec)
    rec = jnp.where(lane == 2, rank1, rec)
    rec = jnp.where(lane == 3, rank2, rec)
    rec = jnp.where(lane == 4, w1, rec)
    rec = jnp.where(lane == 5, w2, rec)
    route_ref[...] = rec
    total = run_ref[0:1, :] + jnp.sum(sel, axis=0, keepdims=True)
    run_ref[...] = jnp.broadcast_to(total, run_ref.shape)
    cnt_ref[...] = jnp.broadcast_to(total, cnt_ref.shape)


def _odd_out(x, att, sg, wo, g, wr, tri):
    t = x.shape[0]
    row = lambda w: pl.BlockSpec((TM, w), lambda i: (i, 0))
    return pl.pallas_call(
        _odd_out_body,
        grid=(t // TM,),
        in_specs=[row(D_MODEL), row(CA_WIDTH), row(SGU_WIDTH), _const_spec(wo.shape), _const_spec(g.shape),
                  _const_spec(wr.shape), _const_spec(tri.shape)],
        out_specs=[row(D_MODEL), row(D_MODEL), row(ROUTE_LANES), pl.BlockSpec((8, ROUTE_LANES), lambda i: (0, 0))],
        out_shape=[jax.ShapeDtypeStruct((t, D_MODEL), F32), jax.ShapeDtypeStruct((t, D_MODEL), F32),
                   jax.ShapeDtypeStruct((t, ROUTE_LANES), F32), jax.ShapeDtypeStruct((8, ROUTE_LANES), F32)],
        scratch_shapes=[pltpu.VMEM((8, ROUTE_LANES), F32)],
        compiler_params=_cparams(("arbitrary",)),
        name="odd_out_router",
    )(x, att, sg, wo, g, wr, tri)


def _moe_body(tile_e_ref, n_used_ref, src_ref, dst_ref, h_hbm, wg_ref, wu_ref, wd_ref, o_hbm,
              xf_ref, xb_ref, acc_ref, y_ref, gsem, ssem):
    i = pl.program_id(0)
    j = pl.program_id(1)
    n_used = n_used_ref[0]
    rows_per_step = MOE_TM // MOE_FF_STEPS

    def gather_copy(tile, r, slot):
        return pltpu.make_async_copy(h_hbm.at[pl.ds(src_ref[tile * MOE_TM + r], 1), :],
                                     xf_ref.at[slot, pl.ds(r, 1), :], gsem)

    def scatter_copy(tile, r, slot):
        return pltpu.make_async_copy(y_ref.at[slot, pl.ds(r, 1), :],
                                     o_hbm.at[pl.ds(dst_ref[tile * MOE_TM + r], 1), :], ssem)

    def wait_tile_gather(slot):
        pltpu.make_async_copy(h_hbm.at[pl.ds(0, MOE_TM), :], xf_ref.at[slot], gsem).wait()

    def wait_tile_scatter(slot):
        pltpu.make_async_copy(y_ref.at[slot], o_hbm.at[pl.ds(0, MOE_TM), :], ssem).wait()

    def start_all(copy_fn):
        lax.fori_loop(0, MOE_TM, lambda r, c: (copy_fn(r).start(), c)[1], 0)

    @pl.when(i < n_used)
    def _():
        cur = lax.rem(i, 2)
        other = 1 - cur
        nxt_tile = jnp.minimum(i + 1, n_used - 1)
        prev_tile = jnp.maximum(i - 1, 0)

        @pl.when(jnp.logical_and(i == 0, j == 0))
        def _():
            y_ref[...] = jnp.zeros_like(y_ref)
            spare0 = o_hbm.shape[0] - N_EXPERTS * MOE_TM
            for e_i in range(N_EXPERTS):
                fill = pltpu.make_async_copy(y_ref.at[0], o_hbm.at[pl.ds(spare0 + e_i * MOE_TM, MOE_TM), :], ssem)
                fill.start()
                fill.wait()
            start_all(lambda r: gather_copy(0, r, 0))

        @pl.when(j == 0)
        def _():
            wait_tile_gather(cur)
            xb_ref[...] = xf_ref[cur].astype(BF16)
            acc_ref[...] = jnp.zeros_like(acc_ref)

        base = j * rows_per_step
        for rr in range(rows_per_step):
            gather_copy(nxt_tile, base + rr, other).start()
            scatter_copy(prev_tile, base + rr, other).start()
        xb = xb_ref[...]
        gate = _dot(xb, wg_ref[0])
        up = _dot(xb, wu_ref[0])
        act = (gate * jax.nn.sigmoid(gate) * up).astype(BF16)
        acc_ref[...] += _dot(act, wd_ref[0])

        @pl.when(j == MOE_FF_STEPS - 1)
        def _():
            wait_tile_scatter(other)
            y_ref[cur] = acc_ref[...]

        @pl.when(jnp.logical_and(i == n_used - 1, j == MOE_FF_STEPS - 1))
        def _():
            start_all(lambda r: scatter_copy(i, r, cur))
            wait_tile_scatter(cur)
            wait_tile_gather(other)


def _moe(h, tile_e, n_used, src, dst, wg, wu, wd):
    tokens = h.shape[0]
    n_tiles = tile_e.shape[0]
    last = MOE_FF_STEPS - 1

    def ff_idx(i, j, n_used_ref):
        return jnp.where(i < n_used_ref[0], j, last)

    grid_spec = pltpu.PrefetchScalarGridSpec(
        num_scalar_prefetch=4,
        grid=(n_tiles, MOE_FF_STEPS),
        in_specs=[pl.BlockSpec(memory_space=pl.ANY),
                  pl.BlockSpec((1, D_MODEL, MOE_TF), lambda i, j, te, nu, sr, ds: (te[i], 0, ff_idx(i, j, nu))),
                  pl.BlockSpec((1, D_MODEL, MOE_TF), lambda i, j, te, nu, sr, ds: (te[i], 0, ff_idx(i, j, nu))),
                  pl.BlockSpec((1, MOE_TF, D_MODEL), lambda i, j, te, nu, sr, ds: (te[i], ff_idx(i, j, nu), 0))],
        out_specs=pl.BlockSpec(memory_space=pl.ANY),
        scratch_shapes=[pltpu.VMEM((2, MOE_TM, D_MODEL), F32), pltpu.VMEM((MOE_TM, D_MODEL), BF16),
                        pltpu.VMEM((MOE_TM, D_MODEL), F32), pltpu.VMEM((2, MOE_TM, D_MODEL), F32),
                        pltpu.SemaphoreType.DMA, pltpu.SemaphoreType.DMA],
    )
    return pl.pallas_call(
        _moe_body,
        grid_spec=grid_spec,
        out_shape=jax.ShapeDtypeStruct((2 * tokens + N_EXPERTS * MOE_TM, D_MODEL), F32),
        compiler_params=_cparams(("arbitrary", "arbitrary")),
        name="moe_experts",
    )(tile_e, n_used, src, dst, h, wg, wu, wd)


def _moe_plan(route, counts, tokens):
    n_tiles = (2 * tokens + N_EXPERTS * (MOE_TM - 1)) // MOE_TM
    experts = jnp.arange(N_EXPERTS, dtype=I32)
    cnt = counts[0, :N_EXPERTS].astype(I32)
    e = route[:, 0:2].astype(I32)
    rank = route[:, 2:4].astype(I32)
    tiles_per_e = (cnt + MOE_TM - 1) // MOE_TM
    tile_end = jnp.cumsum(tiles_per_e)
    tile_start = tile_end - tiles_per_e
    row_start = jnp.sum(jnp.where(e[..., None] == experts, tile_start * MOE_TM, 0), axis=-1)
    pos = row_start + rank
    slot_row = jnp.arange(tokens, dtype=I32)[:, None] + jnp.array([0, tokens], I32)[None, :]
    n_used = tile_end[-1]
    tile = jnp.minimum(jnp.arange(n_tiles, dtype=I32), n_used - 1)
    tile_e = jnp.sum((tile_end[None, :] <= tile[:, None]).astype(I32), axis=1)
    spare = 2 * tokens + tile_e[:, None] * MOE_TM + jnp.arange(MOE_TM, dtype=I32)[None, :]
    dst = spare.reshape(-1).at[pos.reshape(-1)].set(slot_row.reshape(-1))
    src = jnp.where(dst < 2 * tokens, dst % tokens, 0)
    return tile_e, n_used.reshape(1), src, dst


def _combine_body(x_ref, oa_ref, ob_ref, route_ref, g_ref, o_ref, *, final):
    w1 = route_ref[:, 4:5]
    w2 = route_ref[:, 5:6]
    y = x_ref[...] + (w1 * oa_ref[...] + w2 * ob_ref[...])
    o_ref[...] = _rms(y, g_ref[...]) if final else y


def _combine(x, o2, route, g, final):
    t = x.shape[0]
    nt = t // TM
    return pl.pallas_call(
        functools.partial(_combine_body, final=final),
        grid=(nt,),
        in_specs=[pl.BlockSpec((TM, D_MODEL), lambda i: (i, 0)), pl.BlockSpec((TM, D_MODEL), lambda i: (i, 0)),
                  pl.BlockSpec((TM, D_MODEL), lambda i: (i + nt, 0)), pl.BlockSpec((TM, ROUTE_LANES), lambda i: (i, 0)),
                  _const_spec(g.shape)],
        out_specs=pl.BlockSpec((TM, D_MODEL), lambda i: (i, 0)),
        out_shape=jax.ShapeDtypeStruct((t, D_MODEL), F32),
        compiler_params=_cparams(("parallel",)),
        name="moe_combine",
    )(x, o2, o2, route, g)


def _rot_half(w):
    half = w.shape[-1] // 2
    return jnp.concatenate([-w[..., half:], w[..., :half]], axis=-1)


def _rope_tables(seq):
    inv = 1.0 / (ROPE_THETA ** (jnp.arange(0, MLA_ROPE, 2, dtype=F32) / MLA_ROPE))
    ang = jnp.arange(seq, dtype=F32)[:, None] * inv[None, :]
    cos, sin = jnp.cos(ang), jnp.sin(ang)
    pad = HEAD_PAD - MLA_NOPE - MLA_ROPE
    c = jnp.concatenate([jnp.ones((seq, MLA_NOPE), F32), cos, cos, jnp.zeros((seq, pad), F32)], axis=1)
    s = jnp.concatenate([jnp.zeros((seq, MLA_NOPE), F32), sin, sin, jnp.zeros((seq, pad), F32)], axis=1)
    scale = (MLA_NOPE + MLA_ROPE) ** -0.5 * math.log2(math.e)
    return c * scale, s * scale, c, s


def _even_weights(w_in, w_q_up, w_kv_up):
    d = w_in.shape[0]
    pad = HEAD_PAD - MLA_NOPE - MLA_ROPE
    kr = w_in[:, POOL_WIDTH + MLA_Q_RANK + MLA_KV_RANK:]
    place = lambda m: jnp.concatenate([jnp.zeros((d, MLA_NOPE), F32), m, jnp.zeros((d, pad), F32)], axis=1)
    win = jnp.concatenate([w_in[:, :POOL_WIDTH + MLA_Q_RANK + MLA_KV_RANK], place(kr), place(_rot_half(kr))], axis=1)
    wq = w_q_up.reshape(MLA_Q_RANK, MLA_HEADS, MLA_NOPE + MLA_ROPE)
    nope, rope = wq[..., :MLA_NOPE], wq[..., MLA_NOPE:]
    zq = lambda n: jnp.zeros((MLA_Q_RANK, MLA_HEADS, n), F32)
    wq_main = jnp.concatenate([nope, rope, zq(pad)], axis=-1).reshape(MLA_Q_RANK, -1)
    wq_rot = jnp.concatenate([zq(MLA_NOPE), _rot_half(rope), zq(pad)], axis=-1).reshape(MLA_Q_RANK, -1)
    wkv = w_kv_up.reshape(MLA_KV_RANK, MLA_HEADS, MLA_NOPE + MLA_V)
    k_pad = jnp.concatenate([wkv[..., :MLA_NOPE], jnp.zeros((MLA_KV_RANK, MLA_HEADS, HEAD_PAD - MLA_NOPE), F32)],
                            axis=-1).reshape(MLA_KV_RANK, -1)
    v_c = wkv[..., MLA_NOPE:].reshape(MLA_KV_RANK, -1)
    return (win.astype(BF16), jnp.concatenate([wq_main, wq_rot], axis=1).astype(BF16),
            jnp.concatenate([k_pad, v_c], axis=1).astype(BF16))


def _band_bias(rel_bias):
    band = BAND_BLOCKS * TQ
    qpos = jnp.arange(TQ)[:, None] + (BAND_BLOCKS - 1) * TQ
    kpos = jnp.arange(band)[None, :]
    rel = qpos - kpos
    bias = rel_bias[:, jnp.clip(rel, -REL_CLIP, REL_CLIP) + REL_CLIP].astype(F32)
    dc = qpos // CHUNK - kpos // CHUNK
    valid = jnp.logical_and(dc >= 0, dc <= CA_LEFT_CHUNKS)
    return jnp.where(valid[None], bias, NEG)


def kernel(x, ev_norm_mix, ev_w_in, ev_pool_w, ev_pool_scale, ev_q_norm, ev_w_q_up, ev_kv_norm, ev_w_kv_up, ev_w_out, ev_norm_ffn, ev_ffn_w_gate, ev_ffn_w_up, ev_ffn_w_down, od_norm_mix, od_w_in, od_rel_bias, od_sgu_norm_g, od_sgu_norm_b, od_sgu_w, od_sgu_b, od_w_out, od_norm_ffn, od_router, od_moe_w_gate, od_moe_w_up, od_moe_w_down, final_norm):
    batch, seq, d = x.shape
    tokens = batch * seq
    assert d == D_MODEL and seq % TM == 0 and TM % SGU_CHUNK == 0 and TM % TQ == 0 and seq % TK == 0
    depth = ev_w_in.shape[0] + od_w_in.shape[0]
    xt = x.reshape(tokens, d)
    tabs = _rope_tables(seq)
    tri = jnp.tril(jnp.ones((TM, TM), F32), k=-1).astype(BF16)
    causal = jnp.tril(jnp.ones((SGU_CHUNK, SGU_CHUNK), bool))
    vec = lambda v: v.reshape(1, -1).astype(F32)

    for layer in range(depth):
        i = layer // 2
        if layer % 2 == 0:
            win, wq, wkv = _even_weights(ev_w_in[i], ev_w_q_up[i], ev_w_kv_up[i])
            a, q, k, v = _even_in(xt, vec(ev_norm_mix[i]), win, vec(ev_q_norm[i]), wq, vec(ev_kv_norm[i]), wkv,
                                  tabs, seq)
            att = _mla_attention(q, k, v, batch, seq)
            xt = _even_out(xt, a, att, ev_pool_w[i].astype(BF16), vec(ev_pool_scale[i]), ev_w_out[i].astype(BF16),
                           vec(ev_norm_ffn[i]), ev_ffn_w_gate[i].astype(BF16), ev_ffn_w_up[i].astype(BF16),
                           ev_ffn_w_down[i].astype(BF16), seq)
        else:
            w_in = od_w_in[i]
            win = jnp.concatenate([w_in[:, :CA_WIDTH] * (CA_HEAD_DIM ** -0.5), w_in[:, CA_WIDTH:]], axis=1).astype(BF16)
            sw = jnp.where(causal[None], od_sgu_w[i], 0.0).astype(BF16)
            sb = jnp.broadcast_to(od_sgu_b[i][:, :, None], (SGU_GROUPS, SGU_CHUNK, SGU_CH)).astype(F32)
            qkv, sg = _odd_in(xt, vec(od_norm_mix[i]), win, vec(od_sgu_norm_g[i]), vec(od_sgu_norm_b[i]), sw, sb)
            att = _band_attention(qkv, _band_bias(od_rel_bias[i]), batch, seq)
            wr = jnp.concatenate([od_router[i], jnp.zeros((d, ROUTE_LANES - N_EXPERTS), F32)], axis=1)
            x1, h, route, counts = _odd_out(xt, att, sg, od_w_out[i].astype(BF16), vec(od_norm_ffn[i]), wr, tri)
            tile_e, n_used, src, dst = _moe_plan(route, counts, tokens)
            o2 = _moe(h, tile_e, n_used, src, dst, od_moe_w_gate[i].astype(BF16), od_moe_w_up[i].astype(BF16),
                      od_moe_w_down[i].astype(BF16))
            last = layer == depth - 1
            xt = _combine(x1, o2, route, vec(final_norm), last)
    return xt.reshape(batch, seq, d)
```

```python
import functools
import math

import jax
import jax.numpy as jnp
from jax import lax
from jax.experimental import pallas as pl
from jax.experimental.pallas import tpu as pltpu

F32 = jnp.float32
BF16 = jnp.bfloat16
I32 = jnp.int32

D_MODEL = 1024
CHUNK = 64
EPS = 1e-6

POOL_GROUPS = 4
POOL_CH = 128
POOL_WIDTH = 512
POOL_WINDOWS = (2, 4, 8, 16)
POOL_HALO = 16

MLA_HEADS = 8
MLA_NOPE = 64
MLA_ROPE = 32
MLA_V = 64
MLA_Q_RANK = 256
MLA_KV_RANK = 128
ROPE_THETA = 10000.0
HEAD_PAD = 128

CA_HEADS = 8
CA_HEAD_DIM = 64
CA_LEFT_CHUNKS = 8
REL_CLIP = 128
CA_WIDTH = CA_HEADS * CA_HEAD_DIM

SGU_GROUPS = 4
SGU_CH = 128
SGU_WIDTH = 512
SGU_CHUNK = 128

D_FF = 2816
N_EXPERTS = 8
EXPERT_FF = 3584

NEG = -1e30

TM = 512
TQ = 256
TK = 512
MOE_TF = 512
MOE_FF_STEPS = EXPERT_FF // MOE_TF
MOE_TM = 80 * MOE_FF_STEPS
VMEM_LIMIT = 56 * 1024 * 1024


def _cparams(sem):
    return pltpu.CompilerParams(dimension_semantics=sem, vmem_limit_bytes=VMEM_LIMIT)


def _rms(x, g):
    return x * lax.rsqrt(jnp.mean(x * x, axis=-1, keepdims=True) + EPS) * g


def _dot(a, b):
    return jnp.dot(a, b, preferred_element_type=F32)


def _dot_nt(a, b):
    return lax.dot_general(a, b, (((1,), (1,)), ((), ())), preferred_element_type=F32)


def _const_spec(shape):
    nd = len(shape)
    return pl.BlockSpec(shape, lambda *_: (0,) * nd, pipeline_mode=pl.Buffered(1))


def _even_in_body(x_ref, g_ref, win_ref, qn_ref, wq_ref, kvn_ref, wkv_ref, cq_ref, sq_ref, ck_ref, sk_ref,
                  a_ref, q_ref, k_ref, v_ref):
    h = _rms(x_ref[...], g_ref[...]).astype(BF16)
    z = _dot(h, win_ref[...])
    a_ref[...] = z[:, :POOL_WIDTH]
    o = POOL_WIDTH
    cq = z[:, o:o + MLA_Q_RANK]
    o += MLA_Q_RANK
    ckv = z[:, o:o + MLA_KV_RANK]
    o += MLA_KV_RANK
    kr = z[:, o:o + HEAD_PAD]
    kr_rot = z[:, o + HEAD_PAD:o + 2 * HEAD_PAD]

    hw = MLA_HEADS * HEAD_PAD
    qq = _dot(_rms(cq, qn_ref[...]).astype(BF16), wq_ref[...])
    cq_t, sq_t = cq_ref[...], sq_ref[...]
    for h_i in range(MLA_HEADS):
        sl = slice(h_i * HEAD_PAD, (h_i + 1) * HEAD_PAD)
        sl_rot = slice(hw + h_i * HEAD_PAD, hw + (h_i + 1) * HEAD_PAD)
        q_ref[:, sl] = (qq[:, sl] * cq_t + qq[:, sl_rot] * sq_t).astype(BF16)

    kv = _dot(_rms(ckv, kvn_ref[...]).astype(BF16), wkv_ref[...])
    k_rope = kr * ck_ref[...] + kr_rot * sk_ref[...]
    for h_i in range(MLA_HEADS):
        sl = slice(h_i * HEAD_PAD, (h_i + 1) * HEAD_PAD)
        k_ref[:, sl] = (kv[:, sl] + k_rope).astype(BF16)
    v_ref[...] = kv[:, hw:].astype(BF16)


def _even_in(x, g, win, qn, wq, kvn, wkv, tabs, seq):
    t = x.shape[0]
    hw = MLA_HEADS * HEAD_PAD
    row = lambda w: pl.BlockSpec((TM, w), lambda i: (i, 0))
    tab = pl.BlockSpec((TM, HEAD_PAD), lambda i: (i % (seq // TM), 0))
    return pl.pallas_call(
        _even_in_body,
        grid=(t // TM,),
        in_specs=[row(D_MODEL), _const_spec(g.shape), _const_spec(win.shape), _const_spec(qn.shape),
                  _const_spec(wq.shape), _const_spec(kvn.shape), _const_spec(wkv.shape), tab, tab, tab, tab],
        out_specs=[row(POOL_WIDTH), row(hw), row(hw), row(MLA_HEADS * MLA_V)],
        out_shape=[jax.ShapeDtypeStruct((t, POOL_WIDTH), F32), jax.ShapeDtypeStruct((t, hw), BF16),
                   jax.ShapeDtypeStruct((t, hw), BF16), jax.ShapeDtypeStruct((t, MLA_HEADS * MLA_V), BF16)],
        compiler_params=_cparams(("parallel",)),
        name="even_in",
    )(x, g, win, qn, wq, kvn, wkv, *tabs)


def _mla_body(q_ref, k_ref, v_ref, o_ref, s_ref, mx_ref, l_ref, acc_ref):
    qi = pl.program_id(2)
    last = (qi * TQ) // TK
    halves = (slice(0, HEAD_PAD), slice(HEAD_PAD, 2 * HEAD_PAD))
    lane_blocks = [slice(c * 128, (c + 1) * 128) for c in range(TK // 128)]
    q_chunk = lax.broadcasted_iota(I32, (TQ, TK), 0) // CHUNK + qi * (TQ // CHUNK)
    k_chunk = lax.broadcasted_iota(I32, (TQ, TK), 1) // CHUNK + last * (TK // CHUNK)
    visible = k_chunk <= q_chunk

    mx_ref[...] = jnp.full(mx_ref.shape, NEG, F32)
    l_ref[...] = jnp.zeros_like(l_ref)
    acc_ref[...] = jnp.zeros_like(acc_ref)

    def scores(j, masked):
        rows = pl.ds(pl.multiple_of(j * TK, TK), TK)
        for hh, hs in enumerate(halves):
            s = _dot_nt(q_ref[:, hs], k_ref[rows, hs])
            if masked:
                s = jnp.where(visible, s, NEG)
            s_ref[hh, j] = s
            mx_ref[hh] = functools.reduce(jnp.maximum, [s[:, lb] for lb in lane_blocks], mx_ref[hh])

    def pairs_then_rest(count, body):
        def two(jj, c):
            body(2 * jj)
            body(2 * jj + 1)
            return c

        lax.fori_loop(0, count // 2, two, 0)

        @pl.when(count % 2 == 1)
        def _():
            body(count - 1)

    pairs_then_rest(last, lambda j: scores(j, False))
    scores(last, True)

    for hh in range(2):
        m = jnp.max(mx_ref[hh], axis=-1, keepdims=True)
        mx_ref[hh] = jnp.broadcast_to(m, (TQ, 128))

    def weighted(j):
        rows = pl.ds(pl.multiple_of(j * TK, TK), TK)
        for hh in range(2):
            m = mx_ref[hh]
            p = [jnp.exp2(s_ref[hh, j, :, lb] - m) for lb in lane_blocks]
            l_ref[hh] += functools.reduce(lambda a, b: a + b, p)
            p = jnp.concatenate(p, axis=1).astype(BF16)
            acc_ref[hh] += _dot(p, v_ref[rows, :])

    pairs_then_rest(last + 1, weighted)

    outs = [acc_ref[hh] / jnp.sum(l_ref[hh], axis=-1, keepdims=True) for hh in range(2)]
    lane = lax.broadcasted_iota(I32, (TQ, 2 * MLA_V), 1)
    o_ref[...] = jnp.where(lane < MLA_V, outs[0], outs[1]).astype(BF16)


def _mla_attention(q, k, v, batch, seq):
    t = q.shape[0]
    nq = seq // TQ
    pairs = MLA_HEADS // 2
    return pl.pallas_call(
        _mla_body,
        grid=(batch, pairs, nq),
        in_specs=[pl.BlockSpec((TQ, 2 * HEAD_PAD), lambda b, p, i: (b * nq + i, p)),
                  pl.BlockSpec((seq, 2 * HEAD_PAD), lambda b, p, i: (b, p)),
                  pl.BlockSpec((seq, 2 * MLA_V), lambda b, p, i: (b, p))],
        out_specs=pl.BlockSpec((TQ, 2 * MLA_V), lambda b, p, i: (b * nq + i, p)),
        out_shape=jax.ShapeDtypeStruct((t, MLA_HEADS * MLA_V), BF16),
        scratch_shapes=[pltpu.VMEM((2, seq // TK, TQ, TK), F32), pltpu.VMEM((2, TQ, 128), F32),
                        pltpu.VMEM((2, TQ, 128), F32), pltpu.VMEM((2, TQ, 2 * MLA_V), F32)],
        compiler_params=_cparams(("parallel", "parallel", "arbitrary")),
        name="mla_attention",
    )(q, k, v)


def _even_out_body(x_ref, a_ref, halo_ref, att_ref, pw_ref, ps_ref, wo_ref, g_ref, wg_ref, wu_ref, wd_ref,
                   o_ref, ext_ref, *, seq):
    pos0 = (pl.program_id(0) * TM) % seq
    ext_ref[0:POOL_HALO, :] = jnp.where(pos0 == 0, 0.0, halo_ref[...])
    ext_ref[POOL_HALO:, :] = a_ref[...]
    pos = lax.broadcasted_iota(I32, (TM, 1), 0) + pos0
    y = x_ref[...] + _dot(att_ref[...], wo_ref[POOL_WIDTH:, :])
    for g_i, win in enumerate(POOL_WINDOWS):
        gs = slice(g_i * POOL_CH, (g_i + 1) * POOL_CH)
        a_g = ext_ref[POOL_HALO:, gs]
        ssum = a_g
        for j in range(1, win):
            ssum = ssum + ext_ref[POOL_HALO - j:POOL_HALO - j + TM, gs]
        count = jnp.minimum(pos + 1, win).astype(F32)
        pooled = (ssum / count - a_g).astype(BF16)
        mixed = (_dot(pooled, pw_ref[g_i]) * ps_ref[:, gs]).astype(BF16)
        y = y + _dot(mixed, wo_ref[gs, :])
    h = _rms(y, g_ref[...]).astype(BF16)
    gate = _dot(h, wg_ref[...])
    up = _dot(h, wu_ref[...])
    act = (gate * jax.nn.sigmoid(gate) * up).astype(BF16)
    o_ref[...] = y + _dot(act, wd_ref[...])


def _even_out(x, a, att, pw, ps, wo, g, wg, wu, wd, seq):
    t = x.shape[0]
    row = lambda w: pl.BlockSpec((TM, w), lambda i: (i, 0))
    halo = pl.BlockSpec((POOL_HALO, POOL_WIDTH), lambda i: (jnp.maximum(i * (TM // POOL_HALO) - 1, 0), 0))
    return pl.pallas_call(
        functools.partial(_even_out_body, seq=seq),
        grid=(t // TM,),
        in_specs=[row(D_MODEL), row(POOL_WIDTH), halo, row(MLA_HEADS * MLA_V), _const_spec(pw.shape),
                  _const_spec(ps.shape), _const_spec(wo.shape), _const_spec(g.shape), _const_spec(wg.shape),
                  _const_spec(wu.shape), _const_spec(wd.shape)],
        out_specs=row(D_MODEL),
        out_shape=jax.ShapeDtypeStruct((t, D_MODEL), F32),
        scratch_shapes=[pltpu.VMEM((TM + POOL_HALO, POOL_WIDTH), F32)],
        compiler_params=_cparams(("parallel",)),
        name="even_out",
    )(x, a, a, att, pw, ps, wo, g, wg, wu, wd)


def _odd_in_body(x_ref, g_ref, win_ref, ng_ref, nb_ref, sw_ref, sb_ref, qkv_ref, sg_ref):
    h = _rms(x_ref[...], g_ref[...]).astype(BF16)
    z = _dot(h, win_ref[...])
    qkv_ref[...] = z[:, :3 * CA_WIDTH].astype(BF16)
    uv = jax.nn.gelu(z[:, 3 * CA_WIDTH:])
    for g_i in range(SGU_GROUPS):
        gs = slice(g_i * SGU_CH, (g_i + 1) * SGU_CH)
        u = uv[:, gs]
        gv = uv[:, SGU_WIDTH + g_i * SGU_CH:SGU_WIDTH + (g_i + 1) * SGU_CH]
        mu = jnp.mean(gv, axis=-1, keepdims=True)
        xc = gv - mu
        vn = xc * lax.rsqrt(jnp.mean(xc * xc, axis=-1, keepdims=True) + EPS) * ng_ref[:, gs] + nb_ref[:, gs]
        vn = vn.astype(BF16)
        w = sw_ref[g_i]
        for c_i in range(TM // SGU_CHUNK):
            rs = slice(c_i * SGU_CHUNK, (c_i + 1) * SGU_CHUNK)
            mixed = _dot(w, vn[rs, :]) + sb_ref[g_i]
            sg_ref[rs, gs] = (u[rs, :] * mixed).astype(BF16)


def _odd_in(x, g, win, ng, nb, sw, sb):
    t = x.shape[0]
    row = lambda w: pl.BlockSpec((TM, w), lambda i: (i, 0))
    return pl.pallas_call(
        _odd_in_body,
        grid=(t // TM,),
        in_specs=[row(D_MODEL), _const_spec(g.shape), _const_spec(win.shape), _const_spec(ng.shape),
                  _const_spec(nb.shape), _const_spec(sw.shape), _const_spec(sb.shape)],
        out_specs=[row(3 * CA_WIDTH), row(SGU_WIDTH)],
        out_shape=[jax.ShapeDtypeStruct((t, 3 * CA_WIDTH), BF16), jax.ShapeDtypeStruct((t, SGU_WIDTH), BF16)],
        compiler_params=_cparams(("parallel",)),
        name="odd_in",
    )(x, g, win, ng, nb, sw, sb)


BAND_BLOCKS = 3


def _band_body(q_ref, k0_ref, k1_ref, k2_ref, v0_ref, v1_ref, v2_ref, bias_ref, o_ref):
    qi = pl.program_id(2)
    lane = lax.broadcasted_iota(I32, (TQ, 2 * CA_HEAD_DIM), 1)
    low = lane < CA_HEAD_DIM
    q_pair = q_ref[...]
    k_refs = (k0_ref, k1_ref, k2_ref)
    v_refs = (v0_ref, v1_ref, v2_ref)
    outs = []
    for hh in range(2):
        q = jnp.where(low if hh == 0 else jnp.logical_not(low), q_pair, jnp.zeros_like(q_pair))
        s = []
        for jj in range(BAND_BLOCKS):
            before_start = qi + jj < BAND_BLOCKS - 1
            pen = jnp.where(before_start, NEG, 0.0)
            s.append(_dot_nt(q, k_refs[jj][...]) + bias_ref[hh, :, jj * TQ:(jj + 1) * TQ] + pen)
        m = functools.reduce(jnp.maximum, [jnp.max(x, axis=-1, keepdims=True) for x in s])
        p = [jnp.exp(x - m) for x in s]
        l = functools.reduce(lambda a, b: a + b, [jnp.sum(x, axis=-1, keepdims=True) for x in p])
        acc = functools.reduce(lambda a, b: a + b,
                               [_dot(p[jj].astype(BF16), v_refs[jj][...]) for jj in range(BAND_BLOCKS)])
        outs.append(acc / l)
    o_ref[...] = jnp.where(low, outs[0], outs[1]).astype(BF16)


def _band_attention(qkv, bias, batch, seq):
    t = qkv.shape[0]
    nq = seq // TQ
    pairs = CA_HEADS // 2
    w = 2 * CA_HEAD_DIM

    def kv_spec(jj, col0):
        return pl.BlockSpec((TQ, w), lambda b, p, i: (b * nq + jnp.maximum(i - (BAND_BLOCKS - 1) + jj, 0), col0 + p))

    return pl.pallas_call(
        _band_body,
        grid=(batch, pairs, nq),
        in_specs=[pl.BlockSpec((TQ, w), lambda b, p, i: (b * nq + i, p))]
                 + [kv_spec(jj, pairs) for jj in range(BAND_BLOCKS)]
                 + [kv_spec(jj, 2 * pairs) for jj in range(BAND_BLOCKS)]
                 + [pl.BlockSpec((2, TQ, BAND_BLOCKS * TQ), lambda b, p, i: (p, 0, 0))],
        out_specs=pl.BlockSpec((TQ, w), lambda b, p, i: (b * nq + i, p)),
        out_shape=jax.ShapeDtypeStruct((t, CA_WIDTH), BF16),
        compiler_params=_cparams(("parallel", "parallel", "arbitrary")),
        name="band_attention",
    )(qkv, qkv, qkv, qkv, qkv, qkv, qkv, bias)


ROUTE_LANES = 128


def _odd_out_body(x_ref, att_ref, sg_ref, wo_ref, g_ref, wr_ref, tri_ref, x1_ref, h_ref, route_ref, cnt_ref, run_ref):
    @pl.when(pl.program_id(0) == 0)
    def _():
        run_ref[...] = jnp.zeros_like(run_ref)

    y = x_ref[...] + _dot(att_ref[...], wo_ref[:CA_WIDTH, :]) + _dot(sg_ref[...], wo_ref[CA_WIDTH:, :])
    x1_ref[...] = y
    h = _rms(y, g_ref[...])
    h_ref[...] = h
    logits = jnp.dot(h, wr_ref[...], preferred_element_type=F32, precision=lax.Precision.HIGHEST)
    lane = lax.broadcasted_iota(I32, (TM, ROUTE_LANES), 1)
    logits = jnp.where(lane < N_EXPERTS, logits, NEG)
    m1 = jnp.max(logits, axis=-1, keepdims=True)
    e1 = jnp.min(jnp.where(logits == m1, lane, ROUTE_LANES), axis=-1, keepdims=True)
    rest = jnp.where(lane == e1, NEG, logits)
    m2 = jnp.max(rest, axis=-1, keepdims=True)
    e2 = jnp.min(jnp.where(rest == m2, lane, ROUTE_LANES), axis=-1, keepdims=True)
    ex = jnp.exp(m2 - m1)
    w1 = 1.0 / (1.0 + ex)
    w2 = ex / (1.0 + ex)
    hot1 = (lane == e1).astype(F32)
    hot2 = (lane == e2).astype(F32)
    sel = hot1 + hot2
    before = _dot(tri_ref[...], sel.astype(BF16)) + run_ref[0:1, :]
    rank1 = jnp.sum(hot1 * before, axis=-1, keepdims=True)
    rank2 = jnp.sum(hot2 * before, axis=-1, keepdims=True)
    rec = jnp.where(lane == 0, e1.astype(F32), 0.0)
    rec = jnp.where(lane == 1, e2.astype(F32), rec)
    rec = jnp.where(lane == 2, rank1, rec)
    rec = jnp.where(lane == 3, rank2, rec)
    rec = jnp.where(lane == 4, w1, rec)
    rec = jnp.where(lane == 5, w2, rec)
    route_ref[...] = rec
    total = run_ref[0:1, :] + jnp.sum(sel, axis=0, keepdims=True)
    run_ref[...] = jnp.broadcast_to(total, run_ref.shape)
    cnt_ref[...] = jnp.broadcast_to(total, cnt_ref.shape)


def _odd_out(x, att, sg, wo, g, wr, tri):
    t = x.shape[0]
    row = lambda w: pl.BlockSpec((TM, w), lambda i: (i, 0))
    return pl.pallas_call(
        _odd_out_body,
        grid=(t // TM,),
        in_specs=[row(D_MODEL), row(CA_WIDTH), row(SGU_WIDTH), _const_spec(wo.shape), _const_spec(g.shape),
                  _const_spec(wr.shape), _const_spec(tri.shape)],
        out_specs=[row(D_MODEL), row(D_MODEL), row(ROUTE_LANES), pl.BlockSpec((8, ROUTE_LANES), lambda i: (0, 0))],
        out_shape=[jax.ShapeDtypeStruct((t, D_MODEL), F32), jax.ShapeDtypeStruct((t, D_MODEL), F32),
                   jax.ShapeDtypeStruct((t, ROUTE_LANES), F32), jax.ShapeDtypeStruct((8, ROUTE_LANES), F32)],
        scratch_shapes=[pltpu.VMEM((8, ROUTE_LANES), F32)],
        compiler_params=_cparams(("arbitrary",)),
        name="odd_out_router",
    )(x, att, sg, wo, g, wr, tri)


def _moe_body(tile_e_ref, n_used_ref, src_ref, dst_ref, h_hbm, wg_ref, wu_ref, wd_ref, o_hbm,
              xf_ref, xb_ref, acc_ref, y_ref, gsem, ssem):
    i = pl.program_id(0)
    j = pl.program_id(1)
    n_used = n_used_ref[0]
    rows_per_step = MOE_TM // MOE_FF_STEPS

    def gather_copy(tile, r, slot):
        return pltpu.make_async_copy(h_hbm.at[pl.ds(src_ref[tile * MOE_TM + r], 1), :],
                                     xf_ref.at[slot, pl.ds(r, 1), :], gsem)

    def scatter_copy(tile, r, slot):
        return pltpu.make_async_copy(y_ref.at[slot, pl.ds(r, 1), :],
                                     o_hbm.at[pl.ds(dst_ref[tile * MOE_TM + r], 1), :], ssem)

    def wait_tile_gather(slot):
        pltpu.make_async_copy(h_hbm.at[pl.ds(0, MOE_TM), :], xf_ref.at[slot], gsem).wait()

    def wait_tile_scatter(slot):
        pltpu.make_async_copy(y_ref.at[slot], o_hbm.at[pl.ds(0, MOE_TM), :], ssem).wait()

    def start_all(copy_fn):
        lax.fori_loop(0, MOE_TM, lambda r, c: (copy_fn(r).start(), c)[1], 0)

    @pl.when(i < n_used)
    def _():
        cur = lax.rem(i, 2)
        other = 1 - cur
        nxt_tile = jnp.minimum(i + 1, n_used - 1)
        prev_tile = jnp.maximum(i - 1, 0)

        @pl.when(jnp.logical_and(i == 0, j == 0))
        def _():
            y_ref[...] = jnp.zeros_like(y_ref)
            spare0 = o_hbm.shape[0] - N_EXPERTS * MOE_TM
            for e_i in range(N_EXPERTS):
                fill = pltpu.make_async_copy(y_ref.at[0], o_hbm.at[pl.ds(spare0 + e_i * MOE_TM, MOE_TM), :], ssem)
                fill.start()
                fill.wait()
            start_all(lambda r: gather_copy(0, r, 0))

        @pl.when(j == 0)
        def _():
            wait_tile_gather(cur)
            xb_ref[...] = xf_ref[cur].astype(BF16)
            acc_ref[...] = jnp.zeros_like(acc_ref)

        base = pl.multiple_of(j * rows_per_step, 8)
        for rr in range(rows_per_step):
            gather_copy(nxt_tile, base + rr, other).start()
            scatter_copy(prev_tile, base + rr, other).start()
        xb = xb_ref[...]
        gate = _dot(xb, wg_ref[...])
        up = _dot(xb, wu_ref[...])
        act = (gate * jax.nn.sigmoid(gate) * up).astype(BF16)
        acc_ref[...] += _dot(act, wd_ref[...])

        @pl.when(j == MOE_FF_STEPS - 1)
        def _():
            wait_tile_scatter(other)
            y_ref[cur] = acc_ref[...]

        @pl.when(jnp.logical_and(i == n_used - 1, j == MOE_FF_STEPS - 1))
        def _():
            start_all(lambda r: scatter_copy(i, r, cur))
            wait_tile_scatter(cur)
            wait_tile_gather(other)


def _moe(h, tile_e, n_used, src, dst, wg, wu, wd, layer):
    tokens = h.shape[0]
    n_tiles = tile_e.shape[0]
    last = MOE_FF_STEPS - 1

    def ff_idx(i, j, n_used_ref):
        return jnp.where(i < n_used_ref[0], j, last)

    grid_spec = pltpu.PrefetchScalarGridSpec(
        num_scalar_prefetch=4,
        grid=(n_tiles, MOE_FF_STEPS),
        in_specs=[pl.BlockSpec(memory_space=pl.ANY),
                  pl.BlockSpec((None, None, D_MODEL, MOE_TF),
                               lambda i, j, te, nu, sr, ds: (layer, te[i], 0, ff_idx(i, j, nu))),
                  pl.BlockSpec((None, None, D_MODEL, MOE_TF),
                               lambda i, j, te, nu, sr, ds: (layer, te[i], 0, ff_idx(i, j, nu))),
                  pl.BlockSpec((None, None, MOE_TF, D_MODEL),
                               lambda i, j, te, nu, sr, ds: (layer, te[i], ff_idx(i, j, nu), 0))],
        out_specs=pl.BlockSpec(memory_space=pl.ANY),
        scratch_shapes=[pltpu.VMEM((2, MOE_TM, D_MODEL), F32), pltpu.VMEM((MOE_TM, D_MODEL), BF16),
                        pltpu.VMEM((MOE_TM, D_MODEL), F32), pltpu.VMEM((2, MOE_TM, D_MODEL), F32),
                        pltpu.SemaphoreType.DMA, pltpu.SemaphoreType.DMA],
    )
    return pl.pallas_call(
        _moe_body,
        grid_spec=grid_spec,
        out_shape=jax.ShapeDtypeStruct((2 * tokens + N_EXPERTS * MOE_TM, D_MODEL), F32),
        compiler_params=_cparams(("arbitrary", "arbitrary")),
        name="moe_experts",
    )(tile_e, n_used, src, dst, h, wg, wu, wd)


def _moe_plan(route, counts, tokens):
    n_tiles = (2 * tokens + N_EXPERTS * (MOE_TM - 1)) // MOE_TM
    experts = jnp.arange(N_EXPERTS, dtype=I32)
    cnt = counts[0, :N_EXPERTS].astype(I32)
    e = route[:, 0:2].astype(I32)
    rank = route[:, 2:4].astype(I32)
    tiles_per_e = (cnt + MOE_TM - 1) // MOE_TM
    tile_end = jnp.cumsum(tiles_per_e)
    tile_start = tile_end - tiles_per_e
    row_start = jnp.sum(jnp.where(e[..., None] == experts, tile_start * MOE_TM, 0), axis=-1)
    pos = row_start + rank
    slot_row = jnp.arange(tokens, dtype=I32)[:, None] + jnp.array([0, tokens], I32)[None, :]
    n_used = tile_end[-1]
    tile = jnp.minimum(jnp.arange(n_tiles, dtype=I32), n_used - 1)
    tile_e = jnp.sum((tile_end[None, :] <= tile[:, None]).astype(I32), axis=1)
    spare = 2 * tokens + tile_e[:, None] * MOE_TM + jnp.arange(MOE_TM, dtype=I32)[None, :]
    dst = spare.reshape(-1).at[pos.reshape(-1)].set(slot_row.reshape(-1))
    src = jnp.where(dst < 2 * tokens, dst % tokens, 0)
    return tile_e, n_used.reshape(1), src, dst


def _combine_body(x_ref, oa_ref, ob_ref, route_ref, g_ref, o_ref, *, final):
    w1 = route_ref[:, 4:5]
    w2 = route_ref[:, 5:6]
    y = x_ref[...] + (w1 * oa_ref[...] + w2 * ob_ref[...])
    o_ref[...] = _rms(y, g_ref[...]) if final else y


def _combine(x, o2, route, g, final):
    t = x.shape[0]
    nt = t // TM
    return pl.pallas_call(
        functools.partial(_combine_body, final=final),
        grid=(nt,),
        in_specs=[pl.BlockSpec((TM, D_MODEL), lambda i: (i, 0)), pl.BlockSpec((TM, D_MODEL), lambda i: (i, 0)),
                  pl.BlockSpec((TM, D_MODEL), lambda i: (i + nt, 0)), pl.BlockSpec((TM, ROUTE_LANES), lambda i: (i, 0)),
                  _const_spec(g.shape)],
        out_specs=pl.BlockSpec((TM, D_MODEL), lambda i: (i, 0)),
        out_shape=jax.ShapeDtypeStruct((t, D_MODEL), F32),
        compiler_params=_cparams(("parallel",)),
        name="moe_combine",
    )(x, o2, o2, route, g)


def _rot_half(w):
    half = w.shape[-1] // 2
    return jnp.concatenate([-w[..., half:], w[..., :half]], axis=-1)


def _rope_tables(seq):
    inv = 1.0 / (ROPE_THETA ** (jnp.arange(0, MLA_ROPE, 2, dtype=F32) / MLA_ROPE))
    ang = jnp.arange(seq, dtype=F32)[:, None] * inv[None, :]
    cos, sin = jnp.cos(ang), jnp.sin(ang)
    pad = HEAD_PAD - MLA_NOPE - MLA_ROPE
    c = jnp.concatenate([jnp.ones((seq, MLA_NOPE), F32), cos, cos, jnp.zeros((seq, pad), F32)], axis=1)
    s = jnp.concatenate([jnp.zeros((seq, MLA_NOPE), F32), sin, sin, jnp.zeros((seq, pad), F32)], axis=1)
    scale = (MLA_NOPE + MLA_ROPE) ** -0.5 * math.log2(math.e)
    return c * scale, s * scale, c, s


def _even_weights(w_in, w_q_up, w_kv_up):
    d = w_in.shape[0]
    pad = HEAD_PAD - MLA_NOPE - MLA_ROPE
    kr = w_in[:, POOL_WIDTH + MLA_Q_RANK + MLA_KV_RANK:]
    place = lambda m: jnp.concatenate([jnp.zeros((d, MLA_NOPE), F32), m, jnp.zeros((d, pad), F32)], axis=1)
    win = jnp.concatenate([w_in[:, :POOL_WIDTH + MLA_Q_RANK + MLA_KV_RANK], place(kr), place(_rot_half(kr))], axis=1)
    wq = w_q_up.reshape(MLA_Q_RANK, MLA_HEADS, MLA_NOPE + MLA_ROPE)
    nope, rope = wq[..., :MLA_NOPE], wq[..., MLA_NOPE:]
    zq = lambda n: jnp.zeros((MLA_Q_RANK, MLA_HEADS, n), F32)
    wq_main = jnp.concatenate([nope, rope, zq(pad)], axis=-1).reshape(MLA_Q_RANK, -1)
    wq_rot = jnp.concatenate([zq(MLA_NOPE), _rot_half(rope), zq(pad)], axis=-1).reshape(MLA_Q_RANK, -1)
    wkv = w_kv_up.reshape(MLA_KV_RANK, MLA_HEADS, MLA_NOPE + MLA_V)
    k_pad = jnp.concatenate([wkv[..., :MLA_NOPE], jnp.zeros((MLA_KV_RANK, MLA_HEADS, HEAD_PAD - MLA_NOPE), F32)],
                            axis=-1).reshape(MLA_KV_RANK, -1)
    v_c = wkv[..., MLA_NOPE:].reshape(MLA_KV_RANK, -1)
    return (win.astype(BF16), jnp.concatenate([wq_main, wq_rot], axis=1).astype(BF16),
            jnp.concatenate([k_pad, v_c], axis=1).astype(BF16))


def _band_bias(rel_bias):
    band = BAND_BLOCKS * TQ
    qpos = jnp.arange(TQ)[:, None] + (BAND_BLOCKS - 1) * TQ
    kpos = jnp.arange(band)[None, :]
    rel = (BAND_BLOCKS - 1) * TQ + (TQ - 1) - jnp.arange(band + TQ - 1)
    by_rel = rel_bias[:, jnp.clip(rel, -REL_CLIP, REL_CLIP) + REL_CLIP].astype(F32)
    bias = jnp.stack([by_rel[:, TQ - 1 - q:TQ - 1 - q + band] for q in range(TQ)], axis=1)
    dc = qpos // CHUNK - kpos // CHUNK
    valid = jnp.logical_and(dc >= 0, dc <= CA_LEFT_CHUNKS)
    return jnp.where(valid[None], bias, NEG)


def kernel(x, ev_norm_mix, ev_w_in, ev_pool_w, ev_pool_scale, ev_q_norm, ev_w_q_up, ev_kv_norm, ev_w_kv_up, ev_w_out, ev_norm_ffn, ev_ffn_w_gate, ev_ffn_w_up, ev_ffn_w_down, od_norm_mix, od_w_in, od_rel_bias, od_sgu_norm_g, od_sgu_norm_b, od_sgu_w, od_sgu_b, od_w_out, od_norm_ffn, od_router, od_moe_w_gate, od_moe_w_up, od_moe_w_down, final_norm):
    batch, seq, d = x.shape
    tokens = batch * seq
    assert d == D_MODEL and seq % TM == 0 and TM % SGU_CHUNK == 0 and TM % TQ == 0 and seq % TK == 0
    depth = ev_w_in.shape[0] + od_w_in.shape[0]
    xt = x.reshape(tokens, d)
    tabs = _rope_tables(seq)
    tri = jnp.tril(jnp.ones((TM, TM), F32), k=-1).astype(BF16)
    causal = jnp.tril(jnp.ones((SGU_CHUNK, SGU_CHUNK), bool))
    vec = lambda v: v.reshape(1, -1).astype(F32)
    moe_wg, moe_wu, moe_wd = (w.astype(BF16) for w in (od_moe_w_gate, od_moe_w_up, od_moe_w_down))

    for layer in range(depth):
        i = layer // 2
        if layer % 2 == 0:
            win, wq, wkv = _even_weights(ev_w_in[i], ev_w_q_up[i], ev_w_kv_up[i])
            a, q, k, v = _even_in(xt, vec(ev_norm_mix[i]), win, vec(ev_q_norm[i]), wq, vec(ev_kv_norm[i]), wkv,
                                  tabs, seq)
            att = _mla_attention(q, k, v, batch, seq)
            xt = _even_out(xt, a, att, ev_pool_w[i].astype(BF16), vec(ev_pool_scale[i]), ev_w_out[i].astype(BF16),
                           vec(ev_norm_ffn[i]), ev_ffn_w_gate[i].astype(BF16), ev_ffn_w_up[i].astype(BF16),
                           ev_ffn_w_down[i].astype(BF16), seq)
        else:
            w_in = od_w_in[i]
            win = jnp.concatenate([w_in[:, :CA_WIDTH] * (CA_HEAD_DIM ** -0.5), w_in[:, CA_WIDTH:]], axis=1).astype(BF16)
            sw = jnp.where(causal[None], od_sgu_w[i], 0.0).astype(BF16)
            sb = jnp.broadcast_to(od_sgu_b[i][:, :, None], (SGU_GROUPS, SGU_CHUNK, SGU_CH)).astype(F32)
            qkv, sg = _odd_in(xt, vec(od_norm_mix[i]), win, vec(od_sgu_norm_g[i]), vec(od_sgu_norm_b[i]), sw, sb)
            att = _band_attention(qkv, _band_bias(od_rel_bias[i]), batch, seq)
            wr = jnp.concatenate([od_router[i], jnp.zeros((d, ROUTE_LANES - N_EXPERTS), F32)], axis=1)
            x1, h, route, counts = _odd_out(xt, att, sg, od_w_out[i].astype(BF16), vec(od_norm_ffn[i]), wr, tri)
            tile_e, n_used, src, dst = _moe_plan(route, counts, tokens)
            o2 = _moe(h, tile_e, n_used, src, dst, moe_wg, moe_wu, moe_wd, i)
            last = layer == depth - 1
            xt = _combine(x1, o2, route, vec(final_norm), last)
    return xt.reshape(batch, seq, d)
```

```python
import functools
import math

import jax
import jax.numpy as jnp
from jax import lax
from jax.experimental import pallas as pl
from jax.experimental.pallas import tpu as pltpu

F32 = jnp.float32
BF16 = jnp.bfloat16
I32 = jnp.int32

D_MODEL = 1024
CHUNK = 64
EPS = 1e-6

POOL_GROUPS = 4
POOL_CH = 128
POOL_WIDTH = 512
POOL_WINDOWS = (2, 4, 8, 16)
POOL_HALO = 16

MLA_HEADS = 8
MLA_NOPE = 64
MLA_ROPE = 32
MLA_V = 64
MLA_Q_RANK = 256
MLA_KV_RANK = 128
ROPE_THETA = 10000.0
HEAD_PAD = 128

CA_HEADS = 8
CA_HEAD_DIM = 64
CA_LEFT_CHUNKS = 8
REL_CLIP = 128
CA_WIDTH = CA_HEADS * CA_HEAD_DIM

SGU_GROUPS = 4
SGU_CH = 128
SGU_WIDTH = 512
SGU_CHUNK = 128

D_FF = 2816
N_EXPERTS = 8
EXPERT_FF = 3584

NEG = -1e30

TM = 512
TQ = 256
TK = 512
MOE_TF = 512
MOE_FF_STEPS = EXPERT_FF // MOE_TF
MOE_TM = 80 * MOE_FF_STEPS
VMEM_LIMIT = 56 * 1024 * 1024


def _cparams(sem):
    return pltpu.CompilerParams(dimension_semantics=sem, vmem_limit_bytes=VMEM_LIMIT)


def _rms(x, g):
    return x * lax.rsqrt(jnp.mean(x * x, axis=-1, keepdims=True) + EPS) * g


def _dot(a, b):
    return jnp.dot(a, b, preferred_element_type=F32)


def _dot_nt(a, b):
    return lax.dot_general(a, b, (((1,), (1,)), ((), ())), preferred_element_type=F32)


TOKEN_TILE_ROWS = D_MODEL // 128


def _store_token_tiles(ref, x):
    n = x.shape[0]
    for s in range(TOKEN_TILE_ROWS):
        ref[pl.ds(s, n, stride=TOKEN_TILE_ROWS), :] = x[:, s * 128:(s + 1) * 128]


def _load_token_tiles(ref, n):
    return jnp.concatenate([ref[pl.ds(s, n, stride=TOKEN_TILE_ROWS), :] for s in range(TOKEN_TILE_ROWS)], axis=1)


def _const_spec(shape):
    nd = len(shape)
    return pl.BlockSpec(shape, lambda *_: (0,) * nd, pipeline_mode=pl.Buffered(1))


def _even_in_body(x_ref, g_ref, win_ref, qn_ref, wq_ref, kvn_ref, wkv_ref, vones_ref, cq_ref, sq_ref, ck_ref, sk_ref,
                  a_ref, q_ref, k_ref, v_ref):
    h = _rms(x_ref[...], g_ref[...]).astype(BF16)
    z = _dot(h, win_ref[...])
    a_ref[...] = z[:, :POOL_WIDTH]
    o = POOL_WIDTH
    cq = z[:, o:o + MLA_Q_RANK]
    o += MLA_Q_RANK
    ckv = z[:, o:o + MLA_KV_RANK]
    o += MLA_KV_RANK
    kr = z[:, o:o + HEAD_PAD]
    kr_rot = z[:, o + HEAD_PAD:o + 2 * HEAD_PAD]

    hw = MLA_HEADS * HEAD_PAD
    qq = _dot(_rms(cq, qn_ref[...]).astype(BF16), wq_ref[...])
    cq_t, sq_t = cq_ref[...], sq_ref[...]
    for h_i in range(MLA_HEADS):
        sl = slice(h_i * HEAD_PAD, (h_i + 1) * HEAD_PAD)
        sl_rot = slice(hw + h_i * HEAD_PAD, hw + (h_i + 1) * HEAD_PAD)
        q_ref[:, sl] = (qq[:, sl] * cq_t + qq[:, sl_rot] * sq_t).astype(BF16)

    kv = _dot(_rms(ckv, kvn_ref[...]).astype(BF16), wkv_ref[...])
    k_rope = kr * ck_ref[...] + kr_rot * sk_ref[...]
    for h_i in range(MLA_HEADS):
        sl = slice(h_i * HEAD_PAD, (h_i + 1) * HEAD_PAD)
        k_ref[:, sl] = (kv[:, sl] + k_rope).astype(BF16)
    v_ref[...] = (kv[:, hw:] + vones_ref[...]).astype(BF16)


def _even_in(x, g, win, qn, wq, kvn, wkv, vones, tabs, seq):
    t = x.shape[0]
    hw = MLA_HEADS * HEAD_PAD
    row = lambda w: pl.BlockSpec((TM, w), lambda i: (i, 0))
    tab = pl.BlockSpec((TM, HEAD_PAD), lambda i: (i % (seq // TM), 0))
    return pl.pallas_call(
        _even_in_body,
        grid=(t // TM,),
        in_specs=[row(D_MODEL), _const_spec(g.shape), _const_spec(win.shape), _const_spec(qn.shape),
                  _const_spec(wq.shape), _const_spec(kvn.shape), _const_spec(wkv.shape), _const_spec(vones.shape),
                  tab, tab, tab, tab],
        out_specs=[row(POOL_WIDTH), row(hw), row(hw), row(hw)],
        out_shape=[jax.ShapeDtypeStruct((t, POOL_WIDTH), F32), jax.ShapeDtypeStruct((t, hw), BF16),
                   jax.ShapeDtypeStruct((t, hw), BF16), jax.ShapeDtypeStruct((t, hw), BF16)],
        compiler_params=_cparams(("parallel",)),
        name="even_in",
    )(x, g, win, qn, wq, kvn, wkv, vones, *tabs)


def _mla_body(q_ref, k_ref, v_ref, o_ref, s_ref, mx_ref, acc_ref):
    qi = pl.program_id(2)
    last = (qi * TQ) // TK
    halves = (slice(0, HEAD_PAD), slice(HEAD_PAD, 2 * HEAD_PAD))
    lane_blocks = [slice(c * 128, (c + 1) * 128) for c in range(TK // 128)]
    q_chunk = lax.broadcasted_iota(I32, (TQ, TK), 0) // CHUNK + qi * (TQ // CHUNK)
    k_chunk = lax.broadcasted_iota(I32, (TQ, TK), 1) // CHUNK + last * (TK // CHUNK)
    visible = k_chunk <= q_chunk

    mx_ref[...] = jnp.full(mx_ref.shape, NEG, F32)
    acc_ref[...] = jnp.zeros_like(acc_ref)

    def scores(j, masked):
        rows = pl.ds(pl.multiple_of(j * TK, TK), TK)
        for hh, hs in enumerate(halves):
            s = _dot_nt(q_ref[:, hs], k_ref[rows, hs])
            if masked:
                s = jnp.where(visible, s, NEG)
            s_ref[hh, j] = s
            mx_ref[hh] = functools.reduce(jnp.maximum, [s[:, lb] for lb in lane_blocks], mx_ref[hh])

    def pairs_then_rest(count, body):
        def two(jj, c):
            body(2 * jj)
            body(2 * jj + 1)
            return c

        lax.fori_loop(0, count // 2, two, 0)

        @pl.when(count % 2 == 1)
        def _():
            body(count - 1)

    pairs_then_rest(last, lambda j: scores(j, False))
    scores(last, True)

    for hh in range(2):
        m = jnp.max(mx_ref[hh], axis=-1, keepdims=True)
        mx_ref[hh] = jnp.broadcast_to(m, (TQ, 128))

    def weighted(j):
        rows = pl.ds(pl.multiple_of(j * TK, TK), TK)
        for hh, hs in enumerate(halves):
            m = mx_ref[hh]
            p = jnp.concatenate([jnp.exp2(s_ref[hh, j, :, lb] - m) for lb in lane_blocks], axis=1)
            acc_ref[hh] += _dot(p.astype(BF16), v_ref[rows, hs])

    pairs_then_rest(last + 1, weighted)

    even, odd = acc_ref[0], acc_ref[1]
    lane = lax.broadcasted_iota(I32, (TQ, 2 * MLA_V), 1)
    out = jnp.where(lane < MLA_V, even / even[:, MLA_V:MLA_V + 1], odd / odd[:, 0:1])
    o_ref[...] = out.astype(BF16)


def _mla_attention(q, k, v, batch, seq):
    t = q.shape[0]
    nq = seq // TQ
    pairs = MLA_HEADS // 2
    return pl.pallas_call(
        _mla_body,
        grid=(batch, pairs, nq),
        in_specs=[pl.BlockSpec((TQ, 2 * HEAD_PAD), lambda b, p, i: (b * nq + i, p)),
                  pl.BlockSpec((seq, 2 * HEAD_PAD), lambda b, p, i: (b, p)),
                  pl.BlockSpec((seq, 2 * HEAD_PAD), lambda b, p, i: (b, p))],
        out_specs=pl.BlockSpec((TQ, 2 * MLA_V), lambda b, p, i: (b * nq + i, p)),
        out_shape=jax.ShapeDtypeStruct((t, MLA_HEADS * MLA_V), BF16),
        scratch_shapes=[pltpu.VMEM((2, seq // TK, TQ, TK), F32), pltpu.VMEM((2, TQ, 128), F32),
                        pltpu.VMEM((2, TQ, HEAD_PAD), F32)],
        compiler_params=_cparams(("parallel", "parallel", "arbitrary")),
        name="mla_attention",
    )(q, k, v)


def _even_out_body(x_ref, a_ref, halo_ref, att_ref, pw_ref, ps_ref, wo_ref, g_ref, wg_ref, wu_ref, wd_ref,
                   o_ref, ext_ref, *, seq):
    pos0 = (pl.program_id(0) * TM) % seq
    ext_ref[0:POOL_HALO, :] = jnp.where(pos0 == 0, 0.0, halo_ref[...])
    ext_ref[POOL_HALO:, :] = a_ref[...]
    pos = lax.broadcasted_iota(I32, (TM, 1), 0) + pos0
    y = x_ref[...] + _dot(att_ref[...], wo_ref[POOL_WIDTH:, :])
    for g_i, win in enumerate(POOL_WINDOWS):
        gs = slice(g_i * POOL_CH, (g_i + 1) * POOL_CH)
        a_g = ext_ref[POOL_HALO:, gs]
        ssum = a_g
        for j in range(1, win):
            ssum = ssum + ext_ref[POOL_HALO - j:POOL_HALO - j + TM, gs]
        count = jnp.minimum(pos + 1, win).astype(F32)
        pooled = (ssum / count - a_g).astype(BF16)
        mixed = (_dot(pooled, pw_ref[g_i]) * ps_ref[:, gs]).astype(BF16)
        y = y + _dot(mixed, wo_ref[gs, :])
    h = _rms(y, g_ref[...]).astype(BF16)
    gate = _dot(h, wg_ref[...])
    up = _dot(h, wu_ref[...])
    act = (gate * jax.nn.sigmoid(gate) * up).astype(BF16)
    o_ref[...] = y + _dot(act, wd_ref[...])


def _even_out(x, a, att, pw, ps, wo, g, wg, wu, wd, seq):
    t = x.shape[0]
    row = lambda w: pl.BlockSpec((TM, w), lambda i: (i, 0))
    halo = pl.BlockSpec((POOL_HALO, POOL_WIDTH), lambda i: (jnp.maximum(i * (TM // POOL_HALO) - 1, 0), 0))
    return pl.pallas_call(
        functools.partial(_even_out_body, seq=seq),
        grid=(t // TM,),
        in_specs=[row(D_MODEL), row(POOL_WIDTH), halo, row(MLA_HEADS * MLA_V), _const_spec(pw.shape),
                  _const_spec(ps.shape), _const_spec(wo.shape), _const_spec(g.shape), _const_spec(wg.shape),
                  _const_spec(wu.shape), _const_spec(wd.shape)],
        out_specs=row(D_MODEL),
        out_shape=jax.ShapeDtypeStruct((t, D_MODEL), F32),
        scratch_shapes=[pltpu.VMEM((TM + POOL_HALO, POOL_WIDTH), F32)],
        compiler_params=_cparams(("parallel",)),
        name="even_out",
    )(x, a, a, att, pw, ps, wo, g, wg, wu, wd)


def _odd_in_body(x_ref, g_ref, win_ref, ng_ref, nb_ref, sw_ref, sb_ref, qkv_ref, sg_ref):
    h = _rms(x_ref[...], g_ref[...]).astype(BF16)
    z = _dot(h, win_ref[...])
    qkv_ref[...] = z[:, :3 * CA_WIDTH].astype(BF16)
    uv = jax.nn.gelu(z[:, 3 * CA_WIDTH:])
    for g_i in range(SGU_GROUPS):
        gs = slice(g_i * SGU_CH, (g_i + 1) * SGU_CH)
        u = uv[:, gs]
        gv = uv[:, SGU_WIDTH + g_i * SGU_CH:SGU_WIDTH + (g_i + 1) * SGU_CH]
        mu = jnp.mean(gv, axis=-1, keepdims=True)
        xc = gv - mu
        vn = xc * lax.rsqrt(jnp.mean(xc * xc, axis=-1, keepdims=True) + EPS) * ng_ref[:, gs] + nb_ref[:, gs]
        vn = vn.astype(BF16)
        w = sw_ref[g_i]
        for c_i in range(TM // SGU_CHUNK):
            rs = slice(c_i * SGU_CHUNK, (c_i + 1) * SGU_CHUNK)
            mixed = _dot(w, vn[rs, :]) + sb_ref[g_i]
            sg_ref[rs, gs] = (u[rs, :] * mixed).astype(BF16)


def _odd_in(x, g, win, ng, nb, sw, sb):
    t = x.shape[0]
    row = lambda w: pl.BlockSpec((TM, w), lambda i: (i, 0))
    return pl.pallas_call(
        _odd_in_body,
        grid=(t // TM,),
        in_specs=[row(D_MODEL), _const_spec(g.shape), _const_spec(win.shape), _const_spec(ng.shape),
                  _const_spec(nb.shape), _const_spec(sw.shape), _const_spec(sb.shape)],
        out_specs=[row(3 * CA_WIDTH), row(SGU_WIDTH)],
        out_shape=[jax.ShapeDtypeStruct((t, 3 * CA_WIDTH), BF16), jax.ShapeDtypeStruct((t, SGU_WIDTH), BF16)],
        compiler_params=_cparams(("parallel",)),
        name="odd_in",
    )(x, g, win, ng, nb, sw, sb)


BAND_BLOCKS = 3


def _band_body(q_ref, k0_ref, k1_ref, k2_ref, v0_ref, v1_ref, v2_ref, bias_ref, o_ref):
    qi = pl.program_id(2)
    lane = lax.broadcasted_iota(I32, (TQ, 2 * CA_HEAD_DIM), 1)
    low = lane < CA_HEAD_DIM
    q_pair = q_ref[...]
    k_refs = (k0_ref, k1_ref, k2_ref)
    v_refs = (v0_ref, v1_ref, v2_ref)
    outs = []
    for hh in range(2):
        q = jnp.where(low if hh == 0 else jnp.logical_not(low), q_pair, jnp.zeros_like(q_pair))
        s = []
        for jj in range(BAND_BLOCKS):
            before_start = qi + jj < BAND_BLOCKS - 1
            pen = jnp.where(before_start, NEG, 0.0)
            s.append(_dot_nt(q, k_refs[jj][...]) + bias_ref[hh, :, jj * TQ:(jj + 1) * TQ] + pen)
        m = functools.reduce(jnp.maximum, [jnp.max(x, axis=-1, keepdims=True) for x in s])
        p = [jnp.exp(x - m) for x in s]
        l = functools.reduce(lambda a, b: a + b, [jnp.sum(x, axis=-1, keepdims=True) for x in p])
        acc = functools.reduce(lambda a, b: a + b,
                               [_dot(p[jj].astype(BF16), v_refs[jj][...]) for jj in range(BAND_BLOCKS)])
        outs.append(acc / l)
    o_ref[...] = jnp.where(low, outs[0], outs[1]).astype(BF16)


def _band_attention(qkv, bias, batch, seq):
    t = qkv.shape[0]
    nq = seq // TQ
    pairs = CA_HEADS // 2
    w = 2 * CA_HEAD_DIM

    def kv_spec(jj, col0):
        return pl.BlockSpec((TQ, w), lambda b, p, i: (b * nq + jnp.maximum(i - (BAND_BLOCKS - 1) + jj, 0), col0 + p))

    return pl.pallas_call(
        _band_body,
        grid=(batch, pairs, nq),
        in_specs=[pl.BlockSpec((TQ, w), lambda b, p, i: (b * nq + i, p))]
                 + [kv_spec(jj, pairs) for jj in range(BAND_BLOCKS)]
                 + [kv_spec(jj, 2 * pairs) for jj in range(BAND_BLOCKS)]
                 + [pl.BlockSpec((2, TQ, BAND_BLOCKS * TQ), lambda b, p, i: (p, 0, 0))],
        out_specs=pl.BlockSpec((TQ, w), lambda b, p, i: (b * nq + i, p)),
        out_shape=jax.ShapeDtypeStruct((t, CA_WIDTH), BF16),
        compiler_params=_cparams(("parallel", "parallel", "arbitrary")),
        name="band_attention",
    )(qkv, qkv, qkv, qkv, qkv, qkv, qkv, bias)


ROUTE_LANES = 128


def _odd_out_body(x_ref, att_ref, sg_ref, wo_ref, g_ref, wr_ref, tri_ref, x1_ref, h_ref, route_ref, cnt_ref, run_ref):
    @pl.when(pl.program_id(0) == 0)
    def _():
        run_ref[...] = jnp.zeros_like(run_ref)

    y = x_ref[...] + _dot(att_ref[...], wo_ref[:CA_WIDTH, :]) + _dot(sg_ref[...], wo_ref[CA_WIDTH:, :])
    x1_ref[...] = y
    h = _rms(y, g_ref[...])
    _store_token_tiles(h_ref, h)
    h_hi = h.astype(BF16)
    h_lo = (h - h_hi.astype(F32)).astype(BF16)
    logits = _dot(h_hi, wr_ref[0]) + (_dot(h_lo, wr_ref[0]) + _dot(h_hi, wr_ref[1]))
    lane = lax.broadcasted_iota(I32, (TM, ROUTE_LANES), 1)
    logits = jnp.where(lane < N_EXPERTS, logits, NEG)
    m1 = jnp.max(logits, axis=-1, keepdims=True)
    e1 = jnp.min(jnp.where(logits == m1, lane, ROUTE_LANES), axis=-1, keepdims=True)
    rest = jnp.where(lane == e1, NEG, logits)
    m2 = jnp.max(rest, axis=-1, keepdims=True)
    e2 = jnp.min(jnp.where(rest == m2, lane, ROUTE_LANES), axis=-1, keepdims=True)
    ex = jnp.exp(m2 - m1)
    w1 = 1.0 / (1.0 + ex)
    w2 = ex / (1.0 + ex)
    hot1 = (lane == e1).astype(F32)
    hot2 = (lane == e2).astype(F32)
    sel = hot1 + hot2
    before = _dot(tri_ref[...], sel.astype(BF16)) + run_ref[0:1, :]
    rank1 = jnp.sum(hot1 * before, axis=-1, keepdims=True)
    rank2 = jnp.sum(hot2 * before, axis=-1, keepdims=True)
    rec = jnp.where(lane == 0, e1.astype(F32), 0.0)
    rec = jnp.where(lane == 1, e2.astype(F32), rec)
    rec = jnp.where(lane == 2, rank1, rec)
    rec = jnp.where(lane == 3, rank2, rec)
    rec = jnp.where(lane == 4, w1, rec)
    rec = jnp.where(lane == 5, w2, rec)
    route_ref[...] = rec
    total = run_ref[0:1, :] + jnp.sum(sel, axis=0, keepdims=True)
    run_ref[...] = jnp.broadcast_to(total, run_ref.shape)
    cnt_ref[...] = jnp.broadcast_to(total, cnt_ref.shape)


def _odd_out(x, att, sg, wo, g, wr, tri):
    t = x.shape[0]
    row = lambda w: pl.BlockSpec((TM, w), lambda i: (i, 0))
    return pl.pallas_call(
        _odd_out_body,
        grid=(t // TM,),
        in_specs=[row(D_MODEL), row(CA_WIDTH), row(SGU_WIDTH), _const_spec(wo.shape), _const_spec(g.shape),
                  _const_spec(wr.shape), _const_spec(tri.shape)],
        out_specs=[row(D_MODEL), pl.BlockSpec((TM * TOKEN_TILE_ROWS, 128), lambda i: (i, 0)), row(ROUTE_LANES),
                   pl.BlockSpec((8, ROUTE_LANES), lambda i: (0, 0))],
        out_shape=[jax.ShapeDtypeStruct((t, D_MODEL), F32), jax.ShapeDtypeStruct((t * TOKEN_TILE_ROWS, 128), F32),
                   jax.ShapeDtypeStruct((t, ROUTE_LANES), F32), jax.ShapeDtypeStruct((8, ROUTE_LANES), F32)],
        scratch_shapes=[pltpu.VMEM((8, ROUTE_LANES), F32)],
        compiler_params=_cparams(("arbitrary",)),
        name="odd_out_router",
    )(x, att, sg, wo, g, wr, tri)


def _moe_body(tile_e_ref, n_used_ref, src_ref, dst_ref, h_hbm, wg_ref, wu_ref, wd_ref, o_hbm,
              xf_ref, xb_ref, acc_ref, y_ref, gsem, ssem):
    i = pl.program_id(0)
    j = pl.program_id(1)
    n_used = n_used_ref[0]
    rows_per_step = MOE_TM // MOE_FF_STEPS

    ttr = TOKEN_TILE_ROWS
    tile_rows = MOE_TM * ttr

    def token_rows(ref, first_row):
        return ref.at[pl.ds(pl.multiple_of(first_row, ttr), ttr), :]

    def gather_copy(tile, r, slot):
        return pltpu.make_async_copy(token_rows(h_hbm, src_ref[tile * MOE_TM + r]),
                                     token_rows(xf_ref.at[slot], r * ttr), gsem)

    def scatter_copy(tile, r, slot):
        return pltpu.make_async_copy(token_rows(y_ref.at[slot], r * ttr),
                                     token_rows(o_hbm, dst_ref[tile * MOE_TM + r]), ssem)

    def wait_tile_gather(slot):
        pltpu.make_async_copy(h_hbm.at[pl.ds(0, tile_rows), :], xf_ref.at[slot], gsem).wait()

    def wait_tile_scatter(slot):
        pltpu.make_async_copy(y_ref.at[slot], o_hbm.at[pl.ds(0, tile_rows), :], ssem).wait()

    def start_all(copy_fn):
        lax.fori_loop(0, MOE_TM, lambda r, c: (copy_fn(r).start(), c)[1], 0)

    @pl.when(i < n_used)
    def _():
        cur = lax.rem(i, 2)
        other = 1 - cur
        nxt_tile = jnp.minimum(i + 1, n_used - 1)
        prev_tile = jnp.maximum(i - 1, 0)

        @pl.when(jnp.logical_and(i == 0, j == 0))
        def _():
            y_ref[...] = jnp.zeros_like(y_ref)
            spare0 = o_hbm.shape[0] - N_EXPERTS * tile_rows
            for e_i in range(N_EXPERTS):
                fill = pltpu.make_async_copy(y_ref.at[0], o_hbm.at[pl.ds(spare0 + e_i * tile_rows, tile_rows), :],
                                             ssem)
                fill.start()
                fill.wait()
            start_all(lambda r: gather_copy(0, r, 0))

        @pl.when(j == 0)
        def _():
            wait_tile_gather(cur)
            xb_ref[...] = _load_token_tiles(xf_ref.at[cur], MOE_TM).astype(BF16)
            acc_ref[...] = jnp.zeros_like(acc_ref)

        base = pl.multiple_of(j * rows_per_step, 8)
        for rr in range(rows_per_step):
            gather_copy(nxt_tile, base + rr, other).start()
            scatter_copy(prev_tile, base + rr, other).start()
        xb = xb_ref[...]
        gate = _dot(xb, wg_ref[...])
        up = _dot(xb, wu_ref[...])
        act = (gate * jax.nn.sigmoid(gate) * up).astype(BF16)
        acc_ref[...] += _dot(act, wd_ref[...])

        @pl.when(j == MOE_FF_STEPS - 1)
        def _():
            wait_tile_scatter(other)
            _store_token_tiles(y_ref.at[cur], acc_ref[...])

        @pl.when(jnp.logical_and(i == n_used - 1, j == MOE_FF_STEPS - 1))
        def _():
            start_all(lambda r: scatter_copy(i, r, cur))
            wait_tile_scatter(cur)
            wait_tile_gather(other)


def _moe(h, tile_e, n_used, src, dst, wg, wu, wd, layer):
    tokens = h.shape[0] // TOKEN_TILE_ROWS
    n_tiles = tile_e.shape[0]
    last = MOE_FF_STEPS - 1

    def ff_idx(i, j, n_used_ref):
        return jnp.where(i < n_used_ref[0], j, last)

    grid_spec = pltpu.PrefetchScalarGridSpec(
        num_scalar_prefetch=4,
        grid=(n_tiles, MOE_FF_STEPS),
        in_specs=[pl.BlockSpec(memory_space=pl.ANY),
                  pl.BlockSpec((None, None, D_MODEL, MOE_TF),
                               lambda i, j, te, nu, sr, ds: (layer, te[i], 0, ff_idx(i, j, nu))),
                  pl.BlockSpec((None, None, D_MODEL, MOE_TF),
                               lambda i, j, te, nu, sr, ds: (layer, te[i], 0, ff_idx(i, j, nu))),
                  pl.BlockSpec((None, None, MOE_TF, D_MODEL),
                               lambda i, j, te, nu, sr, ds: (layer, te[i], ff_idx(i, j, nu), 0))],
        out_specs=pl.BlockSpec(memory_space=pl.ANY),
        scratch_shapes=[pltpu.VMEM((2, MOE_TM * TOKEN_TILE_ROWS, 128), F32), pltpu.VMEM((MOE_TM, D_MODEL), BF16),
                        pltpu.VMEM((MOE_TM, D_MODEL), F32), pltpu.VMEM((2, MOE_TM * TOKEN_TILE_ROWS, 128), F32),
                        pltpu.SemaphoreType.DMA, pltpu.SemaphoreType.DMA],
    )
    return pl.pallas_call(
        _moe_body,
        grid_spec=grid_spec,
        out_shape=jax.ShapeDtypeStruct(((2 * tokens + N_EXPERTS * MOE_TM) * TOKEN_TILE_ROWS, 128), F32),
        compiler_params=_cparams(("arbitrary", "arbitrary")),
        name="moe_experts",
    )(tile_e, n_used, src, dst, h, wg, wu, wd)


def _moe_plan(route, counts, tokens):
    n_tiles = (2 * tokens + N_EXPERTS * (MOE_TM - 1)) // MOE_TM
    experts = jnp.arange(N_EXPERTS, dtype=I32)
    cnt = counts[0, :N_EXPERTS].astype(I32)
    e = route[:, 0:2].astype(I32)
    rank = route[:, 2:4].astype(I32)
    tiles_per_e = (cnt + MOE_TM - 1) // MOE_TM
    tile_end = jnp.cumsum(tiles_per_e)
    tile_start = tile_end - tiles_per_e
    row_start = jnp.sum(jnp.where(e[..., None] == experts, tile_start * MOE_TM, 0), axis=-1)
    pos = row_start + rank
    slot_row = jnp.arange(tokens, dtype=I32)[:, None] + jnp.array([0, tokens], I32)[None, :]
    n_used = tile_end[-1]
    tile = jnp.minimum(jnp.arange(n_tiles, dtype=I32), n_used - 1)
    tile_e = jnp.sum((tile_end[None, :] <= tile[:, None]).astype(I32), axis=1)
    spare = 2 * tokens + tile_e[:, None] * MOE_TM + jnp.arange(MOE_TM, dtype=I32)[None, :]
    dst = spare.reshape(-1).at[pos.reshape(-1)].set(slot_row.reshape(-1))
    src = jnp.where(dst < 2 * tokens, dst % tokens, 0)
    return tile_e, n_used.reshape(1), src * TOKEN_TILE_ROWS, dst * TOKEN_TILE_ROWS


def _combine_body(x_ref, oa_ref, ob_ref, route_ref, g_ref, o_ref, *, final):
    w1 = route_ref[:, 4:5]
    w2 = route_ref[:, 5:6]
    y = x_ref[...] + (w1 * _load_token_tiles(oa_ref, TM) + w2 * _load_token_tiles(ob_ref, TM))
    o_ref[...] = _rms(y, g_ref[...]) if final else y


def _combine(x, o2, route, g, final):
    t = x.shape[0]
    nt = t // TM
    return pl.pallas_call(
        functools.partial(_combine_body, final=final),
        grid=(nt,),
        in_specs=[pl.BlockSpec((TM, D_MODEL), lambda i: (i, 0)),
                  pl.BlockSpec((TM * TOKEN_TILE_ROWS, 128), lambda i: (i, 0)),
                  pl.BlockSpec((TM * TOKEN_TILE_ROWS, 128), lambda i: (i + nt, 0)),
                  pl.BlockSpec((TM, ROUTE_LANES), lambda i: (i, 0)), _const_spec(g.shape)],
        out_specs=pl.BlockSpec((TM, D_MODEL), lambda i: (i, 0)),
        out_shape=jax.ShapeDtypeStruct((t, D_MODEL), F32),
        compiler_params=_cparams(("parallel",)),
        name="moe_combine",
    )(x, o2, o2, route, g)


def _rot_half(w):
    half = w.shape[-1] // 2
    return jnp.concatenate([-w[..., half:], w[..., :half]], axis=-1)


def _rope_tables(seq):
    inv = 1.0 / (ROPE_THETA ** (jnp.arange(0, MLA_ROPE, 2, dtype=F32) / MLA_ROPE))
    ang = jnp.arange(seq, dtype=F32)[:, None] * inv[None, :]
    cos, sin = jnp.cos(ang), jnp.sin(ang)
    pad = HEAD_PAD - MLA_NOPE - MLA_ROPE
    c = jnp.concatenate([jnp.ones((seq, MLA_NOPE), F32), cos, cos, jnp.zeros((seq, pad), F32)], axis=1)
    s = jnp.concatenate([jnp.zeros((seq, MLA_NOPE), F32), sin, sin, jnp.zeros((seq, pad), F32)], axis=1)
    scale = (MLA_NOPE + MLA_ROPE) ** -0.5 * math.log2(math.e)
    return c * scale, s * scale, c, s


def _even_weights(w_in, w_q_up, w_kv_up):
    d = w_in.shape[0]
    pad = HEAD_PAD - MLA_NOPE - MLA_ROPE
    kr = w_in[:, POOL_WIDTH + MLA_Q_RANK + MLA_KV_RANK:]
    place = lambda m: jnp.concatenate([jnp.zeros((d, MLA_NOPE), F32), m, jnp.zeros((d, pad), F32)], axis=1)
    win = jnp.concatenate([w_in[:, :POOL_WIDTH + MLA_Q_RANK + MLA_KV_RANK], place(kr), place(_rot_half(kr))], axis=1)
    wq = w_q_up.reshape(MLA_Q_RANK, MLA_HEADS, MLA_NOPE + MLA_ROPE)
    nope, rope = wq[..., :MLA_NOPE], wq[..., MLA_NOPE:]
    zq = lambda n: jnp.zeros((MLA_Q_RANK, MLA_HEADS, n), F32)
    wq_main = jnp.concatenate([nope, rope, zq(pad)], axis=-1).reshape(MLA_Q_RANK, -1)
    wq_rot = jnp.concatenate([zq(MLA_NOPE), _rot_half(rope), zq(pad)], axis=-1).reshape(MLA_Q_RANK, -1)
    wkv = w_kv_up.reshape(MLA_KV_RANK, MLA_HEADS, MLA_NOPE + MLA_V)
    k_pad = jnp.concatenate([wkv[..., :MLA_NOPE], jnp.zeros((MLA_KV_RANK, MLA_HEADS, HEAD_PAD - MLA_NOPE), F32)],
                            axis=-1).reshape(MLA_KV_RANK, -1)
    v_pairs = wkv[..., MLA_NOPE:].reshape(MLA_KV_RANK, MLA_HEADS // 2, 2, MLA_V)
    v_gap = jnp.zeros((MLA_KV_RANK, MLA_HEADS // 2, 2 * (HEAD_PAD - MLA_V)), F32)
    v_ext = jnp.concatenate([v_pairs[:, :, 0], v_gap, v_pairs[:, :, 1]], axis=-1).reshape(MLA_KV_RANK, -1)
    return (win.astype(BF16), jnp.concatenate([wq_main, wq_rot], axis=1).astype(BF16),
            jnp.concatenate([k_pad, v_ext], axis=1).astype(BF16))


def _v_ones():
    lane = jnp.arange(MLA_HEADS * HEAD_PAD) % (2 * HEAD_PAD)
    return jnp.logical_and(lane >= MLA_V, lane < 2 * HEAD_PAD - MLA_V).astype(F32).reshape(1, -1)


def _band_bias(rel_bias):
    band = BAND_BLOCKS * TQ
    qpos = jnp.arange(TQ)[:, None] + (BAND_BLOCKS - 1) * TQ
    kpos = jnp.arange(band)[None, :]
    rel = (BAND_BLOCKS - 1) * TQ + (TQ - 1) - jnp.arange(band + TQ - 1)
    by_rel = rel_bias[:, jnp.clip(rel, -REL_CLIP, REL_CLIP) + REL_CLIP].astype(F32)
    bias = jnp.stack([by_rel[:, TQ - 1 - q:TQ - 1 - q + band] for q in range(TQ)], axis=1)
    dc = qpos // CHUNK - kpos // CHUNK
    valid = jnp.logical_and(dc >= 0, dc <= CA_LEFT_CHUNKS)
    return jnp.where(valid[None], bias, NEG)


def kernel(x, ev_norm_mix, ev_w_in, ev_pool_w, ev_pool_scale, ev_q_norm, ev_w_q_up, ev_kv_norm, ev_w_kv_up, ev_w_out, ev_norm_ffn, ev_ffn_w_gate, ev_ffn_w_up, ev_ffn_w_down, od_norm_mix, od_w_in, od_rel_bias, od_sgu_norm_g, od_sgu_norm_b, od_sgu_w, od_sgu_b, od_w_out, od_norm_ffn, od_router, od_moe_w_gate, od_moe_w_up, od_moe_w_down, final_norm):
    batch, seq, d = x.shape
    tokens = batch * seq
    assert d == D_MODEL and seq % TM == 0 and TM % SGU_CHUNK == 0 and TM % TQ == 0 and seq % TK == 0
    depth = ev_w_in.shape[0] + od_w_in.shape[0]
    xt = x.reshape(tokens, d)
    tabs = _rope_tables(seq)
    tri = jnp.tril(jnp.ones((TM, TM), F32), k=-1).astype(BF16)
    causal = jnp.tril(jnp.ones((SGU_CHUNK, SGU_CHUNK), bool))
    vec = lambda v: v.reshape(1, -1).astype(F32)
    moe_wg, moe_wu, moe_wd = (w.astype(BF16) for w in (od_moe_w_gate, od_moe_w_up, od_moe_w_down))

    for layer in range(depth):
        i = layer // 2
        if layer % 2 == 0:
            win, wq, wkv = _even_weights(ev_w_in[i], ev_w_q_up[i], ev_w_kv_up[i])
            a, q, k, v = _even_in(xt, vec(ev_norm_mix[i]), win, vec(ev_q_norm[i]), wq, vec(ev_kv_norm[i]), wkv,
                                  _v_ones(), tabs, seq)
            att = _mla_attention(q, k, v, batch, seq)
            xt = _even_out(xt, a, att, ev_pool_w[i].astype(BF16), vec(ev_pool_scale[i]), ev_w_out[i].astype(BF16),
                           vec(ev_norm_ffn[i]), ev_ffn_w_gate[i].astype(BF16), ev_ffn_w_up[i].astype(BF16),
                           ev_ffn_w_down[i].astype(BF16), seq)
        else:
            w_in = od_w_in[i]
            win = jnp.concatenate([w_in[:, :CA_WIDTH] * (CA_HEAD_DIM ** -0.5), w_in[:, CA_WIDTH:]], axis=1).astype(BF16)
            sw = jnp.where(causal[None], od_sgu_w[i], 0.0).astype(BF16)
            sb = jnp.broadcast_to(od_sgu_b[i][:, :, None], (SGU_GROUPS, SGU_CHUNK, SGU_CH)).astype(F32)
            qkv, sg = _odd_in(xt, vec(od_norm_mix[i]), win, vec(od_sgu_norm_g[i]), vec(od_sgu_norm_b[i]), sw, sb)
            att = _band_attention(qkv, _band_bias(od_rel_bias[i]), batch, seq)
            wr = jnp.concatenate([od_router[i], jnp.zeros((d, ROUTE_LANES - N_EXPERTS), F32)], axis=1)
            wr_hi = wr.astype(BF16)
            wr = jnp.stack([wr_hi, (wr - wr_hi.astype(F32)).astype(BF16)])
            x1, h, route, counts = _odd_out(xt, att, sg, od_w_out[i].astype(BF16), vec(od_norm_ffn[i]), wr, tri)
            tile_e, n_used, src, dst = _moe_plan(route, counts, tokens)
            o2 = _moe(h, tile_e, n_used, src, dst, moe_wg, moe_wu, moe_wd, i)
            last = layer == depth - 1
            xt = _combine(x1, o2, route, vec(final_norm), last)
    return xt.reshape(batch, seq, d)
```

```python
import functools
import math

import jax
import jax.numpy as jnp
from jax import lax
from jax.experimental import pallas as pl
from jax.experimental.pallas import tpu as pltpu

F32 = jnp.float32
BF16 = jnp.bfloat16
I32 = jnp.int32

D_MODEL = 1024
CHUNK = 64
EPS = 1e-6

POOL_GROUPS = 4
POOL_CH = 128
POOL_WIDTH = 512
POOL_WINDOWS = (2, 4, 8, 16)
POOL_HALO = 16

MLA_HEADS = 8
MLA_NOPE = 64
MLA_ROPE = 32
MLA_V = 64
MLA_Q_RANK = 256
MLA_KV_RANK = 128
ROPE_THETA = 10000.0
HEAD_PAD = 128

CA_HEADS = 8
CA_HEAD_DIM = 64
CA_LEFT_CHUNKS = 8
REL_CLIP = 128
CA_WIDTH = CA_HEADS * CA_HEAD_DIM

SGU_GROUPS = 4
SGU_CH = 128
SGU_WIDTH = 512
SGU_CHUNK = 128

D_FF = 2816
N_EXPERTS = 8
EXPERT_FF = 3584

NEG = -1e30

TM = 512
TQ = 256
TK = 512
MLA_TQ = 512
MOE_TF = 512
MOE_FF_STEPS = EXPERT_FF // MOE_TF
MOE_TM = 80 * MOE_FF_STEPS
VMEM_LIMIT = 56 * 1024 * 1024


def _cparams(sem):
    return pltpu.CompilerParams(dimension_semantics=sem, vmem_limit_bytes=VMEM_LIMIT)


def _rms(x, g):
    return x * lax.rsqrt(jnp.mean(x * x, axis=-1, keepdims=True) + EPS) * g


def _dot(a, b):
    return jnp.dot(a, b, preferred_element_type=F32)


def _dot_nt(a, b):
    return lax.dot_general(a, b, (((1,), (1,)), ((), ())), preferred_element_type=F32)


TOKEN_TILE_ROWS = D_MODEL // 128


def _store_token_tiles(ref, x):
    n = x.shape[0]
    for s in range(TOKEN_TILE_ROWS):
        ref[pl.ds(s, n, stride=TOKEN_TILE_ROWS), :] = x[:, s * 128:(s + 1) * 128]


def _load_token_tiles(ref, n):
    return jnp.concatenate([ref[pl.ds(s, n, stride=TOKEN_TILE_ROWS), :] for s in range(TOKEN_TILE_ROWS)], axis=1)


def _const_spec(shape):
    nd = len(shape)
    return pl.BlockSpec(shape, lambda *_: (0,) * nd, pipeline_mode=pl.Buffered(1))


def _even_in_body(x_ref, g_ref, win_ref, qn_ref, wq_ref, kvn_ref, wkv_ref, vones_ref, cq_ref, sq_ref, ck_ref, sk_ref,
                  a_ref, q_ref, k_ref, v_ref):
    h = _rms(x_ref[...], g_ref[...]).astype(BF16)
    z = _dot(h, win_ref[...])
    a_ref[...] = z[:, :POOL_WIDTH]
    o = POOL_WIDTH
    cq = z[:, o:o + MLA_Q_RANK]
    o += MLA_Q_RANK
    ckv = z[:, o:o + MLA_KV_RANK]
    o += MLA_KV_RANK
    kr = z[:, o:o + HEAD_PAD]
    kr_rot = z[:, o + HEAD_PAD:o + 2 * HEAD_PAD]

    hw = MLA_HEADS * HEAD_PAD
    qq = _dot(_rms(cq, qn_ref[...]).astype(BF16), wq_ref[...])
    cq_t, sq_t = cq_ref[...], sq_ref[...]
    for h_i in range(MLA_HEADS):
        sl = slice(h_i * HEAD_PAD, (h_i + 1) * HEAD_PAD)
        sl_rot = slice(hw + h_i * HEAD_PAD, hw + (h_i + 1) * HEAD_PAD)
        q_ref[:, sl] = (qq[:, sl] * cq_t + qq[:, sl_rot] * sq_t).astype(BF16)

    kv = _dot(_rms(ckv, kvn_ref[...]).astype(BF16), wkv_ref[...])
    k_rope = kr * ck_ref[...] + kr_rot * sk_ref[...]
    for h_i in range(MLA_HEADS):
        sl = slice(h_i * HEAD_PAD, (h_i + 1) * HEAD_PAD)
        k_ref[:, sl] = (kv[:, sl] + k_rope).astype(BF16)
    v_ref[...] = (kv[:, hw:] + vones_ref[...]).astype(BF16)


def _even_in(x, g, win, qn, wq, kvn, wkv, vones, tabs, seq):
    t = x.shape[0]
    hw = MLA_HEADS * HEAD_PAD
    row = lambda w: pl.BlockSpec((TM, w), lambda i: (i, 0))
    tab = pl.BlockSpec((TM, HEAD_PAD), lambda i: (i % (seq // TM), 0))
    return pl.pallas_call(
        _even_in_body,
        grid=(t // TM,),
        in_specs=[row(D_MODEL), _const_spec(g.shape), _const_spec(win.shape), _const_spec(qn.shape),
                  _const_spec(wq.shape), _const_spec(kvn.shape), _const_spec(wkv.shape), _const_spec(vones.shape),
                  tab, tab, tab, tab],
        out_specs=[row(POOL_WIDTH), row(hw), row(hw), row(hw)],
        out_shape=[jax.ShapeDtypeStruct((t, POOL_WIDTH), F32), jax.ShapeDtypeStruct((t, hw), BF16),
                   jax.ShapeDtypeStruct((t, hw), BF16), jax.ShapeDtypeStruct((t, hw), BF16)],
        compiler_params=_cparams(("parallel",)),
        name="even_in",
    )(x, g, win, qn, wq, kvn, wkv, vones, *tabs)


def _mla_body(q_ref, k_ref, v_ref, o_ref, s_ref, mx_ref, acc_ref):
    qi = pl.program_id(2)
    last = (qi * MLA_TQ) // TK
    halves = (slice(0, HEAD_PAD), slice(HEAD_PAD, 2 * HEAD_PAD))
    lane_blocks = [slice(c * 128, (c + 1) * 128) for c in range(TK // 128)]
    q_chunk = lax.broadcasted_iota(I32, (MLA_TQ, TK), 0) // CHUNK + qi * (MLA_TQ // CHUNK)
    k_chunk = lax.broadcasted_iota(I32, (MLA_TQ, TK), 1) // CHUNK + last * (TK // CHUNK)
    visible = k_chunk <= q_chunk

    mx_ref[...] = jnp.full(mx_ref.shape, NEG, F32)
    acc_ref[...] = jnp.zeros_like(acc_ref)

    def scores(j, masked):
        rows = pl.ds(pl.multiple_of(j * TK, TK), TK)
        for hh, hs in enumerate(halves):
            s = _dot_nt(q_ref[:, hs], k_ref[rows, hs])
            if masked:
                s = jnp.where(visible, s, NEG)
            s_ref[hh, j] = s
            mx_ref[hh] = functools.reduce(jnp.maximum, [s[:, lb] for lb in lane_blocks], mx_ref[hh])

    def pairs_then_rest(count, body):
        def two(jj, c):
            body(2 * jj)
            body(2 * jj + 1)
            return c

        lax.fori_loop(0, count // 2, two, 0)

        @pl.when(count % 2 == 1)
        def _():
            body(count - 1)

    pairs_then_rest(last, lambda j: scores(j, False))
    scores(last, True)

    for hh in range(2):
        m = jnp.max(mx_ref[hh], axis=-1, keepdims=True)
        mx_ref[hh] = jnp.broadcast_to(m, (MLA_TQ, 128))

    def weighted(j):
        rows = pl.ds(pl.multiple_of(j * TK, TK), TK)
        for hh, hs in enumerate(halves):
            m = mx_ref[hh]
            p = jnp.concatenate([jnp.exp2(s_ref[hh, j, :, lb] - m) for lb in lane_blocks], axis=1)
            acc_ref[hh] += _dot(p.astype(BF16), v_ref[rows, hs])

    pairs_then_rest(last + 1, weighted)

    even, odd = acc_ref[0], acc_ref[1]
    lane = lax.broadcasted_iota(I32, (MLA_TQ, 2 * MLA_V), 1)
    out = jnp.where(lane < MLA_V, even / even[:, MLA_V:MLA_V + 1], odd / odd[:, 0:1])
    o_ref[...] = out.astype(BF16)


def _mla_attention(q, k, v, batch, seq):
    t = q.shape[0]
    nq = seq // MLA_TQ
    pairs = MLA_HEADS // 2
    return pl.pallas_call(
        _mla_body,
        grid=(batch, pairs, nq),
        in_specs=[pl.BlockSpec((MLA_TQ, 2 * HEAD_PAD), lambda b, p, i: (b * nq + i, p)),
                  pl.BlockSpec((seq, 2 * HEAD_PAD), lambda b, p, i: (b, p)),
                  pl.BlockSpec((seq, 2 * HEAD_PAD), lambda b, p, i: (b, p))],
        out_specs=pl.BlockSpec((MLA_TQ, 2 * MLA_V), lambda b, p, i: (b * nq + i, p)),
        out_shape=jax.ShapeDtypeStruct((t, MLA_HEADS * MLA_V), BF16),
        scratch_shapes=[pltpu.VMEM((2, seq // TK, MLA_TQ, TK), F32), pltpu.VMEM((2, MLA_TQ, 128), F32),
                        pltpu.VMEM((2, MLA_TQ, HEAD_PAD), F32)],
        compiler_params=_cparams(("parallel", "parallel", "arbitrary")),
        name="mla_attention",
    )(q, k, v)


def _even_out_body(x_ref, a_ref, halo_ref, att_ref, pw_ref, ps_ref, wo_ref, g_ref, wg_ref, wu_ref, wd_ref,
                   o_ref, ext_ref, *, seq):
    pos0 = (pl.program_id(0) * TM) % seq
    ext_ref[0:POOL_HALO, :] = jnp.where(pos0 == 0, 0.0, halo_ref[...])
    ext_ref[POOL_HALO:, :] = a_ref[...]
    pos = lax.broadcasted_iota(I32, (TM, 1), 0) + pos0
    y = x_ref[...] + _dot(att_ref[...], wo_ref[POOL_WIDTH:, :])
    for g_i, win in enumerate(POOL_WINDOWS):
        gs = slice(g_i * POOL_CH, (g_i + 1) * POOL_CH)
        a_g = ext_ref[POOL_HALO:, gs]
        ssum = a_g
        for j in range(1, win):
            ssum = ssum + ext_ref[POOL_HALO - j:POOL_HALO - j + TM, gs]
        count = jnp.minimum(pos + 1, win).astype(F32)
        pooled = (ssum / count - a_g).astype(BF16)
        mixed = (_dot(pooled, pw_ref[g_i]) * ps_ref[:, gs]).astype(BF16)
        y = y + _dot(mixed, wo_ref[gs, :])
    h = _rms(y, g_ref[...]).astype(BF16)
    gate = _dot(h, wg_ref[...])
    up = _dot(h, wu_ref[...])
    act = (gate * jax.nn.sigmoid(gate) * up).astype(BF16)
    o_ref[...] = y + _dot(act, wd_ref[...])


def _even_out(x, a, att, pw, ps, wo, g, wg, wu, wd, seq):
    t = x.shape[0]
    row = lambda w: pl.BlockSpec((TM, w), lambda i: (i, 0))
    halo = pl.BlockSpec((POOL_HALO, POOL_WIDTH), lambda i: (jnp.maximum(i * (TM // POOL_HALO) - 1, 0), 0))
    return pl.pallas_call(
        functools.partial(_even_out_body, seq=seq),
        grid=(t // TM,),
        in_specs=[row(D_MODEL), row(POOL_WIDTH), halo, row(MLA_HEADS * MLA_V), _const_spec(pw.shape),
                  _const_spec(ps.shape), _const_spec(wo.shape), _const_spec(g.shape), _const_spec(wg.shape),
                  _const_spec(wu.shape), _const_spec(wd.shape)],
        out_specs=row(D_MODEL),
        out_shape=jax.ShapeDtypeStruct((t, D_MODEL), F32),
        scratch_shapes=[pltpu.VMEM((TM + POOL_HALO, POOL_WIDTH), F32)],
        compiler_params=_cparams(("parallel",)),
        name="even_out",
    )(x, a, a, att, pw, ps, wo, g, wg, wu, wd)


def _odd_in_body(x_ref, g_ref, win_ref, ng_ref, nb_ref, sw_ref, sb_ref, qkv_ref, sg_ref):
    h = _rms(x_ref[...], g_ref[...]).astype(BF16)
    z = _dot(h, win_ref[...])
    qkv_ref[...] = z[:, :3 * CA_WIDTH].astype(BF16)
    uv = jax.nn.gelu(z[:, 3 * CA_WIDTH:])
    for g_i in range(SGU_GROUPS):
        gs = slice(g_i * SGU_CH, (g_i + 1) * SGU_CH)
        u = uv[:, gs]
        gv = uv[:, SGU_WIDTH + g_i * SGU_CH:SGU_WIDTH + (g_i + 1) * SGU_CH]
        mu = jnp.mean(gv, axis=-1, keepdims=True)
        xc = gv - mu
        vn = xc * lax.rsqrt(jnp.mean(xc * xc, axis=-1, keepdims=True) + EPS) * ng_ref[:, gs] + nb_ref[:, gs]
        vn = vn.astype(BF16)
        w = sw_ref[g_i]
        for c_i in range(TM // SGU_CHUNK):
            rs = slice(c_i * SGU_CHUNK, (c_i + 1) * SGU_CHUNK)
            mixed = _dot(w, vn[rs, :]) + sb_ref[g_i]
            sg_ref[rs, gs] = (u[rs, :] * mixed).astype(BF16)


def _odd_in(x, g, win, ng, nb, sw, sb):
    t = x.shape[0]
    row = lambda w: pl.BlockSpec((TM, w), lambda i: (i, 0))
    return pl.pallas_call(
        _odd_in_body,
        grid=(t // TM,),
        in_specs=[row(D_MODEL), _const_spec(g.shape), _const_spec(win.shape), _const_spec(ng.shape),
                  _const_spec(nb.shape), _const_spec(sw.shape), _const_spec(sb.shape)],
        out_specs=[row(3 * CA_WIDTH), row(SGU_WIDTH)],
        out_shape=[jax.ShapeDtypeStruct((t, 3 * CA_WIDTH), BF16), jax.ShapeDtypeStruct((t, SGU_WIDTH), BF16)],
        compiler_params=_cparams(("parallel",)),
        name="odd_in",
    )(x, g, win, ng, nb, sw, sb)


BAND_BLOCKS = 3


def _band_body(q_ref, k0_ref, k1_ref, k2_ref, v0_ref, v1_ref, v2_ref, bias_ref, o_ref):
    qi = pl.program_id(2)
    lane = lax.broadcasted_iota(I32, (TQ, 2 * CA_HEAD_DIM), 1)
    low = lane < CA_HEAD_DIM
    q_pair = q_ref[...]
    k_refs = (k0_ref, k1_ref, k2_ref)
    v_refs = (v0_ref, v1_ref, v2_ref)
    outs = []
    for hh in range(2):
        q = jnp.where(low if hh == 0 else jnp.logical_not(low), q_pair, jnp.zeros_like(q_pair))
        s = []
        for jj in range(BAND_BLOCKS):
            before_start = qi + jj < BAND_BLOCKS - 1
            pen = jnp.where(before_start, NEG, 0.0)
            s.append(_dot_nt(q, k_refs[jj][...]) + bias_ref[hh, :, jj * TQ:(jj + 1) * TQ] + pen)
        m = functools.reduce(jnp.maximum, [jnp.max(x, axis=-1, keepdims=True) for x in s])
        p = [jnp.exp(x - m) for x in s]
        l = functools.reduce(lambda a, b: a + b, [jnp.sum(x, axis=-1, keepdims=True) for x in p])
        acc = functools.reduce(lambda a, b: a + b,
                               [_dot(p[jj].astype(BF16), v_refs[jj][...]) for jj in range(BAND_BLOCKS)])
        outs.append(acc / l)
    o_ref[...] = jnp.where(low, outs[0], outs[1]).astype(BF16)


def _band_attention(qkv, bias, batch, seq):
    t = qkv.shape[0]
    nq = seq // TQ
    pairs = CA_HEADS // 2
    w = 2 * CA_HEAD_DIM

    def kv_spec(jj, col0):
        return pl.BlockSpec((TQ, w), lambda b, p, i: (b * nq + jnp.maximum(i - (BAND_BLOCKS - 1) + jj, 0), col0 + p))

    return pl.pallas_call(
        _band_body,
        grid=(batch, pairs, nq),
        in_specs=[pl.BlockSpec((TQ, w), lambda b, p, i: (b * nq + i, p))]
                 + [kv_spec(jj, pairs) for jj in range(BAND_BLOCKS)]
                 + [kv_spec(jj, 2 * pairs) for jj in range(BAND_BLOCKS)]
                 + [pl.BlockSpec((2, TQ, BAND_BLOCKS * TQ), lambda b, p, i: (p, 0, 0))],
        out_specs=pl.BlockSpec((TQ, w), lambda b, p, i: (b * nq + i, p)),
        out_shape=jax.ShapeDtypeStruct((t, CA_WIDTH), BF16),
        compiler_params=_cparams(("parallel", "parallel", "arbitrary")),
        name="band_attention",
    )(qkv, qkv, qkv, qkv, qkv, qkv, qkv, bias)


ROUTE_LANES = 128


def _odd_out_body(x_ref, att_ref, sg_ref, wo_ref, g_ref, wr_ref, tri_ref, x1_ref, h_ref, route_ref, cnt_ref, run_ref):
    @pl.when(pl.program_id(0) == 0)
    def _():
        run_ref[...] = jnp.zeros_like(run_ref)

    y = x_ref[...] + _dot(att_ref[...], wo_ref[:CA_WIDTH, :]) + _dot(sg_ref[...], wo_ref[CA_WIDTH:, :])
    x1_ref[...] = y
    h = _rms(y, g_ref[...])
    _store_token_tiles(h_ref, h)
    h_hi = h.astype(BF16)
    h_lo = (h - h_hi.astype(F32)).astype(BF16)
    logits = _dot(h_hi, wr_ref[0]) + (_dot(h_lo, wr_ref[0]) + _dot(h_hi, wr_ref[1]))
    lane = lax.broadcasted_iota(I32, (TM, ROUTE_LANES), 1)
    logits = jnp.where(lane < N_EXPERTS, logits, NEG)
    m1 = jnp.max(logits, axis=-1, keepdims=True)
    e1 = jnp.min(jnp.where(logits == m1, lane, ROUTE_LANES), axis=-1, keepdims=True)
    rest = jnp.where(lane == e1, NEG, logits)
    m2 = jnp.max(rest, axis=-1, keepdims=True)
    e2 = jnp.min(jnp.where(rest == m2, lane, ROUTE_LANES), axis=-1, keepdims=True)
    ex = jnp.exp(m2 - m1)
    w1 = 1.0 / (1.0 + ex)
    w2 = ex / (1.0 + ex)
    hot1 = (lane == e1).astype(F32)
    hot2 = (lane == e2).astype(F32)
    sel = hot1 + hot2
    before = _dot(tri_ref[...], sel.astype(BF16)) + run_ref[0:1, :]
    rank1 = jnp.sum(hot1 * before, axis=-1, keepdims=True)
    rank2 = jnp.sum(hot2 * before, axis=-1, keepdims=True)
    rec = jnp.where(lane == 0, e1.astype(F32), 0.0)
    rec = jnp.where(lane == 1, e2.astype(F32), rec)
    rec = jnp.where(lane == 2, rank1, rec)
    rec = jnp.where(lane == 3, rank2, rec)
    rec = jnp.where(lane == 4, w1, rec)
    rec = jnp.where(lane == 5, w2, rec)
    route_ref[...] = rec
    total = run_ref[0:1, :] + jnp.sum(sel, axis=0, keepdims=True)
    run_ref[...] = jnp.broadcast_to(total, run_ref.shape)
    cnt_ref[...] = jnp.broadcast_to(total, cnt_ref.shape)


def _odd_out(x, att, sg, wo, g, wr, tri):
    t = x.shape[0]
    row = lambda w: pl.BlockSpec((TM, w), lambda i: (i, 0))
    return pl.pallas_call(
        _odd_out_body,
        grid=(t // TM,),
        in_specs=[row(D_MODEL), row(CA_WIDTH), row(SGU_WIDTH), _const_spec(wo.shape), _const_spec(g.shape),
                  _const_spec(wr.shape), _const_spec(tri.shape)],
        out_specs=[row(D_MODEL), pl.BlockSpec((TM * TOKEN_TILE_ROWS, 128), lambda i: (i, 0)), row(ROUTE_LANES),
                   pl.BlockSpec((8, ROUTE_LANES), lambda i: (0, 0))],
        out_shape=[jax.ShapeDtypeStruct((t, D_MODEL), F32), jax.ShapeDtypeStruct((t * TOKEN_TILE_ROWS, 128), F32),
                   jax.ShapeDtypeStruct((t, ROUTE_LANES), F32), jax.ShapeDtypeStruct((8, ROUTE_LANES), F32)],
        scratch_shapes=[pltpu.VMEM((8, ROUTE_LANES), F32)],
        compiler_params=_cparams(("arbitrary",)),
        name="odd_out_router",
    )(x, att, sg, wo, g, wr, tri)


def _moe_body(tile_e_ref, n_used_ref, src_ref, dst_ref, h_hbm, wg_ref, wu_ref, wd_ref, o_hbm,
              xf_ref, xb_ref, acc_ref, y_ref, gsem, ssem):
    i = pl.program_id(0)
    j = pl.program_id(1)
    n_used = n_used_ref[0]
    rows_per_step = MOE_TM // MOE_FF_STEPS

    ttr = TOKEN_TILE_ROWS
    tile_rows = MOE_TM * ttr

    def token_rows(ref, first_row):
        return ref.at[pl.ds(pl.multiple_of(first_row, ttr), ttr), :]

    def gather_copy(tile, r, slot):
        return pltpu.make_async_copy(token_rows(h_hbm, src_ref[tile * MOE_TM + r]),
                                     token_rows(xf_ref.at[slot], r * ttr), gsem)

    def scatter_copy(tile, r, slot):
        return pltpu.make_async_copy(token_rows(y_ref.at[slot], r * ttr),
                                     token_rows(o_hbm, dst_ref[tile * MOE_TM + r]), ssem)

    def wait_tile_gather(slot):
        pltpu.make_async_copy(h_hbm.at[pl.ds(0, tile_rows), :], xf_ref.at[slot], gsem).wait()

    def wait_tile_scatter(slot):
        pltpu.make_async_copy(y_ref.at[slot], o_hbm.at[pl.ds(0, tile_rows), :], ssem).wait()

    def start_all(copy_fn):
        lax.fori_loop(0, MOE_TM, lambda r, c: (copy_fn(r).start(), c)[1], 0)

    @pl.when(i < n_used)
    def _():
        cur = lax.rem(i, 2)
        other = 1 - cur
        nxt_tile = jnp.minimum(i + 1, n_used - 1)
        prev_tile = jnp.maximum(i - 1, 0)

        @pl.when(jnp.logical_and(i == 0, j == 0))
        def _():
            y_ref[...] = jnp.zeros_like(y_ref)
            spare0 = o_hbm.shape[0] - N_EXPERTS * tile_rows
            for e_i in range(N_EXPERTS):
                fill = pltpu.make_async_copy(y_ref.at[0], o_hbm.at[pl.ds(spare0 + e_i * tile_rows, tile_rows), :],
                                             ssem)
                fill.start()
                fill.wait()
            start_all(lambda r: gather_copy(0, r, 0))

        @pl.when(j == 0)
        def _():
            wait_tile_gather(cur)
            xb_ref[...] = _load_token_tiles(xf_ref.at[cur], MOE_TM).astype(BF16)
            acc_ref[...] = jnp.zeros_like(acc_ref)

        base = pl.multiple_of(j * rows_per_step, 8)
        for rr in range(rows_per_step):
            gather_copy(nxt_tile, base + rr, other).start()
            scatter_copy(prev_tile, base + rr, other).start()
        xb = xb_ref[...]
        gate = _dot(xb, wg_ref[...])
        up = _dot(xb, wu_ref[...])
        act = (gate * jax.nn.sigmoid(gate) * up).astype(BF16)
        acc_ref[...] += _dot(act, wd_ref[...])

        @pl.when(j == MOE_FF_STEPS - 1)
        def _():
            wait_tile_scatter(other)
            _store_token_tiles(y_ref.at[cur], acc_ref[...])

        @pl.when(jnp.logical_and(i == n_used - 1, j == MOE_FF_STEPS - 1))
        def _():
            start_all(lambda r: scatter_copy(i, r, cur))
            wait_tile_scatter(cur)
            wait_tile_gather(other)


def _moe(h, tile_e, n_used, src, dst, wg, wu, wd, layer):
    tokens = h.shape[0] // TOKEN_TILE_ROWS
    n_tiles = tile_e.shape[0]
    last = MOE_FF_STEPS - 1

    def ff_idx(i, j, n_used_ref):
        return jnp.where(i < n_used_ref[0], j, last)

    grid_spec = pltpu.PrefetchScalarGridSpec(
        num_scalar_prefetch=4,
        grid=(n_tiles, MOE_FF_STEPS),
        in_specs=[pl.BlockSpec(memory_space=pl.ANY),
                  pl.BlockSpec((None, None, None, D_MODEL, MOE_TF),
                               lambda i, j, te, nu, sr, ds: (layer, te[i], ff_idx(i, j, nu), 0, 0)),
                  pl.BlockSpec((None, None, None, D_MODEL, MOE_TF),
                               lambda i, j, te, nu, sr, ds: (layer, te[i], ff_idx(i, j, nu), 0, 0)),
                  pl.BlockSpec((None, None, None, MOE_TF, D_MODEL),
                               lambda i, j, te, nu, sr, ds: (layer, te[i], ff_idx(i, j, nu), 0, 0))],
        out_specs=pl.BlockSpec(memory_space=pl.ANY),
        scratch_shapes=[pltpu.VMEM((2, MOE_TM * TOKEN_TILE_ROWS, 128), F32), pltpu.VMEM((MOE_TM, D_MODEL), BF16),
                        pltpu.VMEM((MOE_TM, D_MODEL), F32), pltpu.VMEM((2, MOE_TM * TOKEN_TILE_ROWS, 128), F32),
                        pltpu.SemaphoreType.DMA, pltpu.SemaphoreType.DMA],
    )
    return pl.pallas_call(
        _moe_body,
        grid_spec=grid_spec,
        out_shape=jax.ShapeDtypeStruct(((2 * tokens + N_EXPERTS * MOE_TM) * TOKEN_TILE_ROWS, 128), F32),
        compiler_params=_cparams(("arbitrary", "arbitrary")),
        name="moe_experts",
    )(tile_e, n_used, src, dst, h, wg, wu, wd)


def _moe_plan(route, counts, tokens):
    n_tiles = (2 * tokens + N_EXPERTS * (MOE_TM - 1)) // MOE_TM
    experts = jnp.arange(N_EXPERTS, dtype=I32)
    cnt = counts[0, :N_EXPERTS].astype(I32)
    e = route[:, 0:2].astype(I32)
    rank = route[:, 2:4].astype(I32)
    tiles_per_e = (cnt + MOE_TM - 1) // MOE_TM
    tile_end = jnp.cumsum(tiles_per_e)
    tile_start = tile_end - tiles_per_e
    row_start = jnp.sum(jnp.where(e[..., None] == experts, tile_start * MOE_TM, 0), axis=-1)
    pos = row_start + rank
    slot_row = jnp.arange(tokens, dtype=I32)[:, None] + jnp.array([0, tokens], I32)[None, :]
    n_used = tile_end[-1]
    tile = jnp.minimum(jnp.arange(n_tiles, dtype=I32), n_used - 1)
    tile_e = jnp.sum((tile_end[None, :] <= tile[:, None]).astype(I32), axis=1)
    spare = 2 * tokens + tile_e[:, None] * MOE_TM + jnp.arange(MOE_TM, dtype=I32)[None, :]
    dst = spare.reshape(-1).at[pos.reshape(-1)].set(slot_row.reshape(-1))
    src = jnp.where(dst < 2 * tokens, dst % tokens, 0)
    return tile_e, n_used.reshape(1), src * TOKEN_TILE_ROWS, dst * TOKEN_TILE_ROWS


def _combine_body(x_ref, oa_ref, ob_ref, route_ref, g_ref, o_ref, *, final):
    w1 = route_ref[:, 4:5]
    w2 = route_ref[:, 5:6]
    y = x_ref[...] + (w1 * _load_token_tiles(oa_ref, TM) + w2 * _load_token_tiles(ob_ref, TM))
    o_ref[...] = _rms(y, g_ref[...]) if final else y


def _combine(x, o2, route, g, final):
    t = x.shape[0]
    nt = t // TM
    return pl.pallas_call(
        functools.partial(_combine_body, final=final),
        grid=(nt,),
        in_specs=[pl.BlockSpec((TM, D_MODEL), lambda i: (i, 0)),
                  pl.BlockSpec((TM * TOKEN_TILE_ROWS, 128), lambda i: (i, 0)),
                  pl.BlockSpec((TM * TOKEN_TILE_ROWS, 128), lambda i: (i + nt, 0)),
                  pl.BlockSpec((TM, ROUTE_LANES), lambda i: (i, 0)), _const_spec(g.shape)],
        out_specs=pl.BlockSpec((TM, D_MODEL), lambda i: (i, 0)),
        out_shape=jax.ShapeDtypeStruct((t, D_MODEL), F32),
        compiler_params=_cparams(("parallel",)),
        name="moe_combine",
    )(x, o2, o2, route, g)


def _rot_half(w):
    half = w.shape[-1] // 2
    return jnp.concatenate([-w[..., half:], w[..., :half]], axis=-1)


def _rope_tables(seq):
    inv = 1.0 / (ROPE_THETA ** (jnp.arange(0, MLA_ROPE, 2, dtype=F32) / MLA_ROPE))
    ang = jnp.arange(seq, dtype=F32)[:, None] * inv[None, :]
    cos, sin = jnp.cos(ang), jnp.sin(ang)
    pad = HEAD_PAD - MLA_NOPE - MLA_ROPE
    c = jnp.concatenate([jnp.ones((seq, MLA_NOPE), F32), cos, cos, jnp.zeros((seq, pad), F32)], axis=1)
    s = jnp.concatenate([jnp.zeros((seq, MLA_NOPE), F32), sin, sin, jnp.zeros((seq, pad), F32)], axis=1)
    scale = (MLA_NOPE + MLA_ROPE) ** -0.5 * math.log2(math.e)
    return c * scale, s * scale, c, s


def _even_weights(w_in, w_q_up, w_kv_up):
    d = w_in.shape[0]
    pad = HEAD_PAD - MLA_NOPE - MLA_ROPE
    kr = w_in[:, POOL_WIDTH + MLA_Q_RANK + MLA_KV_RANK:]
    place = lambda m: jnp.concatenate([jnp.zeros((d, MLA_NOPE), F32), m, jnp.zeros((d, pad), F32)], axis=1)
    win = jnp.concatenate([w_in[:, :POOL_WIDTH + MLA_Q_RANK + MLA_KV_RANK], place(kr), place(_rot_half(kr))], axis=1)
    wq = w_q_up.reshape(MLA_Q_RANK, MLA_HEADS, MLA_NOPE + MLA_ROPE)
    nope, rope = wq[..., :MLA_NOPE], wq[..., MLA_NOPE:]
    zq = lambda n: jnp.zeros((MLA_Q_RANK, MLA_HEADS, n), F32)
    wq_main = jnp.concatenate([nope, rope, zq(pad)], axis=-1).reshape(MLA_Q_RANK, -1)
    wq_rot = jnp.concatenate([zq(MLA_NOPE), _rot_half(rope), zq(pad)], axis=-1).reshape(MLA_Q_RANK, -1)
    wkv = w_kv_up.reshape(MLA_KV_RANK, MLA_HEADS, MLA_NOPE + MLA_V)
    k_pad = jnp.concatenate([wkv[..., :MLA_NOPE], jnp.zeros((MLA_KV_RANK, MLA_HEADS, HEAD_PAD - MLA_NOPE), F32)],
                            axis=-1).reshape(MLA_KV_RANK, -1)
    v_pairs = wkv[..., MLA_NOPE:].reshape(MLA_KV_RANK, MLA_HEADS // 2, 2, MLA_V)
    v_gap = jnp.zeros((MLA_KV_RANK, MLA_HEADS // 2, 2 * (HEAD_PAD - MLA_V)), F32)
    v_ext = jnp.concatenate([v_pairs[:, :, 0], v_gap, v_pairs[:, :, 1]], axis=-1).reshape(MLA_KV_RANK, -1)
    return (win.astype(BF16), jnp.concatenate([wq_main, wq_rot], axis=1).astype(BF16),
            jnp.concatenate([k_pad, v_ext], axis=1).astype(BF16))


def _v_ones():
    lane = jnp.arange(MLA_HEADS * HEAD_PAD) % (2 * HEAD_PAD)
    return jnp.logical_and(lane >= MLA_V, lane < 2 * HEAD_PAD - MLA_V).astype(F32).reshape(1, -1)


def _band_bias(rel_bias):
    band = BAND_BLOCKS * TQ
    qpos = jnp.arange(TQ)[:, None] + (BAND_BLOCKS - 1) * TQ
    kpos = jnp.arange(band)[None, :]
    span = band + TQ - 1
    rel = (BAND_BLOCKS - 1) * TQ + (TQ - 1) - jnp.arange(span)
    by_rel = rel_bias[:, jnp.clip(rel, -REL_CLIP, REL_CLIP) + REL_CLIP].astype(F32)
    rolled = jnp.roll(by_rel, -(TQ - 1), axis=1)
    bias = jnp.tile(rolled, (1, TQ))[:, :TQ * (span - 1)].reshape(-1, TQ, span - 1)[:, :, :band]
    dc = qpos // CHUNK - kpos // CHUNK
    valid = jnp.logical_and(dc >= 0, dc <= CA_LEFT_CHUNKS)
    return jnp.where(valid[None], bias, NEG)


def kernel(x, ev_norm_mix, ev_w_in, ev_pool_w, ev_pool_scale, ev_q_norm, ev_w_q_up, ev_kv_norm, ev_w_kv_up, ev_w_out, ev_norm_ffn, ev_ffn_w_gate, ev_ffn_w_up, ev_ffn_w_down, od_norm_mix, od_w_in, od_rel_bias, od_sgu_norm_g, od_sgu_norm_b, od_sgu_w, od_sgu_b, od_w_out, od_norm_ffn, od_router, od_moe_w_gate, od_moe_w_up, od_moe_w_down, final_norm):
    batch, seq, d = x.shape
    tokens = batch * seq
    assert d == D_MODEL and seq % TM == 0 and TM % SGU_CHUNK == 0 and TM % TQ == 0 and seq % TK == 0
    depth = ev_w_in.shape[0] + od_w_in.shape[0]
    xt = x.reshape(tokens, d)
    tabs = _rope_tables(seq)
    tri = jnp.tril(jnp.ones((TM, TM), F32), k=-1).astype(BF16)
    causal = jnp.tril(jnp.ones((SGU_CHUNK, SGU_CHUNK), bool))
    vec = lambda v: v.reshape(1, -1).astype(F32)
    n_odd = od_moe_w_gate.shape[0]
    by_col_tile = lambda w: w.astype(BF16).reshape(n_odd, N_EXPERTS, d, MOE_FF_STEPS, MOE_TF).transpose(0, 1, 3, 2, 4)
    moe_wg, moe_wu = by_col_tile(od_moe_w_gate), by_col_tile(od_moe_w_up)
    moe_wd = od_moe_w_down.astype(BF16).reshape(n_odd, N_EXPERTS, MOE_FF_STEPS, MOE_TF, d)

    for layer in range(depth):
        i = layer // 2
        if layer % 2 == 0:
            win, wq, wkv = _even_weights(ev_w_in[i], ev_w_q_up[i], ev_w_kv_up[i])
            a, q, k, v = _even_in(xt, vec(ev_norm_mix[i]), win, vec(ev_q_norm[i]), wq, vec(ev_kv_norm[i]), wkv,
                                  _v_ones(), tabs, seq)
            att = _mla_attention(q, k, v, batch, seq)
            xt = _even_out(xt, a, att, ev_pool_w[i].astype(BF16), vec(ev_pool_scale[i]), ev_w_out[i].astype(BF16),
                           vec(ev_norm_ffn[i]), ev_ffn_w_gate[i].astype(BF16), ev_ffn_w_up[i].astype(BF16),
                           ev_ffn_w_down[i].astype(BF16), seq)
        else:
            w_in = od_w_in[i]
            win = jnp.concatenate([w_in[:, :CA_WIDTH] * (CA_HEAD_DIM ** -0.5), w_in[:, CA_WIDTH:]], axis=1).astype(BF16)
            sw = jnp.where(causal[None], od_sgu_w[i], 0.0).astype(BF16)
            sb = jnp.broadcast_to(od_sgu_b[i][:, :, None], (SGU_GROUPS, SGU_CHUNK, SGU_CH)).astype(F32)
            qkv, sg = _odd_in(xt, vec(od_norm_mix[i]), win, vec(od_sgu_norm_g[i]), vec(od_sgu_norm_b[i]), sw, sb)
            att = _band_attention(qkv, _band_bias(od_rel_bias[i]), batch, seq)
            wr = jnp.concatenate([od_router[i], jnp.zeros((d, ROUTE_LANES - N_EXPERTS), F32)], axis=1)
            wr_hi = wr.astype(BF16)
            wr = jnp.stack([wr_hi, (wr - wr_hi.astype(F32)).astype(BF16)])
            x1, h, route, counts = _odd_out(xt, att, sg, od_w_out[i].astype(BF16), vec(od_norm_ffn[i]), wr, tri)
            tile_e, n_used, src, dst = _moe_plan(route, counts, tokens)
            o2 = _moe(h, tile_e, n_used, src, dst, moe_wg, moe_wu, moe_wd, i)
            last = layer == depth - 1
            xt = _combine(x1, o2, route, vec(final_norm), last)
    return xt.reshape(batch, seq, d)
```

```python
import functools
import math

import jax
import jax.numpy as jnp
from jax import lax
from jax.experimental import pallas as pl
from jax.experimental.pallas import tpu as pltpu

F32 = jnp.float32
BF16 = jnp.bfloat16
I32 = jnp.int32

D_MODEL = 1024
CHUNK = 64
EPS = 1e-6

POOL_GROUPS = 4
POOL_CH = 128
POOL_WIDTH = 512
POOL_WINDOWS = (2, 4, 8, 16)
POOL_HALO = 16

MLA_HEADS = 8
MLA_NOPE = 64
MLA_ROPE = 32
MLA_V = 64
MLA_Q_RANK = 256
MLA_KV_RANK = 128
ROPE_THETA = 10000.0
HEAD_PAD = 128

CA_HEADS = 8
CA_HEAD_DIM = 64
CA_LEFT_CHUNKS = 8
REL_CLIP = 128
CA_WIDTH = CA_HEADS * CA_HEAD_DIM

SGU_GROUPS = 4
SGU_CH = 128
SGU_WIDTH = 512
SGU_CHUNK = 128

D_FF = 2816
N_EXPERTS = 8
EXPERT_FF = 3584

NEG = -1e30

TM = 512
TQ = 256
TK = 512
MLA_TQ = 512
MOE_TF = 512
MOE_FF_STEPS = EXPERT_FF // MOE_TF
MOE_TM = 80 * MOE_FF_STEPS
VMEM_LIMIT = 56 * 1024 * 1024


def _cparams(sem):
    return pltpu.CompilerParams(dimension_semantics=sem, vmem_limit_bytes=VMEM_LIMIT)


def _rms(x, g):
    return x * lax.rsqrt(jnp.mean(x * x, axis=-1, keepdims=True) + EPS) * g


def _dot(a, b):
    return jnp.dot(a, b, preferred_element_type=F32)


def _dot_nt(a, b):
    return lax.dot_general(a, b, (((1,), (1,)), ((), ())), preferred_element_type=F32)


TOKEN_TILE_ROWS = D_MODEL // 128


def _store_token_tiles(ref, x):
    n = x.shape[0]
    for s in range(TOKEN_TILE_ROWS):
        ref[pl.ds(s, n, stride=TOKEN_TILE_ROWS), :] = x[:, s * 128:(s + 1) * 128]


def _load_token_tiles(ref, n):
    return jnp.concatenate([ref[pl.ds(s, n, stride=TOKEN_TILE_ROWS), :] for s in range(TOKEN_TILE_ROWS)], axis=1)


def _const_spec(shape):
    nd = len(shape)
    return pl.BlockSpec(shape, lambda *_: (0,) * nd, pipeline_mode=pl.Buffered(1))


def _even_in_body(x_ref, g_ref, win_ref, qn_ref, wq_ref, kvn_ref, wkv_ref, vones_ref, cq_ref, sq_ref, ck_ref, sk_ref,
                  a_ref, q_ref, k_ref, v_ref):
    h = _rms(x_ref[...], g_ref[...]).astype(BF16)
    z = _dot(h, win_ref[...])
    a_ref[...] = z[:, :POOL_WIDTH]
    o = POOL_WIDTH
    cq = z[:, o:o + MLA_Q_RANK]
    o += MLA_Q_RANK
    ckv = z[:, o:o + MLA_KV_RANK]
    o += MLA_KV_RANK
    kr = z[:, o:o + HEAD_PAD]
    kr_rot = z[:, o + HEAD_PAD:o + 2 * HEAD_PAD]

    hw = MLA_HEADS * HEAD_PAD
    qq = _dot(_rms(cq, qn_ref[...]).astype(BF16), wq_ref[...])
    cq_t, sq_t = cq_ref[...], sq_ref[...]
    for h_i in range(MLA_HEADS):
        sl = slice(h_i * HEAD_PAD, (h_i + 1) * HEAD_PAD)
        sl_rot = slice(hw + h_i * HEAD_PAD, hw + (h_i + 1) * HEAD_PAD)
        q_ref[:, sl] = (qq[:, sl] * cq_t + qq[:, sl_rot] * sq_t).astype(BF16)

    kv = _dot(_rms(ckv, kvn_ref[...]).astype(BF16), wkv_ref[...])
    k_rope = kr * ck_ref[...] + kr_rot * sk_ref[...]
    for h_i in range(MLA_HEADS):
        sl = slice(h_i * HEAD_PAD, (h_i + 1) * HEAD_PAD)
        k_ref[:, sl] = (kv[:, sl] + k_rope).astype(BF16)
    v_ref[...] = (kv[:, hw:] + vones_ref[...]).astype(BF16)


def _even_in(x, g, win, qn, wq, kvn, wkv, vones, tabs, seq):
    t = x.shape[0]
    hw = MLA_HEADS * HEAD_PAD
    row = lambda w: pl.BlockSpec((TM, w), lambda i: (i, 0))
    tab = pl.BlockSpec((TM, HEAD_PAD), lambda i: (i % (seq // TM), 0))
    return pl.pallas_call(
        _even_in_body,
        grid=(t // TM,),
        in_specs=[row(D_MODEL), _const_spec(g.shape), _const_spec(win.shape), _const_spec(qn.shape),
                  _const_spec(wq.shape), _const_spec(kvn.shape), _const_spec(wkv.shape), _const_spec(vones.shape),
                  tab, tab, tab, tab],
        out_specs=[row(POOL_WIDTH), row(hw), row(hw), row(hw)],
        out_shape=[jax.ShapeDtypeStruct((t, POOL_WIDTH), F32), jax.ShapeDtypeStruct((t, hw), BF16),
                   jax.ShapeDtypeStruct((t, hw), BF16), jax.ShapeDtypeStruct((t, hw), BF16)],
        compiler_params=_cparams(("parallel",)),
        name="even_in",
    )(x, g, win, qn, wq, kvn, wkv, vones, *tabs)


def _mla_body(q_ref, k_ref, v_ref, o_ref, s_ref, mx_ref, acc_ref):
    qi = pl.program_id(2)
    last = (qi * MLA_TQ) // TK
    halves = (slice(0, HEAD_PAD), slice(HEAD_PAD, 2 * HEAD_PAD))
    lane_blocks = [slice(c * 128, (c + 1) * 128) for c in range(TK // 128)]
    q_chunk = lax.broadcasted_iota(I32, (MLA_TQ, TK), 0) // CHUNK + qi * (MLA_TQ // CHUNK)
    k_chunk = lax.broadcasted_iota(I32, (MLA_TQ, TK), 1) // CHUNK + last * (TK // CHUNK)
    visible = k_chunk <= q_chunk

    mx_ref[...] = jnp.full(mx_ref.shape, NEG, F32)
    acc_ref[...] = jnp.zeros_like(acc_ref)

    def scores(j, masked):
        rows = pl.ds(pl.multiple_of(j * TK, TK), TK)
        for hh, hs in enumerate(halves):
            s = _dot_nt(q_ref[:, hs], k_ref[rows, hs])
            if masked:
                s = jnp.where(visible, s, NEG)
            s_ref[hh, j] = s
            mx_ref[hh] = functools.reduce(jnp.maximum, [s[:, lb] for lb in lane_blocks], mx_ref[hh])

    def pairs_then_rest(count, body):
        def two(jj, c):
            body(2 * jj)
            body(2 * jj + 1)
            return c

        lax.fori_loop(0, count // 2, two, 0)

        @pl.when(count % 2 == 1)
        def _():
            body(count - 1)

    pairs_then_rest(last, lambda j: scores(j, False))
    scores(last, True)

    for hh in range(2):
        m = jnp.max(mx_ref[hh], axis=-1, keepdims=True)
        mx_ref[hh] = jnp.broadcast_to(m, (MLA_TQ, 128))

    def weighted(j):
        rows = pl.ds(pl.multiple_of(j * TK, TK), TK)
        for hh, hs in enumerate(halves):
            m = mx_ref[hh]
            p = jnp.concatenate([jnp.exp2(s_ref[hh, j, :, lb] - m) for lb in lane_blocks], axis=1)
            acc_ref[hh] += _dot(p.astype(BF16), v_ref[rows, hs])

    pairs_then_rest(last + 1, weighted)

    even, odd = acc_ref[0], acc_ref[1]
    lane = lax.broadcasted_iota(I32, (MLA_TQ, 2 * MLA_V), 1)
    out = jnp.where(lane < MLA_V, even / even[:, MLA_V:MLA_V + 1], odd / odd[:, 0:1])
    o_ref[...] = out.astype(BF16)


def _mla_attention(q, k, v, batch, seq):
    t = q.shape[0]
    nq = seq // MLA_TQ
    pairs = MLA_HEADS // 2
    return pl.pallas_call(
        _mla_body,
        grid=(batch, pairs, nq),
        in_specs=[pl.BlockSpec((MLA_TQ, 2 * HEAD_PAD), lambda b, p, i: (b * nq + i, p)),
                  pl.BlockSpec((seq, 2 * HEAD_PAD), lambda b, p, i: (b, p)),
                  pl.BlockSpec((seq, 2 * HEAD_PAD), lambda b, p, i: (b, p))],
        out_specs=pl.BlockSpec((MLA_TQ, 2 * MLA_V), lambda b, p, i: (b * nq + i, p)),
        out_shape=jax.ShapeDtypeStruct((t, MLA_HEADS * MLA_V), BF16),
        scratch_shapes=[pltpu.VMEM((2, seq // TK, MLA_TQ, TK), F32), pltpu.VMEM((2, MLA_TQ, 128), F32),
                        pltpu.VMEM((2, MLA_TQ, HEAD_PAD), F32)],
        compiler_params=_cparams(("parallel", "parallel", "arbitrary")),
        name="mla_attention",
    )(q, k, v)


def _even_out_body(x_ref, a_ref, halo_ref, att_ref, pw_ref, ps_ref, wo_ref, g_ref, wg_ref, wu_ref, wd_ref,
                   o_ref, ext_ref, *, seq):
    pos0 = (pl.program_id(0) * TM) % seq
    ext_ref[0:POOL_HALO, :] = jnp.where(pos0 == 0, 0.0, halo_ref[...])
    ext_ref[POOL_HALO:, :] = a_ref[...]
    pos = lax.broadcasted_iota(I32, (TM, 1), 0) + pos0
    y = x_ref[...] + _dot(att_ref[...], wo_ref[POOL_WIDTH:, :])
    for g_i, win in enumerate(POOL_WINDOWS):
        gs = slice(g_i * POOL_CH, (g_i + 1) * POOL_CH)
        a_g = ext_ref[POOL_HALO:, gs]
        ssum = a_g
        for j in range(1, win):
            ssum = ssum + ext_ref[POOL_HALO - j:POOL_HALO - j + TM, gs]
        count = jnp.minimum(pos + 1, win).astype(F32)
        pooled = (ssum / count - a_g).astype(BF16)
        mixed = (_dot(pooled, pw_ref[g_i]) * ps_ref[:, gs]).astype(BF16)
        y = y + _dot(mixed, wo_ref[gs, :])
    h = _rms(y, g_ref[...]).astype(BF16)
    gate = _dot(h, wg_ref[...])
    up = _dot(h, wu_ref[...])
    act = (gate * jax.nn.sigmoid(gate) * up).astype(BF16)
    o_ref[...] = y + _dot(act, wd_ref[...])


def _even_out(x, a, att, pw, ps, wo, g, wg, wu, wd, seq):
    t = x.shape[0]
    row = lambda w: pl.BlockSpec((TM, w), lambda i: (i, 0))
    halo = pl.BlockSpec((POOL_HALO, POOL_WIDTH), lambda i: (jnp.maximum(i * (TM // POOL_HALO) - 1, 0), 0))
    return pl.pallas_call(
        functools.partial(_even_out_body, seq=seq),
        grid=(t // TM,),
        in_specs=[row(D_MODEL), row(POOL_WIDTH), halo, row(MLA_HEADS * MLA_V), _const_spec(pw.shape),
                  _const_spec(ps.shape), _const_spec(wo.shape), _const_spec(g.shape), _const_spec(wg.shape),
                  _const_spec(wu.shape), _const_spec(wd.shape)],
        out_specs=row(D_MODEL),
        out_shape=jax.ShapeDtypeStruct((t, D_MODEL), F32),
        scratch_shapes=[pltpu.VMEM((TM + POOL_HALO, POOL_WIDTH), F32)],
        compiler_params=_cparams(("parallel",)),
        name="even_out",
    )(x, a, a, att, pw, ps, wo, g, wg, wu, wd)


def _odd_in_body(x_ref, g_ref, win_ref, ng_ref, nb_ref, sw_ref, sb_ref, qkv_ref, sg_ref):
    h = _rms(x_ref[...], g_ref[...]).astype(BF16)
    z = _dot(h, win_ref[...])
    qkv_ref[:, :CA_WIDTH] = (z[:, :CA_WIDTH] * math.log2(math.e)).astype(BF16)
    qkv_ref[:, CA_WIDTH:] = z[:, CA_WIDTH:3 * CA_WIDTH].astype(BF16)
    uv = jax.nn.gelu(z[:, 3 * CA_WIDTH:])
    for g_i in range(SGU_GROUPS):
        gs = slice(g_i * SGU_CH, (g_i + 1) * SGU_CH)
        u = uv[:, gs]
        gv = uv[:, SGU_WIDTH + g_i * SGU_CH:SGU_WIDTH + (g_i + 1) * SGU_CH]
        mu = jnp.mean(gv, axis=-1, keepdims=True)
        xc = gv - mu
        vn = xc * lax.rsqrt(jnp.mean(xc * xc, axis=-1, keepdims=True) + EPS) * ng_ref[:, gs] + nb_ref[:, gs]
        vn = vn.astype(BF16)
        w = sw_ref[g_i]
        for c_i in range(TM // SGU_CHUNK):
            rs = slice(c_i * SGU_CHUNK, (c_i + 1) * SGU_CHUNK)
            mixed = _dot(w, vn[rs, :]) + sb_ref[g_i]
            sg_ref[rs, gs] = (u[rs, :] * mixed).astype(BF16)


def _odd_in(x, g, win, ng, nb, sw, sb):
    t = x.shape[0]
    row = lambda w: pl.BlockSpec((TM, w), lambda i: (i, 0))
    return pl.pallas_call(
        _odd_in_body,
        grid=(t // TM,),
        in_specs=[row(D_MODEL), _const_spec(g.shape), _const_spec(win.shape), _const_spec(ng.shape),
                  _const_spec(nb.shape), _const_spec(sw.shape), _const_spec(sb.shape)],
        out_specs=[row(3 * CA_WIDTH), row(SGU_WIDTH)],
        out_shape=[jax.ShapeDtypeStruct((t, 3 * CA_WIDTH), BF16), jax.ShapeDtypeStruct((t, SGU_WIDTH), BF16)],
        compiler_params=_cparams(("parallel",)),
        name="odd_in",
    )(x, g, win, ng, nb, sw, sb)


BAND_BLOCKS = 3


def _band_body(q_ref, k0_ref, k1_ref, k2_ref, v0_ref, v1_ref, v2_ref, bias_ref, o_ref):
    pair_w = 2 * CA_HEAD_DIM
    low = lax.broadcasted_iota(I32, (TQ, pair_w), 1) < CA_HEAD_DIM
    k_refs = (k0_ref, k1_ref, k2_ref)
    v_refs = (v0_ref, v1_ref, v2_ref)
    add = lambda a, b: a + b
    per_tile = TQ // 128
    for pair in range(CA_HEADS // 2):
        ps = slice(pair * pair_w, (pair + 1) * pair_w)
        q_pair = q_ref[:, ps]
        outs = []
        for hh in range(2):
            q = jnp.where(low if hh == 0 else jnp.logical_not(low), q_pair, jnp.zeros_like(q_pair))
            s = []
            for jj in range(BAND_BLOCKS):
                sj = _dot_nt(q, k_refs[jj][:, ps]) + bias_ref[2 * pair + hh, :, jj * TQ:(jj + 1) * TQ]
                s += [sj[:, c * 128:(c + 1) * 128] for c in range(per_tile)]
            m = jnp.max(functools.reduce(jnp.maximum, s), axis=-1, keepdims=True)
            mb = jnp.broadcast_to(m, (TQ, 128))
            p = [jnp.exp2(x - mb) for x in s]
            l = jnp.sum(functools.reduce(add, p), axis=-1, keepdims=True)
            acc = functools.reduce(add, [
                _dot(jnp.concatenate(p[jj * per_tile:(jj + 1) * per_tile], axis=1).astype(BF16), v_refs[jj][:, ps])
                for jj in range(BAND_BLOCKS)])
            outs.append(acc / l)
        o_ref[:, ps] = jnp.where(low, outs[0], outs[1]).astype(BF16)


def _band_attention(qkv, bias, batch, seq):
    t = qkv.shape[0]
    nq = seq // TQ

    def kv_spec(jj, col):
        return pl.BlockSpec((TQ, CA_WIDTH), lambda b, i: (b * nq + jnp.maximum(i - (BAND_BLOCKS - 1) + jj, 0), col))

    return pl.pallas_call(
        _band_body,
        grid=(batch, nq),
        in_specs=[pl.BlockSpec((TQ, CA_WIDTH), lambda b, i: (b * nq + i, 0))]
                 + [kv_spec(jj, 1) for jj in range(BAND_BLOCKS)]
                 + [kv_spec(jj, 2) for jj in range(BAND_BLOCKS)]
                 + [pl.BlockSpec((None, CA_HEADS, TQ, BAND_BLOCKS * TQ),
                                 lambda b, i: (jnp.minimum(i, BAND_BLOCKS - 1), 0, 0, 0))],
        out_specs=pl.BlockSpec((TQ, CA_WIDTH), lambda b, i: (b * nq + i, 0)),
        out_shape=jax.ShapeDtypeStruct((t, CA_WIDTH), BF16),
        compiler_params=_cparams(("parallel", "arbitrary")),
        name="band_attention",
    )(qkv, qkv, qkv, qkv, qkv, qkv, qkv, bias)


ROUTE_LANES = 128


def _odd_out_body(x_ref, att_ref, sg_ref, wo_ref, g_ref, wr_ref, tri_ref, x1_ref, h_ref, route_ref, cnt_ref, run_ref):
    @pl.when(pl.program_id(0) == 0)
    def _():
        run_ref[...] = jnp.zeros_like(run_ref)

    y = x_ref[...] + _dot(att_ref[...], wo_ref[:CA_WIDTH, :]) + _dot(sg_ref[...], wo_ref[CA_WIDTH:, :])
    x1_ref[...] = y
    h = _rms(y, g_ref[...])
    _store_token_tiles(h_ref, h)
    h_hi = h.astype(BF16)
    h_lo = (h - h_hi.astype(F32)).astype(BF16)
    logits = _dot(h_hi, wr_ref[0]) + (_dot(h_lo, wr_ref[0]) + _dot(h_hi, wr_ref[1]))
    lane = lax.broadcasted_iota(I32, (TM, ROUTE_LANES), 1)
    logits = jnp.where(lane < N_EXPERTS, logits, NEG)
    m1 = jnp.max(logits, axis=-1, keepdims=True)
    e1 = jnp.min(jnp.where(logits == m1, lane, ROUTE_LANES), axis=-1, keepdims=True)
    rest = jnp.where(lane == e1, NEG, logits)
    m2 = jnp.max(rest, axis=-1, keepdims=True)
    e2 = jnp.min(jnp.where(rest == m2, lane, ROUTE_LANES), axis=-1, keepdims=True)
    ex = jnp.exp(m2 - m1)
    w1 = 1.0 / (1.0 + ex)
    w2 = ex / (1.0 + ex)
    hot1 = (lane == e1).astype(F32)
    hot2 = (lane == e2).astype(F32)
    sel = hot1 + hot2
    before = _dot(tri_ref[...], sel.astype(BF16)) + run_ref[0:1, :]
    rank1 = jnp.sum(hot1 * before, axis=-1, keepdims=True)
    rank2 = jnp.sum(hot2 * before, axis=-1, keepdims=True)
    rec = jnp.where(lane == 0, e1.astype(F32), 0.0)
    rec = jnp.where(lane == 1, e2.astype(F32), rec)
    rec = jnp.where(lane == 2, rank1, rec)
    rec = jnp.where(lane == 3, rank2, rec)
    rec = jnp.where(lane == 4, w1, rec)
    rec = jnp.where(lane == 5, w2, rec)
    route_ref[...] = rec
    total = run_ref[0:1, :] + jnp.sum(sel, axis=0, keepdims=True)
    run_ref[...] = jnp.broadcast_to(total, run_ref.shape)
    cnt_ref[...] = jnp.broadcast_to(total, cnt_ref.shape)


def _odd_out(x, att, sg, wo, g, wr, tri):
    t = x.shape[0]
    row = lambda w: pl.BlockSpec((TM, w), lambda i: (i, 0))
    return pl.pallas_call(
        _odd_out_body,
        grid=(t // TM,),
        in_specs=[row(D_MODEL), row(CA_WIDTH), row(SGU_WIDTH), _const_spec(wo.shape), _const_spec(g.shape),
                  _const_spec(wr.shape), _const_spec(tri.shape)],
        out_specs=[row(D_MODEL), pl.BlockSpec((TM * TOKEN_TILE_ROWS, 128), lambda i: (i, 0)), row(ROUTE_LANES),
                   pl.BlockSpec((8, ROUTE_LANES), lambda i: (0, 0))],
        out_shape=[jax.ShapeDtypeStruct((t, D_MODEL), F32), jax.ShapeDtypeStruct((t * TOKEN_TILE_ROWS, 128), F32),
                   jax.ShapeDtypeStruct((t, ROUTE_LANES), F32), jax.ShapeDtypeStruct((8, ROUTE_LANES), F32)],
        scratch_shapes=[pltpu.VMEM((8, ROUTE_LANES), F32)],
        compiler_params=_cparams(("arbitrary",)),
        name="odd_out_router",
    )(x, att, sg, wo, g, wr, tri)


def _moe_body(tile_e_ref, n_used_ref, src_ref, dst_ref, h_hbm, wg_ref, wu_ref, wd_ref, o_hbm,
              xf_ref, xb_ref, acc_ref, y_ref, gsem, ssem):
    i = pl.program_id(0)
    j = pl.program_id(1)
    n_used = n_used_ref[0]
    rows_per_step = MOE_TM // MOE_FF_STEPS

    ttr = TOKEN_TILE_ROWS
    tile_rows = MOE_TM * ttr

    def token_rows(ref, first_row):
        return ref.at[pl.ds(pl.multiple_of(first_row, ttr), ttr), :]

    def gather_copy(tile, r, slot):
        return pltpu.make_async_copy(token_rows(h_hbm, src_ref[tile * MOE_TM + r]),
                                     token_rows(xf_ref.at[slot], r * ttr), gsem)

    def scatter_copy(tile, r, slot):
        return pltpu.make_async_copy(token_rows(y_ref.at[slot], r * ttr),
                                     token_rows(o_hbm, dst_ref[tile * MOE_TM + r]), ssem)

    def wait_tile_gather(slot):
        pltpu.make_async_copy(h_hbm.at[pl.ds(0, tile_rows), :], xf_ref.at[slot], gsem).wait()

    def wait_tile_scatter(slot):
        pltpu.make_async_copy(y_ref.at[slot], o_hbm.at[pl.ds(0, tile_rows), :], ssem).wait()

    def start_all(copy_fn):
        lax.fori_loop(0, MOE_TM, lambda r, c: (copy_fn(r).start(), c)[1], 0)

    @pl.when(i < n_used)
    def _():
        cur = lax.rem(i, 2)
        other = 1 - cur
        nxt_tile = jnp.minimum(i + 1, n_used - 1)
        prev_tile = jnp.maximum(i - 1, 0)

        @pl.when(jnp.logical_and(i == 0, j == 0))
        def _():
            y_ref[...] = jnp.zeros_like(y_ref)
            spare0 = o_hbm.shape[0] - N_EXPERTS * tile_rows
            for e_i in range(N_EXPERTS):
                fill = pltpu.make_async_copy(y_ref.at[0], o_hbm.at[pl.ds(spare0 + e_i * tile_rows, tile_rows), :],
                                             ssem)
                fill.start()
                fill.wait()
            start_all(lambda r: gather_copy(0, r, 0))

        @pl.when(j == 0)
        def _():
            wait_tile_gather(cur)
            xb_ref[...] = _load_token_tiles(xf_ref.at[cur], MOE_TM).astype(BF16)
            acc_ref[...] = jnp.zeros_like(acc_ref)

        base = pl.multiple_of(j * rows_per_step, 8)
        for rr in range(rows_per_step):
            gather_copy(nxt_tile, base + rr, other).start()
            scatter_copy(prev_tile, base + rr, other).start()
        xb = xb_ref[...]
        gate = _dot(xb, wg_ref[...])
        up = _dot(xb, wu_ref[...])
        act = (gate * jax.nn.sigmoid(gate) * up).astype(BF16)
        acc_ref[...] += _dot(act, wd_ref[...])

        @pl.when(j == MOE_FF_STEPS - 1)
        def _():
            wait_tile_scatter(other)
            _store_token_tiles(y_ref.at[cur], acc_ref[...])

        @pl.when(jnp.logical_and(i == n_used - 1, j == MOE_FF_STEPS - 1))
        def _():
            start_all(lambda r: scatter_copy(i, r, cur))
            wait_tile_scatter(cur)
            wait_tile_gather(other)


def _moe(h, tile_e, n_used, src, dst, wg, wu, wd, layer):
    tokens = h.shape[0] // TOKEN_TILE_ROWS
    n_tiles = tile_e.shape[0]
    last = MOE_FF_STEPS - 1

    def ff_idx(i, j, n_used_ref):
        return jnp.where(i < n_used_ref[0], j, last)

    grid_spec = pltpu.PrefetchScalarGridSpec(
        num_scalar_prefetch=4,
        grid=(n_tiles, MOE_FF_STEPS),
        in_specs=[pl.BlockSpec(memory_space=pl.ANY),
                  pl.BlockSpec((None, None, D_MODEL, MOE_TF),
                               lambda i, j, te, nu, sr, ds: (layer, te[i], 0, ff_idx(i, j, nu))),
                  pl.BlockSpec((None, None, D_MODEL, MOE_TF),
                               lambda i, j, te, nu, sr, ds: (layer, te[i], 0, ff_idx(i, j, nu))),
                  pl.BlockSpec((None, None, MOE_TF, D_MODEL),
                               lambda i, j, te, nu, sr, ds: (layer, te[i], ff_idx(i, j, nu), 0))],
        out_specs=pl.BlockSpec(memory_space=pl.ANY),
        scratch_shapes=[pltpu.VMEM((2, MOE_TM * TOKEN_TILE_ROWS, 128), F32), pltpu.VMEM((MOE_TM, D_MODEL), BF16),
                        pltpu.VMEM((MOE_TM, D_MODEL), F32), pltpu.VMEM((2, MOE_TM * TOKEN_TILE_ROWS, 128), F32),
                        pltpu.SemaphoreType.DMA, pltpu.SemaphoreType.DMA],
    )
    return pl.pallas_call(
        _moe_body,
        grid_spec=grid_spec,
        out_shape=jax.ShapeDtypeStruct(((2 * tokens + N_EXPERTS * MOE_TM) * TOKEN_TILE_ROWS, 128), F32),
        compiler_params=_cparams(("arbitrary", "arbitrary")),
        name="moe_experts",
    )(tile_e, n_used, src, dst, h, wg, wu, wd)


def _moe_plan(route, counts, tokens):
    n_tiles = (2 * tokens + N_EXPERTS * (MOE_TM - 1)) // MOE_TM
    experts = jnp.arange(N_EXPERTS, dtype=I32)
    cnt = counts[0, :N_EXPERTS].astype(I32)
    e = route[:, 0:2].astype(I32)
    rank = route[:, 2:4].astype(I32)
    tiles_per_e = (cnt + MOE_TM - 1) // MOE_TM
    tile_end = jnp.cumsum(tiles_per_e)
    tile_start = tile_end - tiles_per_e
    row_start = jnp.sum(jnp.where(e[..., None] == experts, tile_start * MOE_TM, 0), axis=-1)
    pos = row_start + rank
    slot_row = jnp.arange(tokens, dtype=I32)[:, None] + jnp.array([0, tokens], I32)[None, :]
    n_used = tile_end[-1]
    tile = jnp.minimum(jnp.arange(n_tiles, dtype=I32), n_used - 1)
    tile_e = jnp.sum((tile_end[None, :] <= tile[:, None]).astype(I32), axis=1)
    spare = 2 * tokens + tile_e[:, None] * MOE_TM + jnp.arange(MOE_TM, dtype=I32)[None, :]
    dst = spare.reshape(-1).at[pos.reshape(-1)].set(slot_row.reshape(-1))
    src = jnp.where(dst < 2 * tokens, dst % tokens, 0)
    return tile_e, n_used.reshape(1), src * TOKEN_TILE_ROWS, dst * TOKEN_TILE_ROWS


def _combine_body(x_ref, oa_ref, ob_ref, route_ref, g_ref, o_ref, *, final):
    w1 = route_ref[:, 4:5]
    w2 = route_ref[:, 5:6]
    y = x_ref[...] + (w1 * _load_token_tiles(oa_ref, TM) + w2 * _load_token_tiles(ob_ref, TM))
    o_ref[...] = _rms(y, g_ref[...]) if final else y


def _combine(x, o2, route, g, final):
    t = x.shape[0]
    nt = t // TM
    return pl.pallas_call(
        functools.partial(_combine_body, final=final),
        grid=(nt,),
        in_specs=[pl.BlockSpec((TM, D_MODEL), lambda i: (i, 0)),
                  pl.BlockSpec((TM * TOKEN_TILE_ROWS, 128), lambda i: (i, 0)),
                  pl.BlockSpec((TM * TOKEN_TILE_ROWS, 128), lambda i: (i + nt, 0)),
                  pl.BlockSpec((TM, ROUTE_LANES), lambda i: (i, 0)), _const_spec(g.shape)],
        out_specs=pl.BlockSpec((TM, D_MODEL), lambda i: (i, 0)),
        out_shape=jax.ShapeDtypeStruct((t, D_MODEL), F32),
        compiler_params=_cparams(("parallel",)),
        name="moe_combine",
    )(x, o2, o2, route, g)


def _rot_half(w):
    half = w.shape[-1] // 2
    return jnp.concatenate([-w[..., half:], w[..., :half]], axis=-1)


def _rope_tables(seq):
    inv = 1.0 / (ROPE_THETA ** (jnp.arange(0, MLA_ROPE, 2, dtype=F32) / MLA_ROPE))
    ang = jnp.arange(seq, dtype=F32)[:, None] * inv[None, :]
    cos, sin = jnp.cos(ang), jnp.sin(ang)
    pad = HEAD_PAD - MLA_NOPE - MLA_ROPE
    c = jnp.concatenate([jnp.ones((seq, MLA_NOPE), F32), cos, cos, jnp.zeros((seq, pad), F32)], axis=1)
    s = jnp.concatenate([jnp.zeros((seq, MLA_NOPE), F32), sin, sin, jnp.zeros((seq, pad), F32)], axis=1)
    scale = (MLA_NOPE + MLA_ROPE) ** -0.5 * math.log2(math.e)
    return c * scale, s * scale, c, s


def _even_weights(w_in, w_q_up, w_kv_up):
    d = w_in.shape[0]
    pad = HEAD_PAD - MLA_NOPE - MLA_ROPE
    kr = w_in[:, POOL_WIDTH + MLA_Q_RANK + MLA_KV_RANK:]
    place = lambda m: jnp.concatenate([jnp.zeros((d, MLA_NOPE), F32), m, jnp.zeros((d, pad), F32)], axis=1)
    win = jnp.concatenate([w_in[:, :POOL_WIDTH + MLA_Q_RANK + MLA_KV_RANK], place(kr), place(_rot_half(kr))], axis=1)
    wq = w_q_up.reshape(MLA_Q_RANK, MLA_HEADS, MLA_NOPE + MLA_ROPE)
    nope, rope = wq[..., :MLA_NOPE], wq[..., MLA_NOPE:]
    zq = lambda n: jnp.zeros((MLA_Q_RANK, MLA_HEADS, n), F32)
    wq_main = jnp.concatenate([nope, rope, zq(pad)], axis=-1).reshape(MLA_Q_RANK, -1)
    wq_rot = jnp.concatenate([zq(MLA_NOPE), _rot_half(rope), zq(pad)], axis=-1).reshape(MLA_Q_RANK, -1)
    wkv = w_kv_up.reshape(MLA_KV_RANK, MLA_HEADS, MLA_NOPE + MLA_V)
    k_pad = jnp.concatenate([wkv[..., :MLA_NOPE], jnp.zeros((MLA_KV_RANK, MLA_HEADS, HEAD_PAD - MLA_NOPE), F32)],
                            axis=-1).reshape(MLA_KV_RANK, -1)
    v_pairs = wkv[..., MLA_NOPE:].reshape(MLA_KV_RANK, MLA_HEADS // 2, 2, MLA_V)
    v_gap = jnp.zeros((MLA_KV_RANK, MLA_HEADS // 2, 2 * (HEAD_PAD - MLA_V)), F32)
    v_ext = jnp.concatenate([v_pairs[:, :, 0], v_gap, v_pairs[:, :, 1]], axis=-1).reshape(MLA_KV_RANK, -1)
    return (win.astype(BF16), jnp.concatenate([wq_main, wq_rot], axis=1).astype(BF16),
            jnp.concatenate([k_pad, v_ext], axis=1).astype(BF16))


def _v_ones():
    lane = jnp.arange(MLA_HEADS * HEAD_PAD) % (2 * HEAD_PAD)
    return jnp.logical_and(lane >= MLA_V, lane < 2 * HEAD_PAD - MLA_V).astype(F32).reshape(1, -1)


def _band_bias(rel_bias):
    band = BAND_BLOCKS * TQ
    qpos = jnp.arange(TQ)[:, None] + (BAND_BLOCKS - 1) * TQ
    kpos = jnp.arange(band)[None, :]
    span = band + TQ - 1
    rel = (BAND_BLOCKS - 1) * TQ + (TQ - 1) - jnp.arange(span)
    by_rel = rel_bias[:, jnp.clip(rel, -REL_CLIP, REL_CLIP) + REL_CLIP].astype(F32)
    rolled = jnp.roll(by_rel, -(TQ - 1), axis=1)
    bias = jnp.tile(rolled, (1, TQ))[:, :TQ * (span - 1)].reshape(-1, TQ, span - 1)[:, :, :band]
    dc = qpos // CHUNK - kpos // CHUNK
    valid = jnp.logical_and(dc >= 0, dc <= CA_LEFT_CHUNKS)
    started = kpos // TQ >= (BAND_BLOCKS - 1) - jnp.arange(BAND_BLOCKS)[:, None, None]
    ok = jnp.logical_and(valid[None], started)[:, None]
    return jnp.where(ok, bias[None] * math.log2(math.e), NEG)


def kernel(x, ev_norm_mix, ev_w_in, ev_pool_w, ev_pool_scale, ev_q_norm, ev_w_q_up, ev_kv_norm, ev_w_kv_up, ev_w_out, ev_norm_ffn, ev_ffn_w_gate, ev_ffn_w_up, ev_ffn_w_down, od_norm_mix, od_w_in, od_rel_bias, od_sgu_norm_g, od_sgu_norm_b, od_sgu_w, od_sgu_b, od_w_out, od_norm_ffn, od_router, od_moe_w_gate, od_moe_w_up, od_moe_w_down, final_norm):
    batch, seq, d = x.shape
    tokens = batch * seq
    assert d == D_MODEL and seq % TM == 0 and TM % SGU_CHUNK == 0 and TM % TQ == 0 and seq % TK == 0
    depth = ev_w_in.shape[0] + od_w_in.shape[0]
    xt = x.reshape(tokens, d)
    tabs = _rope_tables(seq)
    tri = jnp.tril(jnp.ones((TM, TM), F32), k=-1).astype(BF16)
    causal = jnp.tril(jnp.ones((SGU_CHUNK, SGU_CHUNK), bool))
    vec = lambda v: v.reshape(1, -1).astype(F32)
    moe_wg, moe_wu, moe_wd = (w.astype(BF16) for w in (od_moe_w_gate, od_moe_w_up, od_moe_w_down))

    for layer in range(depth):
        i = layer // 2
        if layer % 2 == 0:
            win, wq, wkv = _even_weights(ev_w_in[i], ev_w_q_up[i], ev_w_kv_up[i])
            a, q, k, v = _even_in(xt, vec(ev_norm_mix[i]), win, vec(ev_q_norm[i]), wq, vec(ev_kv_norm[i]), wkv,
                                  _v_ones(), tabs, seq)
            att = _mla_attention(q, k, v, batch, seq)
            xt = _even_out(xt, a, att, ev_pool_w[i].astype(BF16), vec(ev_pool_scale[i]), ev_w_out[i].astype(BF16),
                           vec(ev_norm_ffn[i]), ev_ffn_w_gate[i].astype(BF16), ev_ffn_w_up[i].astype(BF16),
                           ev_ffn_w_down[i].astype(BF16), seq)
        else:
            w_in = od_w_in[i]
            win = jnp.concatenate([w_in[:, :CA_WIDTH] * (CA_HEAD_DIM ** -0.5), w_in[:, CA_WIDTH:]], axis=1).astype(BF16)
            sw = jnp.where(causal[None], od_sgu_w[i], 0.0).astype(BF16)
            sb = jnp.broadcast_to(od_sgu_b[i][:, :, None], (SGU_GROUPS, SGU_CHUNK, SGU_CH)).astype(F32)
            qkv, sg = _odd_in(xt, vec(od_norm_mix[i]), win, vec(od_sgu_norm_g[i]), vec(od_sgu_norm_b[i]), sw, sb)
            att = _band_attention(qkv, _band_bias(od_rel_bias[i]), batch, seq)
            wr = jnp.concatenate([od_router[i], jnp.zeros((d, ROUTE_LANES - N_EXPERTS), F32)], axis=1)
            wr_hi = wr.astype(BF16)
            wr = jnp.stack([wr_hi, (wr - wr_hi.astype(F32)).astype(BF16)])
            x1, h, route, counts = _odd_out(xt, att, sg, od_w_out[i].astype(BF16), vec(od_norm_ffn[i]), wr, tri)
            tile_e, n_used, src, dst = _moe_plan(route, counts, tokens)
            o2 = _moe(h, tile_e, n_used, src, dst, moe_wg, moe_wu, moe_wd, i)
            last = layer == depth - 1
            xt = _combine(x1, o2, route, vec(final_norm), last)
    return xt.reshape(batch, seq, d)
```

```python
import functools
import math

import jax
import jax.numpy as jnp
from jax import lax
from jax.experimental import pallas as pl
from jax.experimental.pallas import tpu as pltpu

F32 = jnp.float32
BF16 = jnp.bfloat16
I32 = jnp.int32

D_MODEL = 1024
CHUNK = 64
EPS = 1e-6

POOL_GROUPS = 4
POOL_CH = 128
POOL_WIDTH = 512
POOL_WINDOWS = (2, 4, 8, 16)
POOL_HALO = 16

MLA_HEADS = 8
MLA_NOPE = 64
MLA_ROPE = 32
MLA_V = 64
MLA_Q_RANK = 256
MLA_KV_RANK = 128
ROPE_THETA = 10000.0
HEAD_PAD = 128

CA_HEADS = 8
CA_HEAD_DIM = 64
CA_LEFT_CHUNKS = 8
REL_CLIP = 128
CA_WIDTH = CA_HEADS * CA_HEAD_DIM

SGU_GROUPS = 4
SGU_CH = 128
SGU_WIDTH = 512
SGU_CHUNK = 128

D_FF = 2816
N_EXPERTS = 8
EXPERT_FF = 3584

NEG = -1e30

TM = 512
TQ = 256
TK = 512
MLA_TQ = 512
MOE_TF = 512
MOE_FF_STEPS = EXPERT_FF // MOE_TF
MOE_TM = 80 * MOE_FF_STEPS
VMEM_LIMIT = 56 * 1024 * 1024


def _cparams(sem):
    return pltpu.CompilerParams(dimension_semantics=sem, vmem_limit_bytes=VMEM_LIMIT)


def _rms(x, g):
    return x * lax.rsqrt(jnp.mean(x * x, axis=-1, keepdims=True) + EPS) * g


def _dot(a, b):
    return jnp.dot(a, b, preferred_element_type=F32)


def _dot_nt(a, b):
    return lax.dot_general(a, b, (((1,), (1,)), ((), ())), preferred_element_type=F32)


TOKEN_TILE_ROWS = D_MODEL // 128


def _store_token_tiles(ref, x):
    n = x.shape[0]
    for s in range(TOKEN_TILE_ROWS):
        ref[pl.ds(s, n, stride=TOKEN_TILE_ROWS), :] = x[:, s * 128:(s + 1) * 128]


def _load_token_tiles(ref, n):
    return jnp.concatenate([ref[pl.ds(s, n, stride=TOKEN_TILE_ROWS), :] for s in range(TOKEN_TILE_ROWS)], axis=1)


def _const_spec(shape):
    nd = len(shape)
    return pl.BlockSpec(shape, lambda *_: (0,) * nd, pipeline_mode=pl.Buffered(1))


def _even_in_body(x_ref, g_ref, win_ref, qn_ref, wq_ref, kvn_ref, wkv_ref, vones_ref, cq_ref, sq_ref, ck_ref, sk_ref,
                  a_ref, q_ref, k_ref, v_ref):
    h = _rms(x_ref[...], g_ref[...]).astype(BF16)
    z = _dot(h, win_ref[...])
    a_ref[...] = z[:, :POOL_WIDTH]
    o = POOL_WIDTH
    cq = z[:, o:o + MLA_Q_RANK]
    o += MLA_Q_RANK
    ckv = z[:, o:o + MLA_KV_RANK]
    o += MLA_KV_RANK
    kr = z[:, o:o + HEAD_PAD]
    kr_rot = z[:, o + HEAD_PAD:o + 2 * HEAD_PAD]

    hw = MLA_HEADS * HEAD_PAD
    qq = _dot(_rms(cq, qn_ref[...]).astype(BF16), wq_ref[...])
    cq_t, sq_t = cq_ref[...], sq_ref[...]
    for h_i in range(MLA_HEADS):
        sl = slice(h_i * HEAD_PAD, (h_i + 1) * HEAD_PAD)
        sl_rot = slice(hw + h_i * HEAD_PAD, hw + (h_i + 1) * HEAD_PAD)
        q_ref[:, sl] = (qq[:, sl] * cq_t + qq[:, sl_rot] * sq_t).astype(BF16)

    kv = _dot(_rms(ckv, kvn_ref[...]).astype(BF16), wkv_ref[...])
    k_rope = kr * ck_ref[...] + kr_rot * sk_ref[...]
    for h_i in range(MLA_HEADS):
        sl = slice(h_i * HEAD_PAD, (h_i + 1) * HEAD_PAD)
        k_ref[:, sl] = (kv[:, sl] + k_rope).astype(BF16)
    v_ref[...] = (kv[:, hw:] + vones_ref[...]).astype(BF16)


def _even_in(x, g, win, qn, wq, kvn, wkv, vones, tabs, seq):
    t = x.shape[0]
    hw = MLA_HEADS * HEAD_PAD
    row = lambda w: pl.BlockSpec((TM, w), lambda i: (i, 0))
    tab = pl.BlockSpec((TM, HEAD_PAD), lambda i: (i % (seq // TM), 0))
    return pl.pallas_call(
        _even_in_body,
        grid=(t // TM,),
        in_specs=[row(D_MODEL), _const_spec(g.shape), _const_spec(win.shape), _const_spec(qn.shape),
                  _const_spec(wq.shape), _const_spec(kvn.shape), _const_spec(wkv.shape), _const_spec(vones.shape),
                  tab, tab, tab, tab],
        out_specs=[row(POOL_WIDTH), row(hw), row(hw), row(hw)],
        out_shape=[jax.ShapeDtypeStruct((t, POOL_WIDTH), F32), jax.ShapeDtypeStruct((t, hw), BF16),
                   jax.ShapeDtypeStruct((t, hw), BF16), jax.ShapeDtypeStruct((t, hw), BF16)],
        compiler_params=_cparams(("parallel",)),
        name="even_in",
    )(x, g, win, qn, wq, kvn, wkv, vones, *tabs)


def _mla_body(q_ref, k_ref, v_ref, o_ref, s_ref, mx_ref, acc_ref):
    qi = pl.program_id(2)
    last = (qi * MLA_TQ) // TK
    halves = (slice(0, HEAD_PAD), slice(HEAD_PAD, 2 * HEAD_PAD))
    lane_blocks = [slice(c * 128, (c + 1) * 128) for c in range(TK // 128)]
    q_chunk = lax.broadcasted_iota(I32, (MLA_TQ, TK), 0) // CHUNK + qi * (MLA_TQ // CHUNK)
    k_chunk = lax.broadcasted_iota(I32, (MLA_TQ, TK), 1) // CHUNK + last * (TK // CHUNK)
    visible = k_chunk <= q_chunk

    mx_ref[...] = jnp.full(mx_ref.shape, NEG, F32)
    acc_ref[...] = jnp.zeros_like(acc_ref)

    def scores(j, masked):
        rows = pl.ds(pl.multiple_of(j * TK, TK), TK)
        for hh, hs in enumerate(halves):
            s = _dot_nt(q_ref[:, hs], k_ref[rows, hs])
            if masked:
                s = jnp.where(visible, s, NEG)
            s_ref[hh, j] = s
            mx_ref[hh] = functools.reduce(jnp.maximum, [s[:, lb] for lb in lane_blocks], mx_ref[hh])

    def pairs_then_rest(count, body):
        def two(jj, c):
            body(2 * jj)
            body(2 * jj + 1)
            return c

        lax.fori_loop(0, count // 2, two, 0)

        @pl.when(count % 2 == 1)
        def _():
            body(count - 1)

    pairs_then_rest(last, lambda j: scores(j, False))
    scores(last, True)

    for hh in range(2):
        m = jnp.max(mx_ref[hh], axis=-1, keepdims=True)
        mx_ref[hh] = jnp.broadcast_to(m, (MLA_TQ, 128))

    def weighted(j):
        rows = pl.ds(pl.multiple_of(j * TK, TK), TK)
        for hh, hs in enumerate(halves):
            m = mx_ref[hh]
            p = jnp.concatenate([jnp.exp2(s_ref[hh, j, :, lb] - m) for lb in lane_blocks], axis=1)
            acc_ref[hh] += _dot(p.astype(BF16), v_ref[rows, hs])

    pairs_then_rest(last + 1, weighted)

    even, odd = acc_ref[0], acc_ref[1]
    lane = lax.broadcasted_iota(I32, (MLA_TQ, 2 * MLA_V), 1)
    out = jnp.where(lane < MLA_V, even / even[:, MLA_V:MLA_V + 1], odd / odd[:, 0:1])
    o_ref[...] = out.astype(BF16)


def _mla_attention(q, k, v, batch, seq):
    t = q.shape[0]
    nq = seq // MLA_TQ
    pairs = MLA_HEADS // 2
    return pl.pallas_call(
        _mla_body,
        grid=(batch, pairs, nq),
        in_specs=[pl.BlockSpec((MLA_TQ, 2 * HEAD_PAD), lambda b, p, i: (b * nq + i, p)),
                  pl.BlockSpec((seq, 2 * HEAD_PAD), lambda b, p, i: (b, p)),
                  pl.BlockSpec((seq, 2 * HEAD_PAD), lambda b, p, i: (b, p))],
        out_specs=pl.BlockSpec((MLA_TQ, 2 * MLA_V), lambda b, p, i: (b * nq + i, p)),
        out_shape=jax.ShapeDtypeStruct((t, MLA_HEADS * MLA_V), BF16),
        scratch_shapes=[pltpu.VMEM((2, seq // TK, MLA_TQ, TK), F32), pltpu.VMEM((2, MLA_TQ, 128), F32),
                        pltpu.VMEM((2, MLA_TQ, HEAD_PAD), F32)],
        compiler_params=_cparams(("parallel", "parallel", "arbitrary")),
        name="mla_attention",
    )(q, k, v)


def _even_out_body(x_ref, a_ref, halo_ref, att_ref, pw_ref, ps_ref, wo_ref, g_ref, wg_ref, wu_ref, wd_ref,
                   o_ref, ext_ref, *, seq):
    pos0 = (pl.program_id(0) * TM) % seq
    ext_ref[0:POOL_HALO, :] = jnp.where(pos0 == 0, 0.0, halo_ref[...])
    ext_ref[POOL_HALO:, :] = a_ref[...]
    pos = lax.broadcasted_iota(I32, (TM, 1), 0) + pos0
    y = x_ref[...] + _dot(att_ref[...], wo_ref[POOL_WIDTH:, :])
    for g_i, win in enumerate(POOL_WINDOWS):
        gs = slice(g_i * POOL_CH, (g_i + 1) * POOL_CH)
        a_g = ext_ref[POOL_HALO:, gs]
        ssum = a_g
        for j in range(1, win):
            ssum = ssum + ext_ref[POOL_HALO - j:POOL_HALO - j + TM, gs]
        count = jnp.minimum(pos + 1, win).astype(F32)
        pooled = (ssum / count - a_g).astype(BF16)
        mixed = (_dot(pooled, pw_ref[g_i]) * ps_ref[:, gs]).astype(BF16)
        y = y + _dot(mixed, wo_ref[gs, :])
    h = _rms(y, g_ref[...]).astype(BF16)
    gate = _dot(h, wg_ref[...])
    up = _dot(h, wu_ref[...])
    act = (gate * jax.nn.sigmoid(gate) * up).astype(BF16)
    o_ref[...] = y + _dot(act, wd_ref[...])


def _even_out(x, a, att, pw, ps, wo, g, wg, wu, wd, seq):
    t = x.shape[0]
    row = lambda w: pl.BlockSpec((TM, w), lambda i: (i, 0))
    halo = pl.BlockSpec((POOL_HALO, POOL_WIDTH), lambda i: (jnp.maximum(i * (TM // POOL_HALO) - 1, 0), 0))
    return pl.pallas_call(
        functools.partial(_even_out_body, seq=seq),
        grid=(t // TM,),
        in_specs=[row(D_MODEL), row(POOL_WIDTH), halo, row(MLA_HEADS * MLA_V), _const_spec(pw.shape),
                  _const_spec(ps.shape), _const_spec(wo.shape), _const_spec(g.shape), _const_spec(wg.shape),
                  _const_spec(wu.shape), _const_spec(wd.shape)],
        out_specs=row(D_MODEL),
        out_shape=jax.ShapeDtypeStruct((t, D_MODEL), F32),
        scratch_shapes=[pltpu.VMEM((TM + POOL_HALO, POOL_WIDTH), F32)],
        compiler_params=_cparams(("parallel",)),
        name="even_out",
    )(x, a, a, att, pw, ps, wo, g, wg, wu, wd)


def _odd_in_body(x_ref, g_ref, win_ref, ng_ref, nb_ref, sw_ref, sb_ref, qkv_ref, sg_ref):
    h = _rms(x_ref[...], g_ref[...]).astype(BF16)
    z = _dot(h, win_ref[...])
    qkv_ref[:, :CA_WIDTH] = (z[:, :CA_WIDTH] * math.log2(math.e)).astype(BF16)
    qkv_ref[:, CA_WIDTH:] = z[:, CA_WIDTH:3 * CA_WIDTH].astype(BF16)
    uv = jax.nn.gelu(z[:, 3 * CA_WIDTH:])
    for g_i in range(SGU_GROUPS):
        gs = slice(g_i * SGU_CH, (g_i + 1) * SGU_CH)
        u = uv[:, gs]
        gv = uv[:, SGU_WIDTH + g_i * SGU_CH:SGU_WIDTH + (g_i + 1) * SGU_CH]
        mu = jnp.mean(gv, axis=-1, keepdims=True)
        xc = gv - mu
        vn = xc * lax.rsqrt(jnp.mean(xc * xc, axis=-1, keepdims=True) + EPS) * ng_ref[:, gs] + nb_ref[:, gs]
        vn = vn.astype(BF16)
        w = sw_ref[g_i]
        for c_i in range(TM // SGU_CHUNK):
            rs = slice(c_i * SGU_CHUNK, (c_i + 1) * SGU_CHUNK)
            mixed = _dot(w, vn[rs, :]) + sb_ref[g_i]
            sg_ref[rs, gs] = (u[rs, :] * mixed).astype(BF16)


def _odd_in(x, g, win, ng, nb, sw, sb):
    t = x.shape[0]
    row = lambda w: pl.BlockSpec((TM, w), lambda i: (i, 0))
    return pl.pallas_call(
        _odd_in_body,
        grid=(t // TM,),
        in_specs=[row(D_MODEL), _const_spec(g.shape), _const_spec(win.shape), _const_spec(ng.shape),
                  _const_spec(nb.shape), _const_spec(sw.shape), _const_spec(sb.shape)],
        out_specs=[row(3 * CA_WIDTH), row(SGU_WIDTH)],
        out_shape=[jax.ShapeDtypeStruct((t, 3 * CA_WIDTH), BF16), jax.ShapeDtypeStruct((t, SGU_WIDTH), BF16)],
        compiler_params=_cparams(("parallel",)),
        name="odd_in",
    )(x, g, win, ng, nb, sw, sb)


BAND_BLOCKS = 3


def _band_body(q_ref, k0_ref, k1_ref, k2_ref, v0_ref, v1_ref, v2_ref, bias_ref, o_ref):
    pair_w = 2 * CA_HEAD_DIM
    low = lax.broadcasted_iota(I32, (TQ, pair_w), 1) < CA_HEAD_DIM
    k_refs = (k0_ref, k1_ref, k2_ref)
    v_refs = (v0_ref, v1_ref, v2_ref)
    add = lambda a, b: a + b
    per_tile = TQ // 128
    for pair in range(CA_HEADS // 2):
        ps = slice(pair * pair_w, (pair + 1) * pair_w)
        q_pair = q_ref[:, ps]
        outs = []
        for hh in range(2):
            q = jnp.where(low if hh == 0 else jnp.logical_not(low), q_pair, jnp.zeros_like(q_pair))
            s = []
            for jj in range(BAND_BLOCKS):
                sj = _dot_nt(q, k_refs[jj][:, ps]) + bias_ref[2 * pair + hh, :, jj * TQ:(jj + 1) * TQ]
                s += [sj[:, c * 128:(c + 1) * 128] for c in range(per_tile)]
            m = jnp.max(functools.reduce(jnp.maximum, s), axis=-1, keepdims=True)
            mb = jnp.broadcast_to(m, (TQ, 128))
            p = [jnp.exp2(x - mb) for x in s]
            l = jnp.sum(functools.reduce(add, p), axis=-1, keepdims=True)
            acc = functools.reduce(add, [
                _dot(jnp.concatenate(p[jj * per_tile:(jj + 1) * per_tile], axis=1).astype(BF16), v_refs[jj][:, ps])
                for jj in range(BAND_BLOCKS)])
            outs.append(acc / l)
        o_ref[:, ps] = jnp.where(low, outs[0], outs[1]).astype(BF16)


def _band_attention(qkv, bias, batch, seq):
    t = qkv.shape[0]
    nq = seq // TQ

    def kv_spec(jj, col):
        return pl.BlockSpec((TQ, CA_WIDTH), lambda b, i: (b * nq + jnp.maximum(i - (BAND_BLOCKS - 1) + jj, 0), col))

    return pl.pallas_call(
        _band_body,
        grid=(batch, nq),
        in_specs=[pl.BlockSpec((TQ, CA_WIDTH), lambda b, i: (b * nq + i, 0))]
                 + [kv_spec(jj, 1) for jj in range(BAND_BLOCKS)]
                 + [kv_spec(jj, 2) for jj in range(BAND_BLOCKS)]
                 + [pl.BlockSpec((None, CA_HEADS, TQ, BAND_BLOCKS * TQ),
                                 lambda b, i: (jnp.minimum(i, BAND_BLOCKS - 1), 0, 0, 0))],
        out_specs=pl.BlockSpec((TQ, CA_WIDTH), lambda b, i: (b * nq + i, 0)),
        out_shape=jax.ShapeDtypeStruct((t, CA_WIDTH), BF16),
        compiler_params=_cparams(("parallel", "arbitrary")),
        name="band_attention",
    )(qkv, qkv, qkv, qkv, qkv, qkv, qkv, bias)


ROUTE_LANES = 128
ROUTE_ROWS = 8


def _odd_out_body(x_ref, att_ref, sg_ref, wo_ref, g_ref, wr_ref, tri_ref, x1_ref, h_ref, route_ref, cnt_ref, run_ref):
    @pl.when(pl.program_id(0) == 0)
    def _():
        run_ref[...] = jnp.zeros_like(run_ref)

    y = x_ref[...] + _dot(att_ref[...], wo_ref[:CA_WIDTH, :]) + _dot(sg_ref[...], wo_ref[CA_WIDTH:, :])
    x1_ref[...] = y
    h = _rms(y, g_ref[...])
    _store_token_tiles(h_ref, h)
    h_hi = h.astype(BF16)
    h_lo = (h - h_hi.astype(F32)).astype(BF16)
    logits = _dot(h_hi, wr_ref[0]) + (_dot(h_lo, wr_ref[0]) + _dot(h_hi, wr_ref[1]))
    lt = logits.T[:N_EXPERTS]
    row = lax.broadcasted_iota(I32, (N_EXPERTS, TM), 0)
    m1 = jnp.max(lt, axis=0, keepdims=True)
    e1 = jnp.min(jnp.where(lt == m1, row, N_EXPERTS), axis=0, keepdims=True)
    rest = jnp.where(row == e1, NEG, lt)
    m2 = jnp.max(rest, axis=0, keepdims=True)
    e2 = jnp.min(jnp.where(rest == m2, row, N_EXPERTS), axis=0, keepdims=True)
    ex = jnp.exp(m2 - m1)
    w1 = 1.0 / (1.0 + ex)
    w2 = ex / (1.0 + ex)
    hot1 = (row == e1).astype(F32)
    hot2 = (row == e2).astype(F32)
    sel = hot1 + hot2
    before = _dot(sel.astype(BF16), tri_ref[...]) + run_ref[:, 0:1]
    rank1 = jnp.sum(hot1 * before, axis=0, keepdims=True)
    rank2 = jnp.sum(hot2 * before, axis=0, keepdims=True)
    unused = jnp.zeros((ROUTE_ROWS - 6, TM), F32)
    route_ref[...] = jnp.concatenate([e1.astype(F32), e2.astype(F32), rank1, rank2, w1, w2, unused], axis=0)
    total = run_ref[:, 0:1] + jnp.sum(sel, axis=1, keepdims=True)
    run_ref[...] = jnp.broadcast_to(total, run_ref.shape)
    cnt_ref[...] = jnp.broadcast_to(total, cnt_ref.shape)


def _odd_out(x, att, sg, wo, g, wr, tri):
    t = x.shape[0]
    row = lambda w: pl.BlockSpec((TM, w), lambda i: (i, 0))
    return pl.pallas_call(
        _odd_out_body,
        grid=(t // TM,),
        in_specs=[row(D_MODEL), row(CA_WIDTH), row(SGU_WIDTH), _const_spec(wo.shape), _const_spec(g.shape),
                  _const_spec(wr.shape), _const_spec(tri.shape)],
        out_specs=[row(D_MODEL), pl.BlockSpec((TM * TOKEN_TILE_ROWS, 128), lambda i: (i, 0)),
                   pl.BlockSpec((ROUTE_ROWS, TM), lambda i: (0, i)),
                   pl.BlockSpec((N_EXPERTS, ROUTE_LANES), lambda i: (0, 0))],
        out_shape=[jax.ShapeDtypeStruct((t, D_MODEL), F32), jax.ShapeDtypeStruct((t * TOKEN_TILE_ROWS, 128), F32),
                   jax.ShapeDtypeStruct((ROUTE_ROWS, t), F32), jax.ShapeDtypeStruct((N_EXPERTS, ROUTE_LANES), F32)],
        scratch_shapes=[pltpu.VMEM((N_EXPERTS, ROUTE_LANES), F32)],
        compiler_params=_cparams(("arbitrary",)),
        name="odd_out_router",
    )(x, att, sg, wo, g, wr, tri)


def _moe_body(tile_e_ref, n_used_ref, src_ref, dst_ref, h_hbm, wg_ref, wu_ref, wd_ref, o_hbm,
              xf_ref, xb_ref, acc_ref, y_ref, gsem, ssem):
    i = pl.program_id(0)
    j = pl.program_id(1)
    n_used = n_used_ref[0]
    rows_per_step = MOE_TM // MOE_FF_STEPS

    ttr = TOKEN_TILE_ROWS
    tile_rows = MOE_TM * ttr

    def token_rows(ref, first_row):
        return ref.at[pl.ds(pl.multiple_of(first_row, ttr), ttr), :]

    def gather_copy(tile, r, slot):
        return pltpu.make_async_copy(token_rows(h_hbm, src_ref[tile * MOE_TM + r]),
                                     token_rows(xf_ref.at[slot], r * ttr), gsem)

    def scatter_copy(tile, r, slot):
        return pltpu.make_async_copy(token_rows(y_ref.at[slot], r * ttr),
                                     token_rows(o_hbm, dst_ref[tile * MOE_TM + r]), ssem)

    def wait_tile_gather(slot):
        pltpu.make_async_copy(h_hbm.at[pl.ds(0, tile_rows), :], xf_ref.at[slot], gsem).wait()

    def wait_tile_scatter(slot):
        pltpu.make_async_copy(y_ref.at[slot], o_hbm.at[pl.ds(0, tile_rows), :], ssem).wait()

    def start_all(copy_fn):
        lax.fori_loop(0, MOE_TM, lambda r, c: (copy_fn(r).start(), c)[1], 0)

    @pl.when(i < n_used)
    def _():
        cur = lax.rem(i, 2)
        other = 1 - cur
        nxt_tile = jnp.minimum(i + 1, n_used - 1)
        prev_tile = jnp.maximum(i - 1, 0)

        @pl.when(jnp.logical_and(i == 0, j == 0))
        def _():
            y_ref[...] = jnp.zeros_like(y_ref)
            spare0 = o_hbm.shape[0] - N_EXPERTS * tile_rows
            for e_i in range(N_EXPERTS):
                fill = pltpu.make_async_copy(y_ref.at[0], o_hbm.at[pl.ds(spare0 + e_i * tile_rows, tile_rows), :],
                                             ssem)
                fill.start()
                fill.wait()
            start_all(lambda r: gather_copy(0, r, 0))

        @pl.when(j == 0)
        def _():
            wait_tile_gather(cur)
            xb_ref[...] = _load_token_tiles(xf_ref.at[cur], MOE_TM).astype(BF16)
            acc_ref[...] = jnp.zeros_like(acc_ref)

        base = pl.multiple_of(j * rows_per_step, 8)
        for rr in range(rows_per_step):
            gather_copy(nxt_tile, base + rr, other).start()
            scatter_copy(prev_tile, base + rr, other).start()
        xb = xb_ref[...]
        gate = _dot(xb, wg_ref[...])
        up = _dot(xb, wu_ref[...])
        act = (gate * jax.nn.sigmoid(gate) * up).astype(BF16)
        acc_ref[...] += _dot(act, wd_ref[...])

        @pl.when(j == MOE_FF_STEPS - 1)
        def _():
            wait_tile_scatter(other)
            _store_token_tiles(y_ref.at[cur], acc_ref[...])

        @pl.when(jnp.logical_and(i == n_used - 1, j == MOE_FF_STEPS - 1))
        def _():
            start_all(lambda r: scatter_copy(i, r, cur))
            wait_tile_scatter(cur)
            wait_tile_gather(other)


def _moe(h, tile_e, n_used, src, dst, wg, wu, wd, layer):
    tokens = h.shape[0] // TOKEN_TILE_ROWS
    n_tiles = tile_e.shape[0]
    last = MOE_FF_STEPS - 1

    def ff_idx(i, j, n_used_ref):
        return jnp.where(i < n_used_ref[0], j, last)

    grid_spec = pltpu.PrefetchScalarGridSpec(
        num_scalar_prefetch=4,
        grid=(n_tiles, MOE_FF_STEPS),
        in_specs=[pl.BlockSpec(memory_space=pl.ANY),
                  pl.BlockSpec((None, None, D_MODEL, MOE_TF),
                               lambda i, j, te, nu, sr, ds: (layer, te[i], 0, ff_idx(i, j, nu))),
                  pl.BlockSpec((None, None, D_MODEL, MOE_TF),
                               lambda i, j, te, nu, sr, ds: (layer, te[i], 0, ff_idx(i, j, nu))),
                  pl.BlockSpec((None, None, MOE_TF, D_MODEL),
                               lambda i, j, te, nu, sr, ds: (layer, te[i], ff_idx(i, j, nu), 0))],
        out_specs=pl.BlockSpec(memory_space=pl.ANY),
        scratch_shapes=[pltpu.VMEM((2, MOE_TM * TOKEN_TILE_ROWS, 128), F32), pltpu.VMEM((MOE_TM, D_MODEL), BF16),
                        pltpu.VMEM((MOE_TM, D_MODEL), F32), pltpu.VMEM((2, MOE_TM * TOKEN_TILE_ROWS, 128), F32),
                        pltpu.SemaphoreType.DMA, pltpu.SemaphoreType.DMA],
    )
    return pl.pallas_call(
        _moe_body,
        grid_spec=grid_spec,
        out_shape=jax.ShapeDtypeStruct(((2 * tokens + N_EXPERTS * MOE_TM) * TOKEN_TILE_ROWS, 128), F32),
        compiler_params=_cparams(("arbitrary", "arbitrary")),
        name="moe_experts",
    )(tile_e, n_used, src, dst, h, wg, wu, wd)


def _moe_plan(route, counts, tokens):
    n_tiles = (2 * tokens + N_EXPERTS * (MOE_TM - 1)) // MOE_TM
    experts = jnp.arange(N_EXPERTS, dtype=I32)
    cnt = counts[:, 0].astype(I32)
    e = route[0:2].astype(I32)
    rank = route[2:4].astype(I32)
    tiles_per_e = (cnt + MOE_TM - 1) // MOE_TM
    tile_end = jnp.cumsum(tiles_per_e)
    tile_start = tile_end - tiles_per_e
    row_start = jnp.sum(jnp.where(e[..., None] == experts, tile_start * MOE_TM, 0), axis=-1)
    pos = row_start + rank
    slot_row = jnp.arange(2 * tokens, dtype=I32)
    n_used = tile_end[-1]
    tile = jnp.minimum(jnp.arange(n_tiles, dtype=I32), n_used - 1)
    tile_e = jnp.sum((tile_end[None, :] <= tile[:, None]).astype(I32), axis=1)
    spare = 2 * tokens + tile_e[:, None] * MOE_TM + jnp.arange(MOE_TM, dtype=I32)[None, :]
    dst = spare.reshape(-1).at[pos.reshape(-1)].set(slot_row)
    src = jnp.where(dst < 2 * tokens, dst % tokens, 0)
    return tile_e, n_used.reshape(1), src * TOKEN_TILE_ROWS, dst * TOKEN_TILE_ROWS


def _combine_body(x_ref, oa_ref, ob_ref, gate_ref, g_ref, o_ref, *, final):
    w1 = gate_ref[:, 0:1]
    w2 = gate_ref[:, 1:2]
    y = x_ref[...] + (w1 * _load_token_tiles(oa_ref, TM) + w2 * _load_token_tiles(ob_ref, TM))
    o_ref[...] = _rms(y, g_ref[...]) if final else y


def _combine(x, o2, route, g, final):
    t = x.shape[0]
    nt = t // TM
    gates = route[4:6].T
    return pl.pallas_call(
        functools.partial(_combine_body, final=final),
        grid=(nt,),
        in_specs=[pl.BlockSpec((TM, D_MODEL), lambda i: (i, 0)),
                  pl.BlockSpec((TM * TOKEN_TILE_ROWS, 128), lambda i: (i, 0)),
                  pl.BlockSpec((TM * TOKEN_TILE_ROWS, 128), lambda i: (i + nt, 0)),
                  pl.BlockSpec((TM, 2), lambda i: (i, 0)), _const_spec(g.shape)],
        out_specs=pl.BlockSpec((TM, D_MODEL), lambda i: (i, 0)),
        out_shape=jax.ShapeDtypeStruct((t, D_MODEL), F32),
        compiler_params=_cparams(("parallel",)),
        name="moe_combine",
    )(x, o2, o2, gates, g)


def _rot_half(w):
    half = w.shape[-1] // 2
    return jnp.concatenate([-w[..., half:], w[..., :half]], axis=-1)


def _rope_tables(seq):
    inv = 1.0 / (ROPE_THETA ** (jnp.arange(0, MLA_ROPE, 2, dtype=F32) / MLA_ROPE))
    ang = jnp.arange(seq, dtype=F32)[:, None] * inv[None, :]
    cos, sin = jnp.cos(ang), jnp.sin(ang)
    pad = HEAD_PAD - MLA_NOPE - MLA_ROPE
    c = jnp.concatenate([jnp.ones((seq, MLA_NOPE), F32), cos, cos, jnp.zeros((seq, pad), F32)], axis=1)
    s = jnp.concatenate([jnp.zeros((seq, MLA_NOPE), F32), sin, sin, jnp.zeros((seq, pad), F32)], axis=1)
    scale = (MLA_NOPE + MLA_ROPE) ** -0.5 * math.log2(math.e)
    return c * scale, s * scale, c, s


def _even_weights(w_in, w_q_up, w_kv_up):
    d = w_in.shape[0]
    pad = HEAD_PAD - MLA_NOPE - MLA_ROPE
    kr = w_in[:, POOL_WIDTH + MLA_Q_RANK + MLA_KV_RANK:]
    place = lambda m: jnp.concatenate([jnp.zeros((d, MLA_NOPE), F32), m, jnp.zeros((d, pad), F32)], axis=1)
    win = jnp.concatenate([w_in[:, :POOL_WIDTH + MLA_Q_RANK + MLA_KV_RANK], place(kr), place(_rot_half(kr))], axis=1)
    wq = w_q_up.reshape(MLA_Q_RANK, MLA_HEADS, MLA_NOPE + MLA_ROPE)
    nope, rope = wq[..., :MLA_NOPE], wq[..., MLA_NOPE:]
    zq = lambda n: jnp.zeros((MLA_Q_RANK, MLA_HEADS, n), F32)
    wq_main = jnp.concatenate([nope, rope, zq(pad)], axis=-1).reshape(MLA_Q_RANK, -1)
    wq_rot = jnp.concatenate([zq(MLA_NOPE), _rot_half(rope), zq(pad)], axis=-1).reshape(MLA_Q_RANK, -1)
    wkv = w_kv_up.reshape(MLA_KV_RANK, MLA_HEADS, MLA_NOPE + MLA_V)
    k_pad = jnp.concatenate([wkv[..., :MLA_NOPE], jnp.zeros((MLA_KV_RANK, MLA_HEADS, HEAD_PAD - MLA_NOPE), F32)],
                            axis=-1).reshape(MLA_KV_RANK, -1)
    v_pairs = wkv[..., MLA_NOPE:].reshape(MLA_KV_RANK, MLA_HEADS // 2, 2, MLA_V)
    v_gap = jnp.zeros((MLA_KV_RANK, MLA_HEADS // 2, 2 * (HEAD_PAD - MLA_V)), F32)
    v_ext = jnp.concatenate([v_pairs[:, :, 0], v_gap, v_pairs[:, :, 1]], axis=-1).reshape(MLA_KV_RANK, -1)
    return (win.astype(BF16), jnp.concatenate([wq_main, wq_rot], axis=1).astype(BF16),
            jnp.concatenate([k_pad, v_ext], axis=1).astype(BF16))


def _v_ones():
    lane = jnp.arange(MLA_HEADS * HEAD_PAD) % (2 * HEAD_PAD)
    return jnp.logical_and(lane >= MLA_V, lane < 2 * HEAD_PAD - MLA_V).astype(F32).reshape(1, -1)


def _band_bias(rel_bias):
    band = BAND_BLOCKS * TQ
    qpos = jnp.arange(TQ)[:, None] + (BAND_BLOCKS - 1) * TQ
    kpos = jnp.arange(band)[None, :]
    span = band + TQ - 1
    rel = (BAND_BLOCKS - 1) * TQ + (TQ - 1) - jnp.arange(span)
    by_rel = rel_bias[:, jnp.clip(rel, -REL_CLIP, REL_CLIP) + REL_CLIP].astype(F32)
    rolled = jnp.roll(by_rel, -(TQ - 1), axis=1)
    bias = jnp.tile(rolled, (1, TQ))[:, :TQ * (span - 1)].reshape(-1, TQ, span - 1)[:, :, :band]
    dc = qpos // CHUNK - kpos // CHUNK
    valid = jnp.logical_and(dc >= 0, dc <= CA_LEFT_CHUNKS)
    started = kpos // TQ >= (BAND_BLOCKS - 1) - jnp.arange(BAND_BLOCKS)[:, None, None]
    ok = jnp.logical_and(valid[None], started)[:, None]
    return jnp.where(ok, bias[None] * math.log2(math.e), NEG)


def kernel(x, ev_norm_mix, ev_w_in, ev_pool_w, ev_pool_scale, ev_q_norm, ev_w_q_up, ev_kv_norm, ev_w_kv_up, ev_w_out, ev_norm_ffn, ev_ffn_w_gate, ev_ffn_w_up, ev_ffn_w_down, od_norm_mix, od_w_in, od_rel_bias, od_sgu_norm_g, od_sgu_norm_b, od_sgu_w, od_sgu_b, od_w_out, od_norm_ffn, od_router, od_moe_w_gate, od_moe_w_up, od_moe_w_down, final_norm):
    batch, seq, d = x.shape
    tokens = batch * seq
    assert d == D_MODEL and seq % TM == 0 and TM % SGU_CHUNK == 0 and TM % TQ == 0 and seq % TK == 0
    depth = ev_w_in.shape[0] + od_w_in.shape[0]
    xt = x.reshape(tokens, d)
    tabs = _rope_tables(seq)
    tri = jnp.triu(jnp.ones((TM, TM), F32), k=1).astype(BF16)
    causal = jnp.tril(jnp.ones((SGU_CHUNK, SGU_CHUNK), bool))
    vec = lambda v: v.reshape(1, -1).astype(F32)
    moe_wg, moe_wu, moe_wd = (w.astype(BF16) for w in (od_moe_w_gate, od_moe_w_up, od_moe_w_down))

    for layer in range(depth):
        i = layer // 2
        if layer % 2 == 0:
            win, wq, wkv = _even_weights(ev_w_in[i], ev_w_q_up[i], ev_w_kv_up[i])
            a, q, k, v = _even_in(xt, vec(ev_norm_mix[i]), win, vec(ev_q_norm[i]), wq, vec(ev_kv_norm[i]), wkv,
                                  _v_ones(), tabs, seq)
            att = _mla_attention(q, k, v, batch, seq)
            xt = _even_out(xt, a, att, ev_pool_w[i].astype(BF16), vec(ev_pool_scale[i]), ev_w_out[i].astype(BF16),
                           vec(ev_norm_ffn[i]), ev_ffn_w_gate[i].astype(BF16), ev_ffn_w_up[i].astype(BF16),
                           ev_ffn_w_down[i].astype(BF16), seq)
        else:
            w_in = od_w_in[i]
            win = jnp.concatenate([w_in[:, :CA_WIDTH] * (CA_HEAD_DIM ** -0.5), w_in[:, CA_WIDTH:]], axis=1).astype(BF16)
            sw = jnp.where(causal[None], od_sgu_w[i], 0.0).astype(BF16)
            sb = jnp.broadcast_to(od_sgu_b[i][:, :, None], (SGU_GROUPS, SGU_CHUNK, SGU_CH)).astype(F32)
            qkv, sg = _odd_in(xt, vec(od_norm_mix[i]), win, vec(od_sgu_norm_g[i]), vec(od_sgu_norm_b[i]), sw, sb)
            att = _band_attention(qkv, _band_bias(od_rel_bias[i]), batch, seq)
            wr = jnp.concatenate([od_router[i], jnp.zeros((d, ROUTE_LANES - N_EXPERTS), F32)], axis=1)
            wr_hi = wr.astype(BF16)
            wr = jnp.stack([wr_hi, (wr - wr_hi.astype(F32)).astype(BF16)])
            x1, h, route, counts = _odd_out(xt, att, sg, od_w_out[i].astype(BF16), vec(od_norm_ffn[i]), wr, tri)
            tile_e, n_used, src, dst = _moe_plan(route, counts, tokens)
            o2 = _moe(h, tile_e, n_used, src, dst, moe_wg, moe_wu, moe_wd, i)
            last = layer == depth - 1
            xt = _combine(x1, o2, route, vec(final_norm), last)
    return xt.reshape(batch, seq, d)
```

```python
import functools
import math

import jax
import jax.numpy as jnp
from jax import lax
from jax.experimental import pallas as pl
from jax.experimental.pallas import tpu as pltpu

F32 = jnp.float32
BF16 = jnp.bfloat16
I32 = jnp.int32

D_MODEL = 1024
CHUNK = 64
EPS = 1e-6

POOL_GROUPS = 4
POOL_CH = 128
POOL_WIDTH = 512
POOL_WINDOWS = (2, 4, 8, 16)
POOL_HALO = 16

MLA_HEADS = 8
MLA_NOPE = 64
MLA_ROPE = 32
MLA_V = 64
MLA_Q_RANK = 256
MLA_KV_RANK = 128
ROPE_THETA = 10000.0
HEAD_PAD = 128

CA_HEADS = 8
CA_HEAD_DIM = 64
CA_LEFT_CHUNKS = 8
REL_CLIP = 128
CA_WIDTH = CA_HEADS * CA_HEAD_DIM

SGU_GROUPS = 4
SGU_CH = 128
SGU_WIDTH = 512
SGU_CHUNK = 128

D_FF = 2816
N_EXPERTS = 8
EXPERT_FF = 3584

NEG = -1e30

TM = 512
TQ = 256
TK = 512
MLA_TQ = 512
MOE_TF = 512
MOE_FF_STEPS = EXPERT_FF // MOE_TF
MOE_TM = 80 * MOE_FF_STEPS
VMEM_LIMIT = 56 * 1024 * 1024


def _cparams(sem):
    return pltpu.CompilerParams(dimension_semantics=sem, vmem_limit_bytes=VMEM_LIMIT)


def _rms(x, g):
    return x * lax.rsqrt(jnp.mean(x * x, axis=-1, keepdims=True) + EPS) * g


def _dot(a, b):
    return jnp.dot(a, b, preferred_element_type=F32)


def _dot_nt(a, b):
    return lax.dot_general(a, b, (((1,), (1,)), ((), ())), preferred_element_type=F32)


TOKEN_TILE_ROWS = D_MODEL // 128


def _store_token_tiles(ref, x):
    n = x.shape[0]
    for s in range(TOKEN_TILE_ROWS):
        ref[pl.ds(s, n, stride=TOKEN_TILE_ROWS), :] = x[:, s * 128:(s + 1) * 128]


def _load_token_tiles(ref, n):
    return jnp.concatenate([ref[pl.ds(s, n, stride=TOKEN_TILE_ROWS), :] for s in range(TOKEN_TILE_ROWS)], axis=1)


def _const_spec(shape):
    nd = len(shape)
    return pl.BlockSpec(shape, lambda *_: (0,) * nd, pipeline_mode=pl.Buffered(1))


def _even_in_body(x_ref, g_ref, win_ref, qn_ref, wq_ref, kvn_ref, wkv_ref, vones_ref, cq_ref, sq_ref, ck_ref, sk_ref,
                  a_ref, q_ref, k_ref, v_ref):
    h = _rms(x_ref[...], g_ref[...]).astype(BF16)
    z = _dot(h, win_ref[...])
    a_ref[...] = z[:, :POOL_WIDTH]
    o = POOL_WIDTH
    cq = z[:, o:o + MLA_Q_RANK]
    o += MLA_Q_RANK
    ckv = z[:, o:o + MLA_KV_RANK]
    o += MLA_KV_RANK
    kr = z[:, o:o + HEAD_PAD]
    kr_rot = z[:, o + HEAD_PAD:o + 2 * HEAD_PAD]

    hw = MLA_HEADS * HEAD_PAD
    qq = _dot(_rms(cq, qn_ref[...]).astype(BF16), wq_ref[...])
    cq_t, sq_t = cq_ref[...], sq_ref[...]
    for h_i in range(MLA_HEADS):
        sl = slice(h_i * HEAD_PAD, (h_i + 1) * HEAD_PAD)
        sl_rot = slice(hw + h_i * HEAD_PAD, hw + (h_i + 1) * HEAD_PAD)
        q_ref[:, sl] = (qq[:, sl] * cq_t + qq[:, sl_rot] * sq_t).astype(BF16)

    kv = _dot(_rms(ckv, kvn_ref[...]).astype(BF16), wkv_ref[...])
    k_rope = kr * ck_ref[...] + kr_rot * sk_ref[...]
    for h_i in range(MLA_HEADS):
        sl = slice(h_i * HEAD_PAD, (h_i + 1) * HEAD_PAD)
        k_ref[:, sl] = (kv[:, sl] + k_rope).astype(BF16)
    v_ref[...] = (kv[:, hw:] + vones_ref[...]).astype(BF16)


def _even_in(x, g, win, qn, wq, kvn, wkv, vones, tabs, seq):
    t = x.shape[0]
    hw = MLA_HEADS * HEAD_PAD
    row = lambda w: pl.BlockSpec((TM, w), lambda i: (i, 0))
    tab = pl.BlockSpec((TM, HEAD_PAD), lambda i: (i % (seq // TM), 0))
    return pl.pallas_call(
        _even_in_body,
        grid=(t // TM,),
        in_specs=[row(D_MODEL), _const_spec(g.shape), _const_spec(win.shape), _const_spec(qn.shape),
                  _const_spec(wq.shape), _const_spec(kvn.shape), _const_spec(wkv.shape), _const_spec(vones.shape),
                  tab, tab, tab, tab],
        out_specs=[row(POOL_WIDTH), row(hw), row(hw), row(hw)],
        out_shape=[jax.ShapeDtypeStruct((t, POOL_WIDTH), F32), jax.ShapeDtypeStruct((t, hw), BF16),
                   jax.ShapeDtypeStruct((t, hw), BF16), jax.ShapeDtypeStruct((t, hw), BF16)],
        compiler_params=_cparams(("parallel",)),
        name="even_in",
    )(x, g, win, qn, wq, kvn, wkv, vones, *tabs)


def _mla_body(q_ref, k_ref, v_ref, o_ref, s_ref, mx_ref, acc_ref):
    qi = pl.program_id(2)
    last = (qi * MLA_TQ) // TK
    halves = (slice(0, HEAD_PAD), slice(HEAD_PAD, 2 * HEAD_PAD))
    lane_blocks = [slice(c * 128, (c + 1) * 128) for c in range(TK // 128)]
    q_chunk = lax.broadcasted_iota(I32, (MLA_TQ, TK), 0) // CHUNK + qi * (MLA_TQ // CHUNK)
    k_chunk = lax.broadcasted_iota(I32, (MLA_TQ, TK), 1) // CHUNK + last * (TK // CHUNK)
    visible = k_chunk <= q_chunk

    mx_ref[...] = jnp.full(mx_ref.shape, NEG, F32)
    acc_ref[...] = jnp.zeros_like(acc_ref)

    def scores(j, masked):
        rows = pl.ds(pl.multiple_of(j * TK, TK), TK)
        for hh, hs in enumerate(halves):
            s = _dot_nt(q_ref[:, hs], k_ref[rows, hs])
            if masked:
                s = jnp.where(visible, s, NEG)
            s_ref[hh, j] = s
            mx_ref[hh] = functools.reduce(jnp.maximum, [s[:, lb] for lb in lane_blocks], mx_ref[hh])

    def pairs_then_rest(count, body):
        def two(jj, c):
            body(2 * jj)
            body(2 * jj + 1)
            return c

        lax.fori_loop(0, count // 2, two, 0)

        @pl.when(count % 2 == 1)
        def _():
            body(count - 1)

    pairs_then_rest(last, lambda j: scores(j, False))
    scores(last, True)

    for hh in range(2):
        m = jnp.max(mx_ref[hh], axis=-1, keepdims=True)
        mx_ref[hh] = jnp.broadcast_to(m, (MLA_TQ, 128))

    def weighted(j):
        rows = pl.ds(pl.multiple_of(j * TK, TK), TK)
        for hh, hs in enumerate(halves):
            m = mx_ref[hh]
            p = jnp.concatenate([jnp.exp2(s_ref[hh, j, :, lb] - m) for lb in lane_blocks], axis=1)
            acc_ref[hh] += _dot(p.astype(BF16), v_ref[rows, hs])

    pairs_then_rest(last + 1, weighted)

    even, odd = acc_ref[0], acc_ref[1]
    lane = lax.broadcasted_iota(I32, (MLA_TQ, 2 * MLA_V), 1)
    out = jnp.where(lane < MLA_V, even / even[:, MLA_V:MLA_V + 1], odd / odd[:, 0:1])
    o_ref[...] = out.astype(BF16)


def _mla_attention(q, k, v, batch, seq):
    t = q.shape[0]
    nq = seq // MLA_TQ
    pairs = MLA_HEADS // 2
    return pl.pallas_call(
        _mla_body,
        grid=(batch, pairs, nq),
        in_specs=[pl.BlockSpec((MLA_TQ, 2 * HEAD_PAD), lambda b, p, i: (b * nq + i, p)),
                  pl.BlockSpec((seq, 2 * HEAD_PAD), lambda b, p, i: (b, p)),
                  pl.BlockSpec((seq, 2 * HEAD_PAD), lambda b, p, i: (b, p))],
        out_specs=pl.BlockSpec((MLA_TQ, 2 * MLA_V), lambda b, p, i: (b * nq + i, p)),
        out_shape=jax.ShapeDtypeStruct((t, MLA_HEADS * MLA_V), BF16),
        scratch_shapes=[pltpu.VMEM((2, seq // TK, MLA_TQ, TK), F32), pltpu.VMEM((2, MLA_TQ, 128), F32),
                        pltpu.VMEM((2, MLA_TQ, HEAD_PAD), F32)],
        compiler_params=_cparams(("parallel", "parallel", "arbitrary")),
        name="mla_attention",
    )(q, k, v)


def _even_out_body(x_ref, a_ref, halo_ref, att_ref, pw_ref, ps_ref, wo_ref, g_ref, wg_ref, wu_ref, wd_ref,
                   o_ref, ext_ref, *, seq):
    pos0 = (pl.program_id(0) * TM) % seq
    ext_ref[0:POOL_HALO, :] = jnp.where(pos0 == 0, 0.0, halo_ref[...])
    ext_ref[POOL_HALO:, :] = a_ref[...]
    pos = lax.broadcasted_iota(I32, (TM, 1), 0) + pos0
    y = x_ref[...] + _dot(att_ref[...], wo_ref[POOL_WIDTH:, :])
    for g_i, win in enumerate(POOL_WINDOWS):
        gs = slice(g_i * POOL_CH, (g_i + 1) * POOL_CH)
        a_g = ext_ref[POOL_HALO:, gs]
        ssum = a_g
        for j in range(1, win):
            ssum = ssum + ext_ref[POOL_HALO - j:POOL_HALO - j + TM, gs]
        count = jnp.minimum(pos + 1, win).astype(F32)
        pooled = (ssum / count - a_g).astype(BF16)
        mixed = (_dot(pooled, pw_ref[g_i]) * ps_ref[:, gs]).astype(BF16)
        y = y + _dot(mixed, wo_ref[gs, :])
    h = _rms(y, g_ref[...]).astype(BF16)
    gate = _dot(h, wg_ref[...])
    up = _dot(h, wu_ref[...])
    act = (gate * jax.nn.sigmoid(gate) * up).astype(BF16)
    o_ref[...] = y + _dot(act, wd_ref[...])


def _even_out(x, a, att, pw, ps, wo, g, wg, wu, wd, seq):
    t = x.shape[0]
    row = lambda w: pl.BlockSpec((TM, w), lambda i: (i, 0))
    halo = pl.BlockSpec((POOL_HALO, POOL_WIDTH), lambda i: (jnp.maximum(i * (TM // POOL_HALO) - 1, 0), 0))
    return pl.pallas_call(
        functools.partial(_even_out_body, seq=seq),
        grid=(t // TM,),
        in_specs=[row(D_MODEL), row(POOL_WIDTH), halo, row(MLA_HEADS * MLA_V), _const_spec(pw.shape),
                  _const_spec(ps.shape), _const_spec(wo.shape), _const_spec(g.shape), _const_spec(wg.shape),
                  _const_spec(wu.shape), _const_spec(wd.shape)],
        out_specs=row(D_MODEL),
        out_shape=jax.ShapeDtypeStruct((t, D_MODEL), F32),
        scratch_shapes=[pltpu.VMEM((TM + POOL_HALO, POOL_WIDTH), F32)],
        compiler_params=_cparams(("parallel",)),
        name="even_out",
    )(x, a, a, att, pw, ps, wo, g, wg, wu, wd)


def _odd_in_body(x_ref, g_ref, win_ref, ng_ref, nb_ref, sw_ref, sb_ref, qkv_ref, sg_ref):
    h = _rms(x_ref[...], g_ref[...]).astype(BF16)
    z = _dot(h, win_ref[...])
    qkv_ref[:, :CA_WIDTH] = (z[:, :CA_WIDTH] * math.log2(math.e)).astype(BF16)
    qkv_ref[:, CA_WIDTH:] = z[:, CA_WIDTH:3 * CA_WIDTH].astype(BF16)
    uv = jax.nn.gelu(z[:, 3 * CA_WIDTH:])
    for g_i in range(SGU_GROUPS):
        gs = slice(g_i * SGU_CH, (g_i + 1) * SGU_CH)
        u = uv[:, gs]
        gv = uv[:, SGU_WIDTH + g_i * SGU_CH:SGU_WIDTH + (g_i + 1) * SGU_CH]
        mu = jnp.mean(gv, axis=-1, keepdims=True)
        xc = gv - mu
        vn = xc * lax.rsqrt(jnp.mean(xc * xc, axis=-1, keepdims=True) + EPS) * ng_ref[:, gs] + nb_ref[:, gs]
        vn = vn.astype(BF16)
        w = sw_ref[g_i]
        for c_i in range(TM // SGU_CHUNK):
            rs = slice(c_i * SGU_CHUNK, (c_i + 1) * SGU_CHUNK)
            mixed = _dot(w, vn[rs, :]) + sb_ref[g_i]
            sg_ref[rs, gs] = (u[rs, :] * mixed).astype(BF16)


def _odd_in(x, g, win, ng, nb, sw, sb):
    t = x.shape[0]
    row = lambda w: pl.BlockSpec((TM, w), lambda i: (i, 0))
    return pl.pallas_call(
        _odd_in_body,
        grid=(t // TM,),
        in_specs=[row(D_MODEL), _const_spec(g.shape), _const_spec(win.shape), _const_spec(ng.shape),
                  _const_spec(nb.shape), _const_spec(sw.shape), _const_spec(sb.shape)],
        out_specs=[row(3 * CA_WIDTH), row(SGU_WIDTH)],
        out_shape=[jax.ShapeDtypeStruct((t, 3 * CA_WIDTH), BF16), jax.ShapeDtypeStruct((t, SGU_WIDTH), BF16)],
        compiler_params=_cparams(("parallel",)),
        name="odd_in",
    )(x, g, win, ng, nb, sw, sb)


BAND_BLOCKS = 3


def _band_body(q_ref, k0_ref, k1_ref, k2_ref, v0_ref, v1_ref, v2_ref, bias_ref, o_ref):
    pair_w = 2 * CA_HEAD_DIM
    low = lax.broadcasted_iota(I32, (TQ, pair_w), 1) < CA_HEAD_DIM
    k_refs = (k0_ref, k1_ref, k2_ref)
    v_refs = (v0_ref, v1_ref, v2_ref)
    add = lambda a, b: a + b
    per_tile = TQ // 128
    for pair in range(CA_HEADS // 2):
        ps = slice(pair * pair_w, (pair + 1) * pair_w)
        q_pair = q_ref[:, ps]
        outs = []
        for hh in range(2):
            q = jnp.where(low if hh == 0 else jnp.logical_not(low), q_pair, jnp.zeros_like(q_pair))
            s = []
            for jj in range(BAND_BLOCKS):
                sj = _dot_nt(q, k_refs[jj][:, ps]) + bias_ref[2 * pair + hh, :, jj * TQ:(jj + 1) * TQ]
                s += [sj[:, c * 128:(c + 1) * 128] for c in range(per_tile)]
            m = jnp.max(functools.reduce(jnp.maximum, s), axis=-1, keepdims=True)
            mb = jnp.broadcast_to(m, (TQ, 128))
            p = [jnp.exp2(x - mb) for x in s]
            l = jnp.sum(functools.reduce(add, p), axis=-1, keepdims=True)
            acc = functools.reduce(add, [
                _dot(jnp.concatenate(p[jj * per_tile:(jj + 1) * per_tile], axis=1).astype(BF16), v_refs[jj][:, ps])
                for jj in range(BAND_BLOCKS)])
            outs.append(acc / l)
        o_ref[:, ps] = jnp.where(low, outs[0], outs[1]).astype(BF16)


def _band_attention(qkv, bias, batch, seq):
    t = qkv.shape[0]
    nq = seq // TQ

    def kv_spec(jj, col):
        return pl.BlockSpec((TQ, CA_WIDTH), lambda b, i: (b * nq + jnp.maximum(i - (BAND_BLOCKS - 1) + jj, 0), col))

    return pl.pallas_call(
        _band_body,
        grid=(batch, nq),
        in_specs=[pl.BlockSpec((TQ, CA_WIDTH), lambda b, i: (b * nq + i, 0))]
                 + [kv_spec(jj, 1) for jj in range(BAND_BLOCKS)]
                 + [kv_spec(jj, 2) for jj in range(BAND_BLOCKS)]
                 + [pl.BlockSpec((None, CA_HEADS, TQ, BAND_BLOCKS * TQ),
                                 lambda b, i: (jnp.minimum(i, BAND_BLOCKS - 1), 0, 0, 0))],
        out_specs=pl.BlockSpec((TQ, CA_WIDTH), lambda b, i: (b * nq + i, 0)),
        out_shape=jax.ShapeDtypeStruct((t, CA_WIDTH), BF16),
        compiler_params=_cparams(("parallel", "arbitrary")),
        name="band_attention",
    )(qkv, qkv, qkv, qkv, qkv, qkv, qkv, bias)


ROUTE_LANES = 128
ROUTE_ROWS = 8


def _odd_out_body(x_ref, att_ref, sg_ref, wo_ref, g_ref, wr_ref, tri_ref, x1_ref, h_ref, route_ref, cnt_ref, run_ref):
    @pl.when(pl.program_id(0) == 0)
    def _():
        run_ref[...] = jnp.zeros_like(run_ref)

    y = x_ref[...] + _dot(att_ref[...], wo_ref[:CA_WIDTH, :]) + _dot(sg_ref[...], wo_ref[CA_WIDTH:, :])
    x1_ref[...] = y
    h = _rms(y, g_ref[...])
    _store_token_tiles(h_ref, h)
    h_hi = h.astype(BF16)
    h_lo = (h - h_hi.astype(F32)).astype(BF16)
    logits = _dot(h_hi, wr_ref[0]) + (_dot(h_lo, wr_ref[0]) + _dot(h_hi, wr_ref[1]))
    lt = logits.T[:N_EXPERTS]
    row = lax.broadcasted_iota(I32, (N_EXPERTS, TM), 0)
    m1 = jnp.max(lt, axis=0, keepdims=True)
    e1 = jnp.min(jnp.where(lt == m1, row, N_EXPERTS), axis=0, keepdims=True)
    rest = jnp.where(row == e1, NEG, lt)
    m2 = jnp.max(rest, axis=0, keepdims=True)
    e2 = jnp.min(jnp.where(rest == m2, row, N_EXPERTS), axis=0, keepdims=True)
    ex = jnp.exp(m2 - m1)
    w1 = 1.0 / (1.0 + ex)
    w2 = ex / (1.0 + ex)
    hot1 = (row == e1).astype(F32)
    hot2 = (row == e2).astype(F32)
    sel = hot1 + hot2
    before = _dot(sel.astype(BF16), tri_ref[...]) + run_ref[:, 0:1]
    rank1 = jnp.sum(hot1 * before, axis=0, keepdims=True)
    rank2 = jnp.sum(hot2 * before, axis=0, keepdims=True)
    unused = jnp.zeros((ROUTE_ROWS - 6, TM), F32)
    route_ref[...] = jnp.concatenate([e1.astype(F32), e2.astype(F32), rank1, rank2, w1, w2, unused], axis=0)
    total = run_ref[:, 0:1] + jnp.sum(sel, axis=1, keepdims=True)
    run_ref[...] = jnp.broadcast_to(total, run_ref.shape)
    cnt_ref[...] = jnp.broadcast_to(total, cnt_ref.shape)


def _odd_out(x, att, sg, wo, g, wr, tri):
    t = x.shape[0]
    row = lambda w: pl.BlockSpec((TM, w), lambda i: (i, 0))
    return pl.pallas_call(
        _odd_out_body,
        grid=(t // TM,),
        in_specs=[row(D_MODEL), row(CA_WIDTH), row(SGU_WIDTH), _const_spec(wo.shape), _const_spec(g.shape),
                  _const_spec(wr.shape), _const_spec(tri.shape)],
        out_specs=[row(D_MODEL), pl.BlockSpec((TM * TOKEN_TILE_ROWS, 128), lambda i: (i, 0)),
                   pl.BlockSpec((ROUTE_ROWS, TM), lambda i: (0, i)),
                   pl.BlockSpec((N_EXPERTS, ROUTE_LANES), lambda i: (0, 0))],
        out_shape=[jax.ShapeDtypeStruct((t, D_MODEL), F32), jax.ShapeDtypeStruct((t * TOKEN_TILE_ROWS, 128), F32),
                   jax.ShapeDtypeStruct((ROUTE_ROWS, t), F32), jax.ShapeDtypeStruct((N_EXPERTS, ROUTE_LANES), F32)],
        scratch_shapes=[pltpu.VMEM((N_EXPERTS, ROUTE_LANES), F32)],
        compiler_params=_cparams(("arbitrary",)),
        name="odd_out_router",
    )(x, att, sg, wo, g, wr, tri)


def _moe_body(tile_e_ref, n_used_ref, src_ref, dst_ref, h_hbm, wg_ref, wu_ref, wd_ref, o_hbm,
              xf_ref, xb_ref, acc_ref, y_ref, gsem, ssem):
    i = pl.program_id(0)
    j = pl.program_id(1)
    n_used = n_used_ref[0]
    rows_per_step = MOE_TM // MOE_FF_STEPS

    ttr = TOKEN_TILE_ROWS
    tile_rows = MOE_TM * ttr

    def token_rows(ref, first_row):
        return ref.at[pl.ds(pl.multiple_of(first_row, ttr), ttr), :]

    def gather_copy(tile, r, slot):
        return pltpu.make_async_copy(token_rows(h_hbm, src_ref[tile * MOE_TM + r]),
                                     token_rows(xf_ref.at[slot], r * ttr), gsem)

    def scatter_copy(tile, r, slot):
        return pltpu.make_async_copy(token_rows(y_ref.at[slot], r * ttr),
                                     token_rows(o_hbm, dst_ref[tile * MOE_TM + r]), ssem)

    def wait_tile_gather(slot):
        pltpu.make_async_copy(h_hbm.at[pl.ds(0, tile_rows), :], xf_ref.at[slot], gsem).wait()

    def wait_tile_scatter(slot):
        pltpu.make_async_copy(y_ref.at[slot], o_hbm.at[pl.ds(0, tile_rows), :], ssem).wait()

    def start_all(copy_fn):
        lax.fori_loop(0, MOE_TM, lambda r, c: (copy_fn(r).start(), c)[1], 0)

    @pl.when(i < n_used)
    def _():
        cur = lax.rem(i, 2)
        other = 1 - cur
        nxt_tile = jnp.minimum(i + 1, n_used - 1)
        prev_tile = jnp.maximum(i - 1, 0)

        @pl.when(jnp.logical_and(i == 0, j == 0))
        def _():
            y_ref[...] = jnp.zeros_like(y_ref)
            spare0 = o_hbm.shape[0] - N_EXPERTS * tile_rows
            for e_i in range(N_EXPERTS):
                fill = pltpu.make_async_copy(y_ref.at[0], o_hbm.at[pl.ds(spare0 + e_i * tile_rows, tile_rows), :],
                                             ssem)
                fill.start()
                fill.wait()
            start_all(lambda r: gather_copy(0, r, 0))

        @pl.when(j == 0)
        def _():
            wait_tile_gather(cur)
            xb_ref[...] = _load_token_tiles(xf_ref.at[cur], MOE_TM).astype(BF16)
            acc_ref[...] = jnp.zeros_like(acc_ref)

        base = pl.multiple_of(j * rows_per_step, 8)
        for rr in range(rows_per_step):
            gather_copy(nxt_tile, base + rr, other).start()
            scatter_copy(prev_tile, base + rr, other).start()
        xb = xb_ref[...]
        gate = _dot(xb, wg_ref[...])
        up = _dot(xb, wu_ref[...])
        act = (gate * jax.nn.sigmoid(gate) * up).astype(BF16)
        acc_ref[...] += _dot(act, wd_ref[...])

        @pl.when(j == MOE_FF_STEPS - 1)
        def _():
            wait_tile_scatter(other)
            _store_token_tiles(y_ref.at[cur], acc_ref[...])

        @pl.when(jnp.logical_and(i == n_used - 1, j == MOE_FF_STEPS - 1))
        def _():
            start_all(lambda r: scatter_copy(i, r, cur))
            wait_tile_scatter(cur)
            wait_tile_gather(other)


def _moe(h, tile_e, n_used, src, dst, wg, wu, wd, layer):
    tokens = h.shape[0] // TOKEN_TILE_ROWS
    n_tiles = tile_e.shape[0]
    last = MOE_FF_STEPS - 1

    def ff_idx(i, j, n_used_ref):
        return jnp.where(i < n_used_ref[0], j, last)

    grid_spec = pltpu.PrefetchScalarGridSpec(
        num_scalar_prefetch=4,
        grid=(n_tiles, MOE_FF_STEPS),
        in_specs=[pl.BlockSpec(memory_space=pl.ANY),
                  pl.BlockSpec((None, None, D_MODEL, MOE_TF),
                               lambda i, j, te, nu, sr, ds: (layer, te[i], 0, ff_idx(i, j, nu))),
                  pl.BlockSpec((None, None, D_MODEL, MOE_TF),
                               lambda i, j, te, nu, sr, ds: (layer, te[i], 0, ff_idx(i, j, nu))),
                  pl.BlockSpec((None, None, MOE_TF, D_MODEL),
                               lambda i, j, te, nu, sr, ds: (layer, te[i], ff_idx(i, j, nu), 0))],
        out_specs=pl.BlockSpec(memory_space=pl.ANY),
        scratch_shapes=[pltpu.VMEM((2, MOE_TM * TOKEN_TILE_ROWS, 128), F32), pltpu.VMEM((MOE_TM, D_MODEL), BF16),
                        pltpu.VMEM((MOE_TM, D_MODEL), F32), pltpu.VMEM((2, MOE_TM * TOKEN_TILE_ROWS, 128), F32),
                        pltpu.SemaphoreType.DMA, pltpu.SemaphoreType.DMA],
    )
    return pl.pallas_call(
        _moe_body,
        grid_spec=grid_spec,
        out_shape=jax.ShapeDtypeStruct(((2 * tokens + N_EXPERTS * MOE_TM) * TOKEN_TILE_ROWS, 128), F32),
        compiler_params=_cparams(("arbitrary", "arbitrary")),
        name="moe_experts",
    )(tile_e, n_used, src, dst, h, wg, wu, wd)


def _row_order_body(pos_ref, cnt_ref, tile_end_ref, dst_ref, *, tokens):
    spare0 = 2 * tokens
    first_tile = 0
    for e_i in range(N_EXPERTS):
        end_tile = tile_end_ref[e_i]
        end_row = end_tile * MOE_TM
        spare_base = spare0 + e_i * MOE_TM - (end_row - MOE_TM)

        def pad(p, c, spare_base=spare_base):
            dst_ref[p] = spare_base + p
            return c

        lax.fori_loop(first_tile * MOE_TM + cnt_ref[e_i], end_row, pad, 0)
        first_tile = end_tile

    def unused(p, c):
        dst_ref[p] = spare0
        return c

    lax.fori_loop(first_tile * MOE_TM, dst_ref.shape[0], unused, 0)

    unroll = 8

    def place(block, c):
        for u in range(unroll):
            a = block * unroll + u
            dst_ref[pos_ref[a]] = a
        return c

    lax.fori_loop(0, 2 * tokens // unroll, place, 0)


def _row_order(pos, cnt, tile_end, tokens, n_tiles):
    smem = pl.BlockSpec(memory_space=pltpu.SMEM)
    return pl.pallas_call(
        functools.partial(_row_order_body, tokens=tokens),
        in_specs=[smem, smem, smem],
        out_specs=smem,
        out_shape=jax.ShapeDtypeStruct((n_tiles * MOE_TM,), I32),
        name="moe_row_order",
    )(pos, cnt, tile_end)


def _moe_plan(route, counts, tokens):
    n_tiles = (2 * tokens + N_EXPERTS * (MOE_TM - 1)) // MOE_TM
    experts = jnp.arange(N_EXPERTS, dtype=I32)
    cnt = counts[:, 0].astype(I32)
    e = route[0:2].astype(I32)
    rank = route[2:4].astype(I32)
    tiles_per_e = (cnt + MOE_TM - 1) // MOE_TM
    tile_end = jnp.cumsum(tiles_per_e)
    tile_start = tile_end - tiles_per_e
    row_start = jnp.sum(jnp.where(e[..., None] == experts, tile_start * MOE_TM, 0), axis=-1)
    pos = row_start + rank
    n_used = tile_end[-1]
    tile = jnp.minimum(jnp.arange(n_tiles, dtype=I32), n_used - 1)
    tile_e = jnp.sum((tile_end[None, :] <= tile[:, None]).astype(I32), axis=1)
    dst = _row_order(pos.reshape(-1), cnt, tile_end, tokens, n_tiles)
    src = jnp.where(dst < 2 * tokens, dst % tokens, 0)
    return tile_e, n_used.reshape(1), src * TOKEN_TILE_ROWS, dst * TOKEN_TILE_ROWS


def _combine_body(x_ref, oa_ref, ob_ref, gate_ref, g_ref, o_ref, *, final):
    w1 = gate_ref[:, 0:1]
    w2 = gate_ref[:, 1:2]
    y = x_ref[...] + (w1 * _load_token_tiles(oa_ref, TM) + w2 * _load_token_tiles(ob_ref, TM))
    o_ref[...] = _rms(y, g_ref[...]) if final else y


def _combine(x, o2, route, g, final):
    t = x.shape[0]
    nt = t // TM
    gates = route[4:6].T
    return pl.pallas_call(
        functools.partial(_combine_body, final=final),
        grid=(nt,),
        in_specs=[pl.BlockSpec((TM, D_MODEL), lambda i: (i, 0)),
                  pl.BlockSpec((TM * TOKEN_TILE_ROWS, 128), lambda i: (i, 0)),
                  pl.BlockSpec((TM * TOKEN_TILE_ROWS, 128), lambda i: (i + nt, 0)),
                  pl.BlockSpec((TM, 2), lambda i: (i, 0)), _const_spec(g.shape)],
        out_specs=pl.BlockSpec((TM, D_MODEL), lambda i: (i, 0)),
        out_shape=jax.ShapeDtypeStruct((t, D_MODEL), F32),
        compiler_params=_cparams(("parallel",)),
        name="moe_combine",
    )(x, o2, o2, gates, g)


def _rot_half(w):
    half = w.shape[-1] // 2
    return jnp.concatenate([-w[..., half:], w[..., :half]], axis=-1)


def _rope_tables(seq):
    inv = 1.0 / (ROPE_THETA ** (jnp.arange(0, MLA_ROPE, 2, dtype=F32) / MLA_ROPE))
    ang = jnp.arange(seq, dtype=F32)[:, None] * inv[None, :]
    cos, sin = jnp.cos(ang), jnp.sin(ang)
    pad = HEAD_PAD - MLA_NOPE - MLA_ROPE
    c = jnp.concatenate([jnp.ones((seq, MLA_NOPE), F32), cos, cos, jnp.zeros((seq, pad), F32)], axis=1)
    s = jnp.concatenate([jnp.zeros((seq, MLA_NOPE), F32), sin, sin, jnp.zeros((seq, pad), F32)], axis=1)
    scale = (MLA_NOPE + MLA_ROPE) ** -0.5 * math.log2(math.e)
    return c * scale, s * scale, c, s


def _even_weights(w_in, w_q_up, w_kv_up):
    d = w_in.shape[0]
    pad = HEAD_PAD - MLA_NOPE - MLA_ROPE
    kr = w_in[:, POOL_WIDTH + MLA_Q_RANK + MLA_KV_RANK:]
    place = lambda m: jnp.concatenate([jnp.zeros((d, MLA_NOPE), F32), m, jnp.zeros((d, pad), F32)], axis=1)
    win = jnp.concatenate([w_in[:, :POOL_WIDTH + MLA_Q_RANK + MLA_KV_RANK], place(kr), place(_rot_half(kr))], axis=1)
    wq = w_q_up.reshape(MLA_Q_RANK, MLA_HEADS, MLA_NOPE + MLA_ROPE)
    nope, rope = wq[..., :MLA_NOPE], wq[..., MLA_NOPE:]
    zq = lambda n: jnp.zeros((MLA_Q_RANK, MLA_HEADS, n), F32)
    wq_main = jnp.concatenate([nope, rope, zq(pad)], axis=-1).reshape(MLA_Q_RANK, -1)
    wq_rot = jnp.concatenate([zq(MLA_NOPE), _rot_half(rope), zq(pad)], axis=-1).reshape(MLA_Q_RANK, -1)
    wkv = w_kv_up.reshape(MLA_KV_RANK, MLA_HEADS, MLA_NOPE + MLA_V)
    k_pad = jnp.concatenate([wkv[..., :MLA_NOPE], jnp.zeros((MLA_KV_RANK, MLA_HEADS, HEAD_PAD - MLA_NOPE), F32)],
                            axis=-1).reshape(MLA_KV_RANK, -1)
    v_pairs = wkv[..., MLA_NOPE:].reshape(MLA_KV_RANK, MLA_HEADS // 2, 2, MLA_V)
    v_gap = jnp.zeros((MLA_KV_RANK, MLA_HEADS // 2, 2 * (HEAD_PAD - MLA_V)), F32)
    v_ext = jnp.concatenate([v_pairs[:, :, 0], v_gap, v_pairs[:, :, 1]], axis=-1).reshape(MLA_KV_RANK, -1)
    return (win.astype(BF16), jnp.concatenate([wq_main, wq_rot], axis=1).astype(BF16),
            jnp.concatenate([k_pad, v_ext], axis=1).astype(BF16))


def _v_ones():
    lane = jnp.arange(MLA_HEADS * HEAD_PAD) % (2 * HEAD_PAD)
    return jnp.logical_and(lane >= MLA_V, lane < 2 * HEAD_PAD - MLA_V).astype(F32).reshape(1, -1)


def _band_bias(rel_bias):
    band = BAND_BLOCKS * TQ
    qpos = jnp.arange(TQ)[:, None] + (BAND_BLOCKS - 1) * TQ
    kpos = jnp.arange(band)[None, :]
    span = band + TQ - 1
    rel = (BAND_BLOCKS - 1) * TQ + (TQ - 1) - jnp.arange(span)
    by_rel = rel_bias[:, jnp.clip(rel, -REL_CLIP, REL_CLIP) + REL_CLIP].astype(F32)
    rolled = jnp.roll(by_rel, -(TQ - 1), axis=1)
    bias = jnp.tile(rolled, (1, TQ))[:, :TQ * (span - 1)].reshape(-1, TQ, span - 1)[:, :, :band]
    dc = qpos // CHUNK - kpos // CHUNK
    valid = jnp.logical_and(dc >= 0, dc <= CA_LEFT_CHUNKS)
    started = kpos // TQ >= (BAND_BLOCKS - 1) - jnp.arange(BAND_BLOCKS)[:, None, None]
    ok = jnp.logical_and(valid[None], started)[:, None]
    return jnp.where(ok, bias[None] * math.log2(math.e), NEG)


def kernel(x, ev_norm_mix, ev_w_in, ev_pool_w, ev_pool_scale, ev_q_norm, ev_w_q_up, ev_kv_norm, ev_w_kv_up, ev_w_out, ev_norm_ffn, ev_ffn_w_gate, ev_ffn_w_up, ev_ffn_w_down, od_norm_mix, od_w_in, od_rel_bias, od_sgu_norm_g, od_sgu_norm_b, od_sgu_w, od_sgu_b, od_w_out, od_norm_ffn, od_router, od_moe_w_gate, od_moe_w_up, od_moe_w_down, final_norm):
    batch, seq, d = x.shape
    tokens = batch * seq
    assert d == D_MODEL and seq % TM == 0 and TM % SGU_CHUNK == 0 and TM % TQ == 0 and seq % TK == 0
    depth = ev_w_in.shape[0] + od_w_in.shape[0]
    xt = x.reshape(tokens, d)
    tabs = _rope_tables(seq)
    tri = jnp.triu(jnp.ones((TM, TM), F32), k=1).astype(BF16)
    causal = jnp.tril(jnp.ones((SGU_CHUNK, SGU_CHUNK), bool))
    vec = lambda v: v.reshape(1, -1).astype(F32)
    moe_wg, moe_wu, moe_wd = (w.astype(BF16) for w in (od_moe_w_gate, od_moe_w_up, od_moe_w_down))

    for layer in range(depth):
        i = layer // 2
        if layer % 2 == 0:
            win, wq, wkv = _even_weights(ev_w_in[i], ev_w_q_up[i], ev_w_kv_up[i])
            a, q, k, v = _even_in(xt, vec(ev_norm_mix[i]), win, vec(ev_q_norm[i]), wq, vec(ev_kv_norm[i]), wkv,
                                  _v_ones(), tabs, seq)
            att = _mla_attention(q, k, v, batch, seq)
            xt = _even_out(xt, a, att, ev_pool_w[i].astype(BF16), vec(ev_pool_scale[i]), ev_w_out[i].astype(BF16),
                           vec(ev_norm_ffn[i]), ev_ffn_w_gate[i].astype(BF16), ev_ffn_w_up[i].astype(BF16),
                           ev_ffn_w_down[i].astype(BF16), seq)
        else:
            w_in = od_w_in[i]
            win = jnp.concatenate([w_in[:, :CA_WIDTH] * (CA_HEAD_DIM ** -0.5), w_in[:, CA_WIDTH:]], axis=1).astype(BF16)
            sw = jnp.where(causal[None], od_sgu_w[i], 0.0).astype(BF16)
            sb = jnp.broadcast_to(od_sgu_b[i][:, :, None], (SGU_GROUPS, SGU_CHUNK, SGU_CH)).astype(F32)
            qkv, sg = _odd_in(xt, vec(od_norm_mix[i]), win, vec(od_sgu_norm_g[i]), vec(od_sgu_norm_b[i]), sw, sb)
            att = _band_attention(qkv, _band_bias(od_rel_bias[i]), batch, seq)
            wr = jnp.concatenate([od_router[i], jnp.zeros((d, ROUTE_LANES - N_EXPERTS), F32)], axis=1)
            wr_hi = wr.astype(BF16)
            wr = jnp.stack([wr_hi, (wr - wr_hi.astype(F32)).astype(BF16)])
            x1, h, route, counts = _odd_out(xt, att, sg, od_w_out[i].astype(BF16), vec(od_norm_ffn[i]), wr, tri)
            tile_e, n_used, src, dst = _moe_plan(route, counts, tokens)
            o2 = _moe(h, tile_e, n_used, src, dst, moe_wg, moe_wu, moe_wd, i)
            last = layer == depth - 1
            xt = _combine(x1, o2, route, vec(final_norm), last)
    return xt.reshape(batch, seq, d)
```

```python
import functools
import math

import jax
import jax.numpy as jnp
from jax import lax
from jax.experimental import pallas as pl
from jax.experimental.pallas import tpu as pltpu

F32 = jnp.float32
BF16 = jnp.bfloat16
I32 = jnp.int32

D_MODEL = 1024
CHUNK = 64
EPS = 1e-6

POOL_GROUPS = 4
POOL_CH = 128
POOL_WIDTH = 512
POOL_WINDOWS = (2, 4, 8, 16)
POOL_HALO = 16

MLA_HEADS = 8
MLA_NOPE = 64
MLA_ROPE = 32
MLA_V = 64
MLA_Q_RANK = 256
MLA_KV_RANK = 128
ROPE_THETA = 10000.0
HEAD_PAD = 128

CA_HEADS = 8
CA_HEAD_DIM = 64
CA_LEFT_CHUNKS = 8
REL_CLIP = 128
CA_WIDTH = CA_HEADS * CA_HEAD_DIM

SGU_GROUPS = 4
SGU_CH = 128
SGU_WIDTH = 512
SGU_CHUNK = 128

D_FF = 2816
N_EXPERTS = 8
EXPERT_FF = 3584

NEG = -1e30
LANES = 128

TM = 512
TQ = 256
TK = 512
MLA_TQ = 512
MOE_TF = 512
MOE_FF_STEPS = EXPERT_FF // MOE_TF
MOE_TM = 80 * MOE_FF_STEPS
VMEM_LIMIT = 56 * 1024 * 1024


def _cparams(sem):
    return pltpu.CompilerParams(dimension_semantics=sem, vmem_limit_bytes=VMEM_LIMIT)


def _rms(x, g):
    return x * lax.rsqrt(jnp.mean(x * x, axis=-1, keepdims=True) + EPS) * g


def _dot(a, b):
    return jnp.dot(a, b, preferred_element_type=F32)


def _dot_nt(a, b):
    return lax.dot_general(a, b, (((1,), (1,)), ((), ())), preferred_element_type=F32)


TOKEN_TILE_ROWS = D_MODEL // LANES


def _store_token_tiles(ref, x):
    n = x.shape[0]
    for s in range(TOKEN_TILE_ROWS):
        ref[pl.ds(s, n, stride=TOKEN_TILE_ROWS), :] = x[:, s * LANES:(s + 1) * LANES]


def _load_token_tiles(ref, n):
    return jnp.concatenate([ref[pl.ds(s, n, stride=TOKEN_TILE_ROWS), :] for s in range(TOKEN_TILE_ROWS)], axis=1)


def _const_spec(shape):
    nd = len(shape)
    return pl.BlockSpec(shape, lambda *_: (0,) * nd, pipeline_mode=pl.Buffered(1))


def _even_in_body(x_ref, g_ref, win_ref, qn_ref, wq_ref, kvn_ref, wkv_ref, vones_ref, cq_ref, sq_ref, ck_ref, sk_ref,
                  a_ref, q_ref, k_ref, v_ref):
    h = _rms(x_ref[...], g_ref[...]).astype(BF16)
    z = _dot(h, win_ref[...])
    a_ref[...] = z[:, :POOL_WIDTH]
    o = POOL_WIDTH
    cq = z[:, o:o + MLA_Q_RANK]
    o += MLA_Q_RANK
    ckv = z[:, o:o + MLA_KV_RANK]
    o += MLA_KV_RANK
    kr = z[:, o:o + HEAD_PAD]
    kr_rot = z[:, o + HEAD_PAD:o + 2 * HEAD_PAD]

    hw = MLA_HEADS * HEAD_PAD
    qq = _dot(_rms(cq, qn_ref[...]).astype(BF16), wq_ref[...])
    cq_t, sq_t = cq_ref[...], sq_ref[...]
    for h_i in range(MLA_HEADS):
        sl = slice(h_i * HEAD_PAD, (h_i + 1) * HEAD_PAD)
        sl_rot = slice(hw + h_i * HEAD_PAD, hw + (h_i + 1) * HEAD_PAD)
        q_ref[:, sl] = (qq[:, sl] * cq_t + qq[:, sl_rot] * sq_t).astype(BF16)

    kv = _dot(_rms(ckv, kvn_ref[...]).astype(BF16), wkv_ref[...])
    k_rope = kr * ck_ref[...] + kr_rot * sk_ref[...]
    for h_i in range(MLA_HEADS):
        sl = slice(h_i * HEAD_PAD, (h_i + 1) * HEAD_PAD)
        k_ref[:, sl] = (kv[:, sl] + k_rope).astype(BF16)
    v_ref[...] = (kv[:, hw:] + vones_ref[...]).astype(BF16)


def _even_in(x, g, win, qn, wq, kvn, wkv, vones, tabs, seq):
    t = x.shape[0]
    hw = MLA_HEADS * HEAD_PAD
    row = lambda w: pl.BlockSpec((TM, w), lambda i: (i, 0))
    tab = pl.BlockSpec((TM, HEAD_PAD), lambda i: (i % (seq // TM), 0))
    return pl.pallas_call(
        _even_in_body,
        grid=(t // TM,),
        in_specs=[row(D_MODEL), _const_spec(g.shape), _const_spec(win.shape), _const_spec(qn.shape),
                  _const_spec(wq.shape), _const_spec(kvn.shape), _const_spec(wkv.shape), _const_spec(vones.shape),
                  tab, tab, tab, tab],
        out_specs=[row(POOL_WIDTH), row(hw), row(hw), row(hw)],
        out_shape=[jax.ShapeDtypeStruct((t, POOL_WIDTH), F32), jax.ShapeDtypeStruct((t, hw), BF16),
                   jax.ShapeDtypeStruct((t, hw), BF16), jax.ShapeDtypeStruct((t, hw), BF16)],
        compiler_params=_cparams(("parallel",)),
        name="even_in",
    )(x, g, win, qn, wq, kvn, wkv, vones, *tabs)


def _mla_body(q_ref, k_ref, v_ref, o_ref, s_ref, mx_ref, acc_ref):
    qi = pl.program_id(2)
    last = (qi * MLA_TQ) // TK
    halves = (slice(0, HEAD_PAD), slice(HEAD_PAD, 2 * HEAD_PAD))
    lane_blocks = [slice(c * LANES, (c + 1) * LANES) for c in range(TK // LANES)]
    q_chunk = lax.broadcasted_iota(I32, (MLA_TQ, TK), 0) // CHUNK + qi * (MLA_TQ // CHUNK)
    k_chunk = lax.broadcasted_iota(I32, (MLA_TQ, TK), 1) // CHUNK + last * (TK // CHUNK)
    visible = k_chunk <= q_chunk

    mx_ref[...] = jnp.full(mx_ref.shape, NEG, F32)
    acc_ref[...] = jnp.zeros_like(acc_ref)

    def scores(j, masked):
        rows = pl.ds(pl.multiple_of(j * TK, TK), TK)
        for hh, hs in enumerate(halves):
            s = _dot_nt(q_ref[:, hs], k_ref[rows, hs])
            if masked:
                s = jnp.where(visible, s, NEG)
            s_ref[hh, j] = s
            mx_ref[hh] = functools.reduce(jnp.maximum, [s[:, lb] for lb in lane_blocks], mx_ref[hh])

    def pairs_then_rest(count, body):
        def two(jj, c):
            body(2 * jj)
            body(2 * jj + 1)
            return c

        lax.fori_loop(0, count // 2, two, 0)

        @pl.when(count % 2 == 1)
        def _():
            body(count - 1)

    pairs_then_rest(last, lambda j: scores(j, False))
    scores(last, True)

    for hh in range(2):
        m = jnp.max(mx_ref[hh], axis=-1, keepdims=True)
        mx_ref[hh] = jnp.broadcast_to(m, (MLA_TQ, LANES))

    def weighted(j):
        rows = pl.ds(pl.multiple_of(j * TK, TK), TK)
        for hh, hs in enumerate(halves):
            m = mx_ref[hh]
            p = jnp.concatenate([jnp.exp2(s_ref[hh, j, :, lb] - m) for lb in lane_blocks], axis=1)
            acc_ref[hh] += _dot(p.astype(BF16), v_ref[rows, hs])

    pairs_then_rest(last + 1, weighted)

    even, odd = acc_ref[0], acc_ref[1]
    lane = lax.broadcasted_iota(I32, (MLA_TQ, 2 * MLA_V), 1)
    out = jnp.where(lane < MLA_V, even / even[:, MLA_V:MLA_V + 1], odd / odd[:, 0:1])
    o_ref[...] = out.astype(BF16)


def _mla_attention(q, k, v, batch, seq):
    t = q.shape[0]
    nq = seq // MLA_TQ
    pairs = MLA_HEADS // 2
    return pl.pallas_call(
        _mla_body,
        grid=(batch, pairs, nq),
        in_specs=[pl.BlockSpec((MLA_TQ, 2 * HEAD_PAD), lambda b, p, i: (b * nq + i, p)),
                  pl.BlockSpec((seq, 2 * HEAD_PAD), lambda b, p, i: (b, p)),
                  pl.BlockSpec((seq, 2 * HEAD_PAD), lambda b, p, i: (b, p))],
        out_specs=pl.BlockSpec((MLA_TQ, 2 * MLA_V), lambda b, p, i: (b * nq + i, p)),
        out_shape=jax.ShapeDtypeStruct((t, MLA_HEADS * MLA_V), BF16),
        scratch_shapes=[pltpu.VMEM((2, seq // TK, MLA_TQ, TK), F32), pltpu.VMEM((2, MLA_TQ, LANES), F32),
                        pltpu.VMEM((2, MLA_TQ, HEAD_PAD), F32)],
        compiler_params=_cparams(("parallel", "parallel", "arbitrary")),
        name="mla_attention",
    )(q, k, v)


def _even_out_body(x_ref, a_ref, halo_ref, att_ref, pw_ref, ps_ref, wo_ref, g_ref, wg_ref, wu_ref, wd_ref,
                   o_ref, ext_ref, *, seq):
    pos0 = (pl.program_id(0) * TM) % seq
    ext_ref[0:POOL_HALO, :] = jnp.where(pos0 == 0, 0.0, halo_ref[...])
    ext_ref[POOL_HALO:, :] = a_ref[...]
    pos = lax.broadcasted_iota(I32, (TM, 1), 0) + pos0
    y = x_ref[...] + _dot(att_ref[...], wo_ref[POOL_WIDTH:, :])
    for g_i, win in enumerate(POOL_WINDOWS):
        gs = slice(g_i * POOL_CH, (g_i + 1) * POOL_CH)
        a_g = ext_ref[POOL_HALO:, gs]
        ssum = a_g
        for j in range(1, win):
            ssum = ssum + ext_ref[POOL_HALO - j:POOL_HALO - j + TM, gs]
        count = jnp.minimum(pos + 1, win).astype(F32)
        pooled = (ssum / count - a_g).astype(BF16)
        mixed = (_dot(pooled, pw_ref[g_i]) * ps_ref[:, gs]).astype(BF16)
        y = y + _dot(mixed, wo_ref[gs, :])
    h = _rms(y, g_ref[...]).astype(BF16)
    gate = _dot(h, wg_ref[...])
    up = _dot(h, wu_ref[...])
    act = (gate * jax.nn.sigmoid(gate) * up).astype(BF16)
    o_ref[...] = y + _dot(act, wd_ref[...])


def _even_out(x, a, att, pw, ps, wo, g, wg, wu, wd, seq):
    t = x.shape[0]
    row = lambda w: pl.BlockSpec((TM, w), lambda i: (i, 0))
    halo = pl.BlockSpec((POOL_HALO, POOL_WIDTH), lambda i: (jnp.maximum(i * (TM // POOL_HALO) - 1, 0), 0))
    return pl.pallas_call(
        functools.partial(_even_out_body, seq=seq),
        grid=(t // TM,),
        in_specs=[row(D_MODEL), row(POOL_WIDTH), halo, row(MLA_HEADS * MLA_V), _const_spec(pw.shape),
                  _const_spec(ps.shape), _const_spec(wo.shape), _const_spec(g.shape), _const_spec(wg.shape),
                  _const_spec(wu.shape), _const_spec(wd.shape)],
        out_specs=row(D_MODEL),
        out_shape=jax.ShapeDtypeStruct((t, D_MODEL), F32),
        scratch_shapes=[pltpu.VMEM((TM + POOL_HALO, POOL_WIDTH), F32)],
        compiler_params=_cparams(("parallel",)),
        name="even_out",
    )(x, a, a, att, pw, ps, wo, g, wg, wu, wd)


def _odd_in_body(x_ref, g_ref, win_ref, ng_ref, nb_ref, sw_ref, sb_ref, qkv_ref, sg_ref):
    h = _rms(x_ref[...], g_ref[...]).astype(BF16)
    z = _dot(h, win_ref[...])
    qkv_ref[:, :CA_WIDTH] = (z[:, :CA_WIDTH] * math.log2(math.e)).astype(BF16)
    qkv_ref[:, CA_WIDTH:] = z[:, CA_WIDTH:3 * CA_WIDTH].astype(BF16)
    uv = jax.nn.gelu(z[:, 3 * CA_WIDTH:])
    for g_i in range(SGU_GROUPS):
        gs = slice(g_i * SGU_CH, (g_i + 1) * SGU_CH)
        u = uv[:, gs]
        gv = uv[:, SGU_WIDTH + g_i * SGU_CH:SGU_WIDTH + (g_i + 1) * SGU_CH]
        mu = jnp.mean(gv, axis=-1, keepdims=True)
        xc = gv - mu
        vn = xc * lax.rsqrt(jnp.mean(xc * xc, axis=-1, keepdims=True) + EPS) * ng_ref[:, gs] + nb_ref[:, gs]
        vn = vn.astype(BF16)
        w = sw_ref[g_i]
        for c_i in range(TM // SGU_CHUNK):
            rs = slice(c_i * SGU_CHUNK, (c_i + 1) * SGU_CHUNK)
            mixed = _dot(w, vn[rs, :]) + sb_ref[g_i]
            sg_ref[rs, gs] = (u[rs, :] * mixed).astype(BF16)


def _odd_in(x, g, win, ng, nb, sw, sb):
    t = x.shape[0]
    row = lambda w: pl.BlockSpec((TM, w), lambda i: (i, 0))
    return pl.pallas_call(
        _odd_in_body,
        grid=(t // TM,),
        in_specs=[row(D_MODEL), _const_spec(g.shape), _const_spec(win.shape), _const_spec(ng.shape),
                  _const_spec(nb.shape), _const_spec(sw.shape), _const_spec(sb.shape)],
        out_specs=[row(3 * CA_WIDTH), row(SGU_WIDTH)],
        out_shape=[jax.ShapeDtypeStruct((t, 3 * CA_WIDTH), BF16), jax.ShapeDtypeStruct((t, SGU_WIDTH), BF16)],
        compiler_params=_cparams(("parallel",)),
        name="odd_in",
    )(x, g, win, ng, nb, sw, sb)


BAND_BLOCKS = 3


def _band_body(q_ref, k0_ref, k1_ref, k2_ref, v0_ref, v1_ref, v2_ref, bias_ref, o_ref):
    pair_w = 2 * CA_HEAD_DIM
    low = lax.broadcasted_iota(I32, (TQ, pair_w), 1) < CA_HEAD_DIM
    k_refs = (k0_ref, k1_ref, k2_ref)
    v_refs = (v0_ref, v1_ref, v2_ref)
    add = lambda a, b: a + b
    per_tile = TQ // LANES
    for pair in range(CA_HEADS // 2):
        ps = slice(pair * pair_w, (pair + 1) * pair_w)
        q_pair = q_ref[:, ps]
        outs = []
        for hh in range(2):
            q = jnp.where(low if hh == 0 else jnp.logical_not(low), q_pair, jnp.zeros_like(q_pair))
            s = []
            for jj in range(BAND_BLOCKS):
                sj = _dot_nt(q, k_refs[jj][:, ps]) + bias_ref[2 * pair + hh, :, jj * TQ:(jj + 1) * TQ]
                s += [sj[:, c * LANES:(c + 1) * LANES] for c in range(per_tile)]
            m = jnp.max(functools.reduce(jnp.maximum, s), axis=-1, keepdims=True)
            mb = jnp.broadcast_to(m, (TQ, LANES))
            p = [jnp.exp2(x - mb) for x in s]
            l = jnp.sum(functools.reduce(add, p), axis=-1, keepdims=True)
            acc = functools.reduce(add, [
                _dot(jnp.concatenate(p[jj * per_tile:(jj + 1) * per_tile], axis=1).astype(BF16), v_refs[jj][:, ps])
                for jj in range(BAND_BLOCKS)])
            outs.append(acc / l)
        o_ref[:, ps] = jnp.where(low, outs[0], outs[1]).astype(BF16)


def _band_attention(qkv, bias, batch, seq):
    t = qkv.shape[0]
    nq = seq // TQ

    def kv_spec(jj, col):
        return pl.BlockSpec((TQ, CA_WIDTH), lambda b, i: (b * nq + jnp.maximum(i - (BAND_BLOCKS - 1) + jj, 0), col))

    return pl.pallas_call(
        _band_body,
        grid=(batch, nq),
        in_specs=[pl.BlockSpec((TQ, CA_WIDTH), lambda b, i: (b * nq + i, 0))]
                 + [kv_spec(jj, 1) for jj in range(BAND_BLOCKS)]
                 + [kv_spec(jj, 2) for jj in range(BAND_BLOCKS)]
                 + [pl.BlockSpec((None, CA_HEADS, TQ, BAND_BLOCKS * TQ),
                                 lambda b, i: (jnp.minimum(i, BAND_BLOCKS - 1), 0, 0, 0))],
        out_specs=pl.BlockSpec((TQ, CA_WIDTH), lambda b, i: (b * nq + i, 0)),
        out_shape=jax.ShapeDtypeStruct((t, CA_WIDTH), BF16),
        compiler_params=_cparams(("parallel", "arbitrary")),
        name="band_attention",
    )(qkv, qkv, qkv, qkv, qkv, qkv, qkv, bias)


ROUTE_LANES = 128
ROUTE_ROWS = 8


def _odd_out_body(x_ref, att_ref, sg_ref, wo_ref, g_ref, wr_ref, tri_ref, x1_ref, h_ref, route_ref, cnt_ref, run_ref):
    @pl.when(pl.program_id(0) == 0)
    def _():
        run_ref[...] = jnp.zeros_like(run_ref)

    y = x_ref[...] + _dot(att_ref[...], wo_ref[:CA_WIDTH, :]) + _dot(sg_ref[...], wo_ref[CA_WIDTH:, :])
    x1_ref[...] = y
    h = _rms(y, g_ref[...])
    _store_token_tiles(h_ref, h)
    h_hi = h.astype(BF16)
    h_lo = (h - h_hi.astype(F32)).astype(BF16)
    logits = _dot(h_hi, wr_ref[0]) + (_dot(h_lo, wr_ref[0]) + _dot(h_hi, wr_ref[1]))
    lt = logits.T[:N_EXPERTS]
    row = lax.broadcasted_iota(I32, (N_EXPERTS, TM), 0)
    m1 = jnp.max(lt, axis=0, keepdims=True)
    e1 = jnp.min(jnp.where(lt == m1, row, N_EXPERTS), axis=0, keepdims=True)
    rest = jnp.where(row == e1, NEG, lt)
    m2 = jnp.max(rest, axis=0, keepdims=True)
    e2 = jnp.min(jnp.where(rest == m2, row, N_EXPERTS), axis=0, keepdims=True)
    ex = jnp.exp(m2 - m1)
    w1 = 1.0 / (1.0 + ex)
    w2 = ex / (1.0 + ex)
    hot1 = (row == e1).astype(F32)
    hot2 = (row == e2).astype(F32)
    sel = hot1 + hot2
    before = _dot(sel.astype(BF16), tri_ref[...]) + run_ref[:, 0:1]
    rank1 = jnp.sum(hot1 * before, axis=0, keepdims=True)
    rank2 = jnp.sum(hot2 * before, axis=0, keepdims=True)
    unused = jnp.zeros((ROUTE_ROWS - 6, TM), F32)
    route_ref[...] = jnp.concatenate([e1.astype(F32), e2.astype(F32), rank1, rank2, w1, w2, unused], axis=0)
    total = run_ref[:, 0:1] + jnp.sum(sel, axis=1, keepdims=True)
    run_ref[...] = jnp.broadcast_to(total, run_ref.shape)
    cnt_ref[...] = jnp.broadcast_to(total, cnt_ref.shape)


def _odd_out(x, att, sg, wo, g, wr, tri):
    t = x.shape[0]
    row = lambda w: pl.BlockSpec((TM, w), lambda i: (i, 0))
    return pl.pallas_call(
        _odd_out_body,
        grid=(t // TM,),
        in_specs=[row(D_MODEL), row(CA_WIDTH), row(SGU_WIDTH), _const_spec(wo.shape), _const_spec(g.shape),
                  _const_spec(wr.shape), _const_spec(tri.shape)],
        out_specs=[row(D_MODEL), pl.BlockSpec((TM * TOKEN_TILE_ROWS, LANES), lambda i: (i, 0)),
                   pl.BlockSpec((ROUTE_ROWS, TM), lambda i: (0, i)),
                   pl.BlockSpec((N_EXPERTS, ROUTE_LANES), lambda i: (0, 0))],
        out_shape=[jax.ShapeDtypeStruct((t, D_MODEL), F32), jax.ShapeDtypeStruct((t * TOKEN_TILE_ROWS, LANES), F32),
                   jax.ShapeDtypeStruct((ROUTE_ROWS, t), F32), jax.ShapeDtypeStruct((N_EXPERTS, ROUTE_LANES), F32)],
        scratch_shapes=[pltpu.VMEM((N_EXPERTS, ROUTE_LANES), F32)],
        compiler_params=_cparams(("arbitrary",)),
        name="odd_out_router",
    )(x, att, sg, wo, g, wr, tri)


def _moe_body(tile_e_ref, n_used_ref, src_ref, dst_ref, h_hbm, wg_ref, wu_ref, wd_ref, o_hbm,
              xf_ref, xb_ref, acc_ref, y_ref, gsem, ssem):
    i = pl.program_id(0)
    j = pl.program_id(1)
    n_used = n_used_ref[0]
    rows_per_step = MOE_TM // MOE_FF_STEPS

    ttr = TOKEN_TILE_ROWS
    tile_rows = MOE_TM * ttr

    def token_rows(ref, first_row):
        return ref.at[pl.ds(pl.multiple_of(first_row, ttr), ttr), :]

    def gather_copy(tile, r, slot):
        return pltpu.make_async_copy(token_rows(h_hbm, src_ref[tile * MOE_TM + r]),
                                     token_rows(xf_ref.at[slot], r * ttr), gsem)

    def scatter_copy(tile, r, slot):
        return pltpu.make_async_copy(token_rows(y_ref.at[slot], r * ttr),
                                     token_rows(o_hbm, dst_ref[tile * MOE_TM + r]), ssem)

    def wait_tile_gather(slot):
        pltpu.make_async_copy(h_hbm.at[pl.ds(0, tile_rows), :], xf_ref.at[slot], gsem).wait()

    def wait_tile_scatter(slot):
        pltpu.make_async_copy(y_ref.at[slot], o_hbm.at[pl.ds(0, tile_rows), :], ssem).wait()

    def start_all(copy_fn):
        lax.fori_loop(0, MOE_TM, lambda r, c: (copy_fn(r).start(), c)[1], 0)

    @pl.when(i < n_used)
    def _():
        cur = lax.rem(i, 2)
        other = 1 - cur
        nxt_tile = jnp.minimum(i + 1, n_used - 1)
        prev_tile = jnp.maximum(i - 1, 0)

        @pl.when(jnp.logical_and(i == 0, j == 0))
        def _():
            y_ref[...] = jnp.zeros_like(y_ref)
            acc_ref[...] = jnp.zeros_like(acc_ref)
            spare0 = o_hbm.shape[0] - N_EXPERTS * tile_rows
            for e_i in range(N_EXPERTS):
                fill = pltpu.make_async_copy(y_ref.at[0], o_hbm.at[pl.ds(spare0 + e_i * tile_rows, tile_rows), :],
                                             ssem)
                fill.start()
                fill.wait()
            start_all(lambda r: gather_copy(0, r, 0))

        @pl.when(j == 0)
        def _():
            wait_tile_gather(cur)
            xb_ref[...] = _load_token_tiles(xf_ref.at[cur], MOE_TM).astype(BF16)

        base = pl.multiple_of(j * rows_per_step, 8)
        for rr in range(rows_per_step):
            gather_copy(nxt_tile, base + rr, other).start()
            scatter_copy(prev_tile, base + rr, other).start()
        xb = xb_ref[...]
        gate = _dot(xb, wg_ref[...])
        up = _dot(xb, wu_ref[...])
        act = (gate * jax.nn.sigmoid(gate) * up).astype(BF16)
        acc_ref[...] = jnp.where(j == 0, 0.0, acc_ref[...]) + _dot(act, wd_ref[...])

        @pl.when(j == MOE_FF_STEPS - 1)
        def _():
            wait_tile_scatter(other)
            _store_token_tiles(y_ref.at[cur], acc_ref[...])

        @pl.when(jnp.logical_and(i == n_used - 1, j == MOE_FF_STEPS - 1))
        def _():
            start_all(lambda r: scatter_copy(i, r, cur))
            wait_tile_scatter(cur)
            wait_tile_gather(other)


def _moe(h, tile_e, n_used, src, dst, wg, wu, wd, layer):
    tokens = h.shape[0] // TOKEN_TILE_ROWS
    n_tiles = tile_e.shape[0]
    last = MOE_FF_STEPS - 1

    def ff_idx(i, j, n_used_ref):
        return jnp.where(i < n_used_ref[0], j, last)

    grid_spec = pltpu.PrefetchScalarGridSpec(
        num_scalar_prefetch=4,
        grid=(n_tiles, MOE_FF_STEPS),
        in_specs=[pl.BlockSpec(memory_space=pl.ANY),
                  pl.BlockSpec((None, None, D_MODEL, MOE_TF),
                               lambda i, j, te, nu, sr, ds: (layer, te[i], 0, ff_idx(i, j, nu))),
                  pl.BlockSpec((None, None, D_MODEL, MOE_TF),
                               lambda i, j, te, nu, sr, ds: (layer, te[i], 0, ff_idx(i, j, nu))),
                  pl.BlockSpec((None, None, MOE_TF, D_MODEL),
                               lambda i, j, te, nu, sr, ds: (layer, te[i], ff_idx(i, j, nu), 0))],
        out_specs=pl.BlockSpec(memory_space=pl.ANY),
        scratch_shapes=[pltpu.VMEM((2, MOE_TM * TOKEN_TILE_ROWS, LANES), F32), pltpu.VMEM((MOE_TM, D_MODEL), BF16),
                        pltpu.VMEM((MOE_TM, D_MODEL), F32), pltpu.VMEM((2, MOE_TM * TOKEN_TILE_ROWS, LANES), F32),
                        pltpu.SemaphoreType.DMA, pltpu.SemaphoreType.DMA],
    )
    return pl.pallas_call(
        _moe_body,
        grid_spec=grid_spec,
        out_shape=jax.ShapeDtypeStruct(((2 * tokens + N_EXPERTS * MOE_TM) * TOKEN_TILE_ROWS, LANES), F32),
        compiler_params=_cparams(("arbitrary", "arbitrary")),
        name="moe_experts",
    )(tile_e, n_used, src, dst, h, wg, wu, wd)


def _row_order_body(pos_ref, cnt_ref, tile_end_ref, dst_ref, *, tokens):
    spare0 = 2 * tokens
    first_tile = 0
    for e_i in range(N_EXPERTS):
        end_tile = tile_end_ref[e_i]
        end_row = end_tile * MOE_TM
        spare_base = spare0 + e_i * MOE_TM - (end_row - MOE_TM)

        def pad(p, c, spare_base=spare_base):
            dst_ref[p] = spare_base + p
            return c

        lax.fori_loop(first_tile * MOE_TM + cnt_ref[e_i], end_row, pad, 0)
        first_tile = end_tile

    def unused(p, c):
        dst_ref[p] = spare0
        return c

    lax.fori_loop(first_tile * MOE_TM, dst_ref.shape[0], unused, 0)

    unroll = 16

    def place(block, c):
        for u in range(unroll):
            a = block * unroll + u
            dst_ref[pos_ref[a]] = a
        return c

    lax.fori_loop(0, 2 * tokens // unroll, place, 0)


def _row_order(pos, cnt, tile_end, tokens, n_tiles):
    smem = pl.BlockSpec(memory_space=pltpu.SMEM)
    return pl.pallas_call(
        functools.partial(_row_order_body, tokens=tokens),
        in_specs=[smem, smem, smem],
        out_specs=smem,
        out_shape=jax.ShapeDtypeStruct((n_tiles * MOE_TM,), I32),
        name="moe_row_order",
    )(pos, cnt, tile_end)


def _moe_plan(route, counts, tokens):
    n_tiles = (2 * tokens + N_EXPERTS * (MOE_TM - 1)) // MOE_TM
    experts = jnp.arange(N_EXPERTS, dtype=I32)
    cnt = counts[:, 0].astype(I32)
    e = route[0:2].astype(I32)
    rank = route[2:4].astype(I32)
    tiles_per_e = (cnt + MOE_TM - 1) // MOE_TM
    tile_end = jnp.cumsum(tiles_per_e)
    tile_start = tile_end - tiles_per_e
    row_start = jnp.sum(jnp.where(e[..., None] == experts, tile_start * MOE_TM, 0), axis=-1)
    pos = row_start + rank
    n_used = tile_end[-1]
    tile = jnp.minimum(jnp.arange(n_tiles, dtype=I32), n_used - 1)
    tile_e = jnp.sum((tile_end[None, :] <= tile[:, None]).astype(I32), axis=1)
    dst = _row_order(pos.reshape(-1), cnt, tile_end, tokens, n_tiles)
    src = jnp.where(dst < 2 * tokens, dst % tokens, 0)
    return tile_e, n_used.reshape(1), src * TOKEN_TILE_ROWS, dst * TOKEN_TILE_ROWS


def _combine_body(x_ref, oa_ref, ob_ref, gate_ref, g_ref, o_ref, *, final):
    w1 = gate_ref[:, 0:1]
    w2 = gate_ref[:, 1:2]
    y = x_ref[...] + (w1 * _load_token_tiles(oa_ref, TM) + w2 * _load_token_tiles(ob_ref, TM))
    o_ref[...] = _rms(y, g_ref[...]) if final else y


def _combine(x, o2, route, g, final):
    t = x.shape[0]
    nt = t // TM
    gates = route[4:6].T
    return pl.pallas_call(
        functools.partial(_combine_body, final=final),
        grid=(nt,),
        in_specs=[pl.BlockSpec((TM, D_MODEL), lambda i: (i, 0)),
                  pl.BlockSpec((TM * TOKEN_TILE_ROWS, LANES), lambda i: (i, 0)),
                  pl.BlockSpec((TM * TOKEN_TILE_ROWS, LANES), lambda i: (i + nt, 0)),
                  pl.BlockSpec((TM, 2), lambda i: (i, 0)), _const_spec(g.shape)],
        out_specs=pl.BlockSpec((TM, D_MODEL), lambda i: (i, 0)),
        out_shape=jax.ShapeDtypeStruct((t, D_MODEL), F32),
        compiler_params=_cparams(("parallel",)),
        name="moe_combine",
    )(x, o2, o2, gates, g)


def _rot_half(w):
    half = w.shape[-1] // 2
    return jnp.concatenate([-w[..., half:], w[..., :half]], axis=-1)


def _rope_tables(seq):
    inv = 1.0 / (ROPE_THETA ** (jnp.arange(0, MLA_ROPE, 2, dtype=F32) / MLA_ROPE))
    ang = jnp.arange(seq, dtype=F32)[:, None] * inv[None, :]
    cos, sin = jnp.cos(ang), jnp.sin(ang)
    pad = HEAD_PAD - MLA_NOPE - MLA_ROPE
    c = jnp.concatenate([jnp.ones((seq, MLA_NOPE), F32), cos, cos, jnp.zeros((seq, pad), F32)], axis=1)
    s = jnp.concatenate([jnp.zeros((seq, MLA_NOPE), F32), sin, sin, jnp.zeros((seq, pad), F32)], axis=1)
    scale = (MLA_NOPE + MLA_ROPE) ** -0.5 * math.log2(math.e)
    return c * scale, s * scale, c, s


def _even_weights(w_in, w_q_up, w_kv_up):
    d = w_in.shape[0]
    pad = HEAD_PAD - MLA_NOPE - MLA_ROPE
    kr = w_in[:, POOL_WIDTH + MLA_Q_RANK + MLA_KV_RANK:]
    place = lambda m: jnp.concatenate([jnp.zeros((d, MLA_NOPE), F32), m, jnp.zeros((d, pad), F32)], axis=1)
    win = jnp.concatenate([w_in[:, :POOL_WIDTH + MLA_Q_RANK + MLA_KV_RANK], place(kr), place(_rot_half(kr))], axis=1)
    wq = w_q_up.reshape(MLA_Q_RANK, MLA_HEADS, MLA_NOPE + MLA_ROPE)
    nope, rope = wq[..., :MLA_NOPE], wq[..., MLA_NOPE:]
    zq = lambda n: jnp.zeros((MLA_Q_RANK, MLA_HEADS, n), F32)
    wq_main = jnp.concatenate([nope, rope, zq(pad)], axis=-1).reshape(MLA_Q_RANK, -1)
    wq_rot = jnp.concatenate([zq(MLA_NOPE), _rot_half(rope), zq(pad)], axis=-1).reshape(MLA_Q_RANK, -1)
    wkv = w_kv_up.reshape(MLA_KV_RANK, MLA_HEADS, MLA_NOPE + MLA_V)
    k_pad = jnp.concatenate([wkv[..., :MLA_NOPE], jnp.zeros((MLA_KV_RANK, MLA_HEADS, HEAD_PAD - MLA_NOPE), F32)],
                            axis=-1).reshape(MLA_KV_RANK, -1)
    v_pairs = wkv[..., MLA_NOPE:].reshape(MLA_KV_RANK, MLA_HEADS // 2, 2, MLA_V)
    v_gap = jnp.zeros((MLA_KV_RANK, MLA_HEADS // 2, 2 * (HEAD_PAD - MLA_V)), F32)
    v_ext = jnp.concatenate([v_pairs[:, :, 0], v_gap, v_pairs[:, :, 1]], axis=-1).reshape(MLA_KV_RANK, -1)
    return (win.astype(BF16), jnp.concatenate([wq_main, wq_rot], axis=1).astype(BF16),
            jnp.concatenate([k_pad, v_ext], axis=1).astype(BF16))


def _v_ones():
    lane = jnp.arange(MLA_HEADS * HEAD_PAD) % (2 * HEAD_PAD)
    return jnp.logical_and(lane >= MLA_V, lane < 2 * HEAD_PAD - MLA_V).astype(F32).reshape(1, -1)


def _band_bias(rel_bias):
    band = BAND_BLOCKS * TQ
    qpos = jnp.arange(TQ)[:, None] + (BAND_BLOCKS - 1) * TQ
    kpos = jnp.arange(band)[None, :]
    span = band + TQ - 1
    rel = (BAND_BLOCKS - 1) * TQ + (TQ - 1) - jnp.arange(span)
    by_rel = rel_bias[:, jnp.clip(rel, -REL_CLIP, REL_CLIP) + REL_CLIP].astype(F32)
    rolled = jnp.roll(by_rel, -(TQ - 1), axis=1)
    bias = jnp.tile(rolled, (1, TQ))[:, :TQ * (span - 1)].reshape(-1, TQ, span - 1)[:, :, :band]
    dc = qpos // CHUNK - kpos // CHUNK
    valid = jnp.logical_and(dc >= 0, dc <= CA_LEFT_CHUNKS)
    started = kpos // TQ >= (BAND_BLOCKS - 1) - jnp.arange(BAND_BLOCKS)[:, None, None]
    ok = jnp.logical_and(valid[None], started)[:, None]
    return jnp.where(ok, bias[None] * math.log2(math.e), NEG)


def kernel(x, ev_norm_mix, ev_w_in, ev_pool_w, ev_pool_scale, ev_q_norm, ev_w_q_up, ev_kv_norm, ev_w_kv_up, ev_w_out, ev_norm_ffn, ev_ffn_w_gate, ev_ffn_w_up, ev_ffn_w_down, od_norm_mix, od_w_in, od_rel_bias, od_sgu_norm_g, od_sgu_norm_b, od_sgu_w, od_sgu_b, od_w_out, od_norm_ffn, od_router, od_moe_w_gate, od_moe_w_up, od_moe_w_down, final_norm):
    batch, seq, d = x.shape
    tokens = batch * seq
    assert d == D_MODEL and seq % TM == 0 and TM % SGU_CHUNK == 0 and TM % TQ == 0 and seq % TK == 0
    depth = ev_w_in.shape[0] + od_w_in.shape[0]
    xt = x.reshape(tokens, d)
    tabs = _rope_tables(seq)
    tri = jnp.triu(jnp.ones((TM, TM), F32), k=1).astype(BF16)
    causal = jnp.tril(jnp.ones((SGU_CHUNK, SGU_CHUNK), bool))
    vec = lambda v: v.reshape(1, -1).astype(F32)
    moe_wg, moe_wu, moe_wd = (w.astype(BF16) for w in (od_moe_w_gate, od_moe_w_up, od_moe_w_down))

    for layer in range(depth):
        i = layer // 2
        if layer % 2 == 0:
            win, wq, wkv = _even_weights(ev_w_in[i], ev_w_q_up[i], ev_w_kv_up[i])
            a, q, k, v = _even_in(xt, vec(ev_norm_mix[i]), win, vec(ev_q_norm[i]), wq, vec(ev_kv_norm[i]), wkv,
                                  _v_ones(), tabs, seq)
            att = _mla_attention(q, k, v, batch, seq)
            xt = _even_out(xt, a, att, ev_pool_w[i].astype(BF16), vec(ev_pool_scale[i]), ev_w_out[i].astype(BF16),
                           vec(ev_norm_ffn[i]), ev_ffn_w_gate[i].astype(BF16), ev_ffn_w_up[i].astype(BF16),
                           ev_ffn_w_down[i].astype(BF16), seq)
        else:
            w_in = od_w_in[i]
            win = jnp.concatenate([w_in[:, :CA_WIDTH] * (CA_HEAD_DIM ** -0.5), w_in[:, CA_WIDTH:]], axis=1).astype(BF16)
            sw = jnp.where(causal[None], od_sgu_w[i], 0.0).astype(BF16)
            sb = jnp.broadcast_to(od_sgu_b[i][:, :, None], (SGU_GROUPS, SGU_CHUNK, SGU_CH)).astype(F32)
            qkv, sg = _odd_in(xt, vec(od_norm_mix[i]), win, vec(od_sgu_norm_g[i]), vec(od_sgu_norm_b[i]), sw, sb)
            att = _band_attention(qkv, _band_bias(od_rel_bias[i]), batch, seq)
            wr = jnp.concatenate([od_router[i], jnp.zeros((d, ROUTE_LANES - N_EXPERTS), F32)], axis=1)
            wr_hi = wr.astype(BF16)
            wr = jnp.stack([wr_hi, (wr - wr_hi.astype(F32)).astype(BF16)])
            x1, h, route, counts = _odd_out(xt, att, sg, od_w_out[i].astype(BF16), vec(od_norm_ffn[i]), wr, tri)
            tile_e, n_used, src, dst = _moe_plan(route, counts, tokens)
            o2 = _moe(h, tile_e, n_used, src, dst, moe_wg, moe_wu, moe_wd, i)
            last = layer == depth - 1
            xt = _combine(x1, o2, route, vec(final_norm), last)
    return xt.reshape(batch, seq, d)
```

```python
import functools
import math

import jax
import jax.numpy as jnp
from jax import lax
from jax.experimental import pallas as pl
from jax.experimental.pallas import tpu as pltpu

F32 = jnp.float32
BF16 = jnp.bfloat16
I32 = jnp.int32

D_MODEL = 1024
CHUNK = 64
EPS = 1e-6

POOL_GROUPS = 4
POOL_CH = 128
POOL_WIDTH = 512
POOL_WINDOWS = (2, 4, 8, 16)
POOL_HALO = 16

MLA_HEADS = 8
MLA_NOPE = 64
MLA_ROPE = 32
MLA_V = 64
MLA_Q_RANK = 256
MLA_KV_RANK = 128
ROPE_THETA = 10000.0
HEAD_PAD = 128

CA_HEADS = 8
CA_HEAD_DIM = 64
CA_LEFT_CHUNKS = 8
REL_CLIP = 128
CA_WIDTH = CA_HEADS * CA_HEAD_DIM

SGU_GROUPS = 4
SGU_CH = 128
SGU_WIDTH = 512
SGU_CHUNK = 128

D_FF = 2816
N_EXPERTS = 8
EXPERT_FF = 3584

NEG = -1e30
LANES = 128

TM = 512
TQ = 256
TK = 512
MLA_TQ = 512
MOE_TF = 512
MOE_FF_STEPS = EXPERT_FF // MOE_TF
MOE_TM = 80 * MOE_FF_STEPS
VMEM_LIMIT = 56 * 1024 * 1024


def _cparams(sem):
    return pltpu.CompilerParams(dimension_semantics=sem, vmem_limit_bytes=VMEM_LIMIT)


def _rms(x, g):
    return x * lax.rsqrt(jnp.mean(x * x, axis=-1, keepdims=True) + EPS) * g


def _dot(a, b):
    return jnp.dot(a, b, preferred_element_type=F32)


def _dot_nt(a, b):
    return lax.dot_general(a, b, (((1,), (1,)), ((), ())), preferred_element_type=F32)


TOKEN_TILE_ROWS = D_MODEL // LANES


def _store_token_tiles(ref, x):
    n = x.shape[0]
    for s in range(TOKEN_TILE_ROWS):
        ref[pl.ds(s, n, stride=TOKEN_TILE_ROWS), :] = x[:, s * LANES:(s + 1) * LANES]


def _load_token_tiles(ref, n):
    return jnp.concatenate([ref[pl.ds(s, n, stride=TOKEN_TILE_ROWS), :] for s in range(TOKEN_TILE_ROWS)], axis=1)


def _const_spec(shape):
    nd = len(shape)
    return pl.BlockSpec(shape, lambda *_: (0,) * nd, pipeline_mode=pl.Buffered(1))


def _even_in_body(x_ref, g_ref, win_ref, qn_ref, wq_ref, kvn_ref, wkv_ref, vones_ref, cq_ref, sq_ref, ck_ref, sk_ref,
                  a_ref, q_ref, k_ref, v_ref):
    h = _rms(x_ref[...], g_ref[...]).astype(BF16)
    z = _dot(h, win_ref[...])
    a_ref[...] = z[:, :POOL_WIDTH]
    o = POOL_WIDTH
    cq = z[:, o:o + MLA_Q_RANK]
    o += MLA_Q_RANK
    ckv = z[:, o:o + MLA_KV_RANK]
    o += MLA_KV_RANK
    kr = z[:, o:o + HEAD_PAD]
    kr_rot = z[:, o + HEAD_PAD:o + 2 * HEAD_PAD]

    hw = MLA_HEADS * HEAD_PAD
    qq = _dot(_rms(cq, qn_ref[...]).astype(BF16), wq_ref[...])
    cq_t, sq_t = cq_ref[...], sq_ref[...]
    for h_i in range(MLA_HEADS):
        sl = slice(h_i * HEAD_PAD, (h_i + 1) * HEAD_PAD)
        sl_rot = slice(hw + h_i * HEAD_PAD, hw + (h_i + 1) * HEAD_PAD)
        q_ref[:, sl] = (qq[:, sl] * cq_t + qq[:, sl_rot] * sq_t).astype(BF16)

    kv = _dot(_rms(ckv, kvn_ref[...]).astype(BF16), wkv_ref[...])
    k_rope = kr * ck_ref[...] + kr_rot * sk_ref[...]
    for h_i in range(MLA_HEADS):
        sl = slice(h_i * HEAD_PAD, (h_i + 1) * HEAD_PAD)
        k_ref[:, sl] = (kv[:, sl] + k_rope).astype(BF16)
    v_ref[...] = (kv[:, hw:] + vones_ref[...]).astype(BF16)


def _even_in(x, g, win, qn, wq, kvn, wkv, vones, tabs, seq):
    t = x.shape[0]
    hw = MLA_HEADS * HEAD_PAD
    row = lambda w: pl.BlockSpec((TM, w), lambda i: (i, 0))
    tab = pl.BlockSpec((TM, HEAD_PAD), lambda i: (i % (seq // TM), 0))
    return pl.pallas_call(
        _even_in_body,
        grid=(t // TM,),
        in_specs=[row(D_MODEL), _const_spec(g.shape), _const_spec(win.shape), _const_spec(qn.shape),
                  _const_spec(wq.shape), _const_spec(kvn.shape), _const_spec(wkv.shape), _const_spec(vones.shape),
                  tab, tab, tab, tab],
        out_specs=[row(POOL_WIDTH), row(hw), row(hw), row(hw)],
        out_shape=[jax.ShapeDtypeStruct((t, POOL_WIDTH), F32), jax.ShapeDtypeStruct((t, hw), BF16),
                   jax.ShapeDtypeStruct((t, hw), BF16), jax.ShapeDtypeStruct((t, hw), BF16)],
        compiler_params=_cparams(("parallel",)),
        name="even_in",
    )(x, g, win, qn, wq, kvn, wkv, vones, *tabs)


def _mla_body(q_ref, k_ref, v_ref, o_ref, s_ref, mx_ref, acc_ref):
    qi = pl.program_id(2)
    assert MLA_TQ == TK
    last = qi
    halves = (slice(0, HEAD_PAD), slice(HEAD_PAD, 2 * HEAD_PAD))
    lane_blocks = [slice(c * LANES, (c + 1) * LANES) for c in range(TK // LANES)]
    q_chunk = lax.broadcasted_iota(I32, (MLA_TQ, TK), 0) // CHUNK + qi * (MLA_TQ // CHUNK)
    k_chunk = lax.broadcasted_iota(I32, (MLA_TQ, TK), 1) // CHUNK + last * (TK // CHUNK)
    visible = k_chunk <= q_chunk

    mx_ref[...] = jnp.full(mx_ref.shape, NEG, F32)
    acc_ref[...] = jnp.zeros_like(acc_ref)

    def scores(j):
        rows = pl.ds(pl.multiple_of(j * TK, TK), TK)
        for hh, hs in enumerate(halves):
            s = _dot_nt(q_ref[:, hs], k_ref[rows, hs])
            s_ref[hh, j] = s
            mx_ref[hh] = functools.reduce(jnp.maximum, [s[:, lb] for lb in lane_blocks], mx_ref[hh])

    def pairs_then_rest(count, body):
        def two(jj, c):
            body(2 * jj)
            body(2 * jj + 1)
            return c

        lax.fori_loop(0, count // 2, two, 0)

        @pl.when(count % 2 == 1)
        def _():
            body(count - 1)

    half = MLA_TQ // 2
    top, bot = slice(0, half), slice(half, MLA_TQ)
    half_blocks = lane_blocks[:len(lane_blocks) // 2]
    diag_rows = pl.ds(pl.multiple_of(last * TK, TK), TK)
    diag_first_rows = pl.ds(pl.multiple_of(last * TK, TK), half)

    def diag_scores():
        for hh, hs in enumerate(halves):
            s_top = jnp.where(visible[top, :half], _dot_nt(q_ref[top, hs], k_ref[diag_first_rows, hs]), NEG)
            s_bot = jnp.where(visible[bot], _dot_nt(q_ref[bot, hs], k_ref[diag_rows, hs]), NEG)
            s_ref[hh, last, top, :half] = s_top
            s_ref[hh, last, bot, :] = s_bot
            mx_ref[hh, top] = functools.reduce(jnp.maximum, [s_top[:, lb] for lb in half_blocks], mx_ref[hh, top])
            mx_ref[hh, bot] = functools.reduce(jnp.maximum, [s_bot[:, lb] for lb in lane_blocks], mx_ref[hh, bot])

    pairs_then_rest(last, scores)
    diag_scores()

    for hh in range(2):
        m = jnp.max(mx_ref[hh], axis=-1, keepdims=True)
        mx_ref[hh] = jnp.broadcast_to(m, (MLA_TQ, LANES))

    def weights(hh, j, row_slice, blocks):
        m = mx_ref[hh, row_slice]
        return jnp.concatenate([jnp.exp2(s_ref[hh, j, row_slice, lb] - m) for lb in blocks], axis=1).astype(BF16)

    def weighted(j):
        rows = pl.ds(pl.multiple_of(j * TK, TK), TK)
        for hh, hs in enumerate(halves):
            acc_ref[hh] += _dot(weights(hh, j, slice(None), lane_blocks), v_ref[rows, hs])

    def diag_weighted():
        for hh, hs in enumerate(halves):
            acc_ref[hh, top] += _dot(weights(hh, last, top, half_blocks), v_ref[diag_first_rows, hs])
            acc_ref[hh, bot] += _dot(weights(hh, last, bot, lane_blocks), v_ref[diag_rows, hs])

    pairs_then_rest(last, weighted)
    diag_weighted()

    even, odd = acc_ref[0], acc_ref[1]
    lane = lax.broadcasted_iota(I32, (MLA_TQ, 2 * MLA_V), 1)
    out = jnp.where(lane < MLA_V, even / even[:, MLA_V:MLA_V + 1], odd / odd[:, 0:1])
    o_ref[...] = out.astype(BF16)


def _mla_attention(q, k, v, batch, seq):
    t = q.shape[0]
    nq = seq // MLA_TQ
    pairs = MLA_HEADS // 2
    return pl.pallas_call(
        _mla_body,
        grid=(batch, pairs, nq),
        in_specs=[pl.BlockSpec((MLA_TQ, 2 * HEAD_PAD), lambda b, p, i: (b * nq + i, p)),
                  pl.BlockSpec((seq, 2 * HEAD_PAD), lambda b, p, i: (b, p)),
                  pl.BlockSpec((seq, 2 * HEAD_PAD), lambda b, p, i: (b, p))],
        out_specs=pl.BlockSpec((MLA_TQ, 2 * MLA_V), lambda b, p, i: (b * nq + i, p)),
        out_shape=jax.ShapeDtypeStruct((t, MLA_HEADS * MLA_V), BF16),
        scratch_shapes=[pltpu.VMEM((2, seq // TK, MLA_TQ, TK), F32), pltpu.VMEM((2, MLA_TQ, LANES), F32),
                        pltpu.VMEM((2, MLA_TQ, HEAD_PAD), F32)],
        compiler_params=_cparams(("parallel", "parallel", "arbitrary")),
        name="mla_attention",
    )(q, k, v)


def _even_out_body(x_ref, a_ref, halo_ref, att_ref, pw_ref, ps_ref, wo_ref, g_ref, wg_ref, wu_ref, wd_ref,
                   o_ref, ext_ref, *, seq):
    pos0 = (pl.program_id(0) * TM) % seq
    ext_ref[0:POOL_HALO, :] = jnp.where(pos0 == 0, 0.0, halo_ref[...])
    ext_ref[POOL_HALO:, :] = a_ref[...]
    pos = lax.broadcasted_iota(I32, (TM, 1), 0) + pos0
    y = x_ref[...] + _dot(att_ref[...], wo_ref[POOL_WIDTH:, :])
    for g_i, win in enumerate(POOL_WINDOWS):
        gs = slice(g_i * POOL_CH, (g_i + 1) * POOL_CH)
        a_g = ext_ref[POOL_HALO:, gs]
        ssum = a_g
        for j in range(1, win):
            ssum = ssum + ext_ref[POOL_HALO - j:POOL_HALO - j + TM, gs]
        count = jnp.minimum(pos + 1, win).astype(F32)
        pooled = (ssum / count - a_g).astype(BF16)
        mixed = (_dot(pooled, pw_ref[g_i]) * ps_ref[:, gs]).astype(BF16)
        y = y + _dot(mixed, wo_ref[gs, :])
    h = _rms(y, g_ref[...]).astype(BF16)
    gate = _dot(h, wg_ref[...])
    up = _dot(h, wu_ref[...])
    act = (gate * jax.nn.sigmoid(gate) * up).astype(BF16)
    o_ref[...] = y + _dot(act, wd_ref[...])


def _even_out(x, a, att, pw, ps, wo, g, wg, wu, wd, seq):
    t = x.shape[0]
    row = lambda w: pl.BlockSpec((TM, w), lambda i: (i, 0))
    halo = pl.BlockSpec((POOL_HALO, POOL_WIDTH), lambda i: (jnp.maximum(i * (TM // POOL_HALO) - 1, 0), 0))
    return pl.pallas_call(
        functools.partial(_even_out_body, seq=seq),
        grid=(t // TM,),
        in_specs=[row(D_MODEL), row(POOL_WIDTH), halo, row(MLA_HEADS * MLA_V), _const_spec(pw.shape),
                  _const_spec(ps.shape), _const_spec(wo.shape), _const_spec(g.shape), _const_spec(wg.shape),
                  _const_spec(wu.shape), _const_spec(wd.shape)],
        out_specs=row(D_MODEL),
        out_shape=jax.ShapeDtypeStruct((t, D_MODEL), F32),
        scratch_shapes=[pltpu.VMEM((TM + POOL_HALO, POOL_WIDTH), F32)],
        compiler_params=_cparams(("parallel",)),
        name="even_out",
    )(x, a, a, att, pw, ps, wo, g, wg, wu, wd)


def _odd_in_body(x_ref, g_ref, win_ref, ng_ref, nb_ref, sw_ref, sb_ref, qkv_ref, sg_ref):
    h = _rms(x_ref[...], g_ref[...]).astype(BF16)
    z = _dot(h, win_ref[...])
    qkv_ref[:, :CA_WIDTH] = (z[:, :CA_WIDTH] * math.log2(math.e)).astype(BF16)
    qkv_ref[:, CA_WIDTH:] = z[:, CA_WIDTH:3 * CA_WIDTH].astype(BF16)
    uv = jax.nn.gelu(z[:, 3 * CA_WIDTH:])
    for g_i in range(SGU_GROUPS):
        gs = slice(g_i * SGU_CH, (g_i + 1) * SGU_CH)
        u = uv[:, gs]
        gv = uv[:, SGU_WIDTH + g_i * SGU_CH:SGU_WIDTH + (g_i + 1) * SGU_CH]
        mu = jnp.mean(gv, axis=-1, keepdims=True)
        xc = gv - mu
        vn = xc * lax.rsqrt(jnp.mean(xc * xc, axis=-1, keepdims=True) + EPS) * ng_ref[:, gs] + nb_ref[:, gs]
        vn = vn.astype(BF16)
        w = sw_ref[g_i]
        for c_i in range(TM // SGU_CHUNK):
            rs = slice(c_i * SGU_CHUNK, (c_i + 1) * SGU_CHUNK)
            mixed = _dot(w, vn[rs, :]) + sb_ref[g_i]
            sg_ref[rs, gs] = (u[rs, :] * mixed).astype(BF16)


def _odd_in(x, g, win, ng, nb, sw, sb):
    t = x.shape[0]
    row = lambda w: pl.BlockSpec((TM, w), lambda i: (i, 0))
    return pl.pallas_call(
        _odd_in_body,
        grid=(t // TM,),
        in_specs=[row(D_MODEL), _const_spec(g.shape), _const_spec(win.shape), _const_spec(ng.shape),
                  _const_spec(nb.shape), _const_spec(sw.shape), _const_spec(sb.shape)],
        out_specs=[row(3 * CA_WIDTH), row(SGU_WIDTH)],
        out_shape=[jax.ShapeDtypeStruct((t, 3 * CA_WIDTH), BF16), jax.ShapeDtypeStruct((t, SGU_WIDTH), BF16)],
        compiler_params=_cparams(("parallel",)),
        name="odd_in",
    )(x, g, win, ng, nb, sw, sb)


BAND_BLOCKS = 3


def _band_body(q_ref, k0_ref, k1_ref, k2_ref, v0_ref, v1_ref, v2_ref, bias_ref, o_ref):
    pair_w = 2 * CA_HEAD_DIM
    low = lax.broadcasted_iota(I32, (TQ, pair_w), 1) < CA_HEAD_DIM
    k_refs = (k0_ref, k1_ref, k2_ref)
    v_refs = (v0_ref, v1_ref, v2_ref)
    add = lambda a, b: a + b
    per_tile = TQ // LANES
    for pair in range(CA_HEADS // 2):
        ps = slice(pair * pair_w, (pair + 1) * pair_w)
        q_pair = q_ref[:, ps]
        outs = []
        for hh in range(2):
            q = jnp.where(low if hh == 0 else jnp.logical_not(low), q_pair, jnp.zeros_like(q_pair))
            s = []
            for jj in range(BAND_BLOCKS):
                sj = _dot_nt(q, k_refs[jj][:, ps]) + bias_ref[2 * pair + hh, :, jj * TQ:(jj + 1) * TQ]
                s += [sj[:, c * LANES:(c + 1) * LANES] for c in range(per_tile)]
            m = jnp.max(functools.reduce(jnp.maximum, s), axis=-1, keepdims=True)
            mb = jnp.broadcast_to(m, (TQ, LANES))
            p = [jnp.exp2(x - mb) for x in s]
            l = jnp.sum(functools.reduce(add, p), axis=-1, keepdims=True)
            acc = functools.reduce(add, [
                _dot(jnp.concatenate(p[jj * per_tile:(jj + 1) * per_tile], axis=1).astype(BF16), v_refs[jj][:, ps])
                for jj in range(BAND_BLOCKS)])
            outs.append(acc / l)
        o_ref[:, ps] = jnp.where(low, outs[0], outs[1]).astype(BF16)


def _band_attention(qkv, bias, batch, seq):
    t = qkv.shape[0]
    nq = seq // TQ

    def kv_spec(jj, col):
        return pl.BlockSpec((TQ, CA_WIDTH), lambda b, i: (b * nq + jnp.maximum(i - (BAND_BLOCKS - 1) + jj, 0), col))

    return pl.pallas_call(
        _band_body,
        grid=(batch, nq),
        in_specs=[pl.BlockSpec((TQ, CA_WIDTH), lambda b, i: (b * nq + i, 0))]
                 + [kv_spec(jj, 1) for jj in range(BAND_BLOCKS)]
                 + [kv_spec(jj, 2) for jj in range(BAND_BLOCKS)]
                 + [pl.BlockSpec((None, CA_HEADS, TQ, BAND_BLOCKS * TQ),
                                 lambda b, i: (jnp.minimum(i, BAND_BLOCKS - 1), 0, 0, 0))],
        out_specs=pl.BlockSpec((TQ, CA_WIDTH), lambda b, i: (b * nq + i, 0)),
        out_shape=jax.ShapeDtypeStruct((t, CA_WIDTH), BF16),
        compiler_params=_cparams(("parallel", "arbitrary")),
        name="band_attention",
    )(qkv, qkv, qkv, qkv, qkv, qkv, qkv, bias)


ROUTE_LANES = 128
ROUTE_ROWS = 8


def _odd_out_body(x_ref, att_ref, sg_ref, wo_ref, g_ref, wr_ref, tri_ref, x1_ref, h_ref, route_ref, cnt_ref, run_ref):
    @pl.when(pl.program_id(0) == 0)
    def _():
        run_ref[...] = jnp.zeros_like(run_ref)

    y = x_ref[...] + _dot(att_ref[...], wo_ref[:CA_WIDTH, :]) + _dot(sg_ref[...], wo_ref[CA_WIDTH:, :])
    x1_ref[...] = y
    h = _rms(y, g_ref[...])
    _store_token_tiles(h_ref, h)
    h_hi = h.astype(BF16)
    h_lo = (h - h_hi.astype(F32)).astype(BF16)
    logits = _dot(h_hi, wr_ref[0]) + (_dot(h_lo, wr_ref[0]) + _dot(h_hi, wr_ref[1]))
    lt = logits.T[:N_EXPERTS]
    row = lax.broadcasted_iota(I32, (N_EXPERTS, TM), 0)
    m1 = jnp.max(lt, axis=0, keepdims=True)
    e1 = jnp.min(jnp.where(lt == m1, row, N_EXPERTS), axis=0, keepdims=True)
    rest = jnp.where(row == e1, NEG, lt)
    m2 = jnp.max(rest, axis=0, keepdims=True)
    e2 = jnp.min(jnp.where(rest == m2, row, N_EXPERTS), axis=0, keepdims=True)
    ex = jnp.exp(m2 - m1)
    w1 = 1.0 / (1.0 + ex)
    w2 = ex / (1.0 + ex)
    hot1 = (row == e1).astype(F32)
    hot2 = (row == e2).astype(F32)
    sel = hot1 + hot2
    before = _dot(sel.astype(BF16), tri_ref[...]) + run_ref[:, 0:1]
    rank1 = jnp.sum(hot1 * before, axis=0, keepdims=True)
    rank2 = jnp.sum(hot2 * before, axis=0, keepdims=True)
    unused = jnp.zeros((ROUTE_ROWS - 6, TM), F32)
    route_ref[...] = jnp.concatenate([e1.astype(F32), e2.astype(F32), rank1, rank2, w1, w2, unused], axis=0)
    total = run_ref[:, 0:1] + jnp.sum(sel, axis=1, keepdims=True)
    run_ref[...] = jnp.broadcast_to(total, run_ref.shape)
    cnt_ref[...] = jnp.broadcast_to(total, cnt_ref.shape)


def _odd_out(x, att, sg, wo, g, wr, tri):
    t = x.shape[0]
    row = lambda w: pl.BlockSpec((TM, w), lambda i: (i, 0))
    return pl.pallas_call(
        _odd_out_body,
        grid=(t // TM,),
        in_specs=[row(D_MODEL), row(CA_WIDTH), row(SGU_WIDTH), _const_spec(wo.shape), _const_spec(g.shape),
                  _const_spec(wr.shape), _const_spec(tri.shape)],
        out_specs=[row(D_MODEL), pl.BlockSpec((TM * TOKEN_TILE_ROWS, LANES), lambda i: (i, 0)),
                   pl.BlockSpec((ROUTE_ROWS, TM), lambda i: (0, i)),
                   pl.BlockSpec((N_EXPERTS, ROUTE_LANES), lambda i: (0, 0))],
        out_shape=[jax.ShapeDtypeStruct((t, D_MODEL), F32), jax.ShapeDtypeStruct((t * TOKEN_TILE_ROWS, LANES), F32),
                   jax.ShapeDtypeStruct((ROUTE_ROWS, t), F32), jax.ShapeDtypeStruct((N_EXPERTS, ROUTE_LANES), F32)],
        scratch_shapes=[pltpu.VMEM((N_EXPERTS, ROUTE_LANES), F32)],
        compiler_params=_cparams(("arbitrary",)),
        name="odd_out_router",
    )(x, att, sg, wo, g, wr, tri)


def _moe_body(tile_e_ref, n_used_ref, src_ref, dst_ref, h_hbm, wg_ref, wu_ref, wd_ref, o_hbm,
              xf_ref, xb_ref, acc_ref, y_ref, gsem, ssem):
    i = pl.program_id(0)
    j = pl.program_id(1)
    n_used = n_used_ref[0]
    rows_per_step = MOE_TM // MOE_FF_STEPS

    ttr = TOKEN_TILE_ROWS
    tile_rows = MOE_TM * ttr

    def token_rows(ref, first_row):
        return ref.at[pl.ds(pl.multiple_of(first_row, ttr), ttr), :]

    def gather_copy(tile, r, slot):
        return pltpu.make_async_copy(token_rows(h_hbm, src_ref[tile * MOE_TM + r]),
                                     token_rows(xf_ref.at[slot], r * ttr), gsem)

    def scatter_copy(tile, r, slot):
        return pltpu.make_async_copy(token_rows(y_ref.at[slot], r * ttr),
                                     token_rows(o_hbm, dst_ref[tile * MOE_TM + r]), ssem)

    def wait_tile_gather(slot):
        pltpu.make_async_copy(h_hbm.at[pl.ds(0, tile_rows), :], xf_ref.at[slot], gsem).wait()

    def wait_tile_scatter(slot):
        pltpu.make_async_copy(y_ref.at[slot], o_hbm.at[pl.ds(0, tile_rows), :], ssem).wait()

    def start_all(copy_fn):
        lax.fori_loop(0, MOE_TM, lambda r, c: (copy_fn(r).start(), c)[1], 0)

    @pl.when(i < n_used)
    def _():
        cur = lax.rem(i, 2)
        other = 1 - cur
        nxt_tile = jnp.minimum(i + 1, n_used - 1)
        prev_tile = jnp.maximum(i - 1, 0)

        @pl.when(jnp.logical_and(i == 0, j == 0))
        def _():
            y_ref[...] = jnp.zeros_like(y_ref)
            acc_ref[...] = jnp.zeros_like(acc_ref)
            spare0 = o_hbm.shape[0] - N_EXPERTS * tile_rows
            for e_i in range(N_EXPERTS):
                fill = pltpu.make_async_copy(y_ref.at[0], o_hbm.at[pl.ds(spare0 + e_i * tile_rows, tile_rows), :],
                                             ssem)
                fill.start()
                fill.wait()
            start_all(lambda r: gather_copy(0, r, 0))

        @pl.when(j == 0)
        def _():
            wait_tile_gather(cur)
            xb_ref[...] = _load_token_tiles(xf_ref.at[cur], MOE_TM).astype(BF16)

        base = pl.multiple_of(j * rows_per_step, 8)
        for rr in range(rows_per_step):
            gather_copy(nxt_tile, base + rr, other).start()
            scatter_copy(prev_tile, base + rr, other).start()
        xb = xb_ref[...]
        gate = _dot(xb, wg_ref[...])
        up = _dot(xb, wu_ref[...])
        act = (gate * jax.nn.sigmoid(gate) * up).astype(BF16)
        acc_ref[...] = jnp.where(j == 0, 0.0, acc_ref[...]) + _dot(act, wd_ref[...])

        @pl.when(j == MOE_FF_STEPS - 1)
        def _():
            wait_tile_scatter(other)
            _store_token_tiles(y_ref.at[cur], acc_ref[...])

        @pl.when(jnp.logical_and(i == n_used - 1, j == MOE_FF_STEPS - 1))
        def _():
            start_all(lambda r: scatter_copy(i, r, cur))
            wait_tile_scatter(cur)
            wait_tile_gather(other)


def _moe(h, tile_e, n_used, src, dst, wg, wu, wd, layer):
    tokens = h.shape[0] // TOKEN_TILE_ROWS
    n_tiles = tile_e.shape[0]
    last = MOE_FF_STEPS - 1

    def ff_idx(i, j, n_used_ref):
        return jnp.where(i < n_used_ref[0], j, last)

    grid_spec = pltpu.PrefetchScalarGridSpec(
        num_scalar_prefetch=4,
        grid=(n_tiles, MOE_FF_STEPS),
        in_specs=[pl.BlockSpec(memory_space=pl.ANY),
                  pl.BlockSpec((None, None, D_MODEL, MOE_TF),
                               lambda i, j, te, nu, sr, ds: (layer, te[i], 0, ff_idx(i, j, nu))),
                  pl.BlockSpec((None, None, D_MODEL, MOE_TF),
                               lambda i, j, te, nu, sr, ds: (layer, te[i], 0, ff_idx(i, j, nu))),
                  pl.BlockSpec((None, None, MOE_TF, D_MODEL),
                               lambda i, j, te, nu, sr, ds: (layer, te[i], ff_idx(i, j, nu), 0))],
        out_specs=pl.BlockSpec(memory_space=pl.ANY),
        scratch_shapes=[pltpu.VMEM((2, MOE_TM * TOKEN_TILE_ROWS, LANES), F32), pltpu.VMEM((MOE_TM, D_MODEL), BF16),
                        pltpu.VMEM((MOE_TM, D_MODEL), F32), pltpu.VMEM((2, MOE_TM * TOKEN_TILE_ROWS, LANES), F32),
                        pltpu.SemaphoreType.DMA, pltpu.SemaphoreType.DMA],
    )
    return pl.pallas_call(
        _moe_body,
        grid_spec=grid_spec,
        out_shape=jax.ShapeDtypeStruct(((2 * tokens + N_EXPERTS * MOE_TM) * TOKEN_TILE_ROWS, LANES), F32),
        compiler_params=_cparams(("arbitrary", "arbitrary")),
        name="moe_experts",
    )(tile_e, n_used, src, dst, h, wg, wu, wd)


def _row_order_body(pos_ref, cnt_ref, tile_end_ref, dst_ref, *, tokens):
    spare0 = 2 * tokens
    first_tile = 0
    for e_i in range(N_EXPERTS):
        end_tile = tile_end_ref[e_i]
        end_row = end_tile * MOE_TM
        spare_base = spare0 + e_i * MOE_TM - (end_row - MOE_TM)

        def pad(p, c, spare_base=spare_base):
            dst_ref[p] = spare_base + p
            return c

        lax.fori_loop(first_tile * MOE_TM + cnt_ref[e_i], end_row, pad, 0)
        first_tile = end_tile

    def unused(p, c):
        dst_ref[p] = spare0
        return c

    lax.fori_loop(first_tile * MOE_TM, dst_ref.shape[0], unused, 0)

    unroll = 16

    def place(block, c):
        for u in range(unroll):
            a = block * unroll + u
            dst_ref[pos_ref[a]] = a
        return c

    lax.fori_loop(0, 2 * tokens // unroll, place, 0)


def _row_order(pos, cnt, tile_end, tokens, n_tiles):
    smem = pl.BlockSpec(memory_space=pltpu.SMEM)
    return pl.pallas_call(
        functools.partial(_row_order_body, tokens=tokens),
        in_specs=[smem, smem, smem],
        out_specs=smem,
        out_shape=jax.ShapeDtypeStruct((n_tiles * MOE_TM,), I32),
        name="moe_row_order",
    )(pos, cnt, tile_end)


def _moe_plan(route, counts, tokens):
    n_tiles = (2 * tokens + N_EXPERTS * (MOE_TM - 1)) // MOE_TM
    experts = jnp.arange(N_EXPERTS, dtype=I32)
    cnt = counts[:, 0].astype(I32)
    e = route[0:2].astype(I32)
    rank = route[2:4].astype(I32)
    tiles_per_e = (cnt + MOE_TM - 1) // MOE_TM
    tile_end = jnp.cumsum(tiles_per_e)
    tile_start = tile_end - tiles_per_e
    row_start = jnp.sum(jnp.where(e[..., None] == experts, tile_start * MOE_TM, 0), axis=-1)
    pos = row_start + rank
    n_used = tile_end[-1]
    tile = jnp.minimum(jnp.arange(n_tiles, dtype=I32), n_used - 1)
    tile_e = jnp.sum((tile_end[None, :] <= tile[:, None]).astype(I32), axis=1)
    dst = _row_order(pos.reshape(-1), cnt, tile_end, tokens, n_tiles)
    src = jnp.where(dst < 2 * tokens, dst % tokens, 0)
    return tile_e, n_used.reshape(1), src * TOKEN_TILE_ROWS, dst * TOKEN_TILE_ROWS


def _combine_body(x_ref, oa_ref, ob_ref, gate_ref, g_ref, o_ref, *, final):
    w1 = gate_ref[:, 0:1]
    w2 = gate_ref[:, 1:2]
    y = x_ref[...] + (w1 * _load_token_tiles(oa_ref, TM) + w2 * _load_token_tiles(ob_ref, TM))
    o_ref[...] = _rms(y, g_ref[...]) if final else y


def _combine(x, o2, route, g, final):
    t = x.shape[0]
    nt = t // TM
    gates = route[4:6].T
    return pl.pallas_call(
        functools.partial(_combine_body, final=final),
        grid=(nt,),
        in_specs=[pl.BlockSpec((TM, D_MODEL), lambda i: (i, 0)),
                  pl.BlockSpec((TM * TOKEN_TILE_ROWS, LANES), lambda i: (i, 0)),
                  pl.BlockSpec((TM * TOKEN_TILE_ROWS, LANES), lambda i: (i + nt, 0)),
                  pl.BlockSpec((TM, 2), lambda i: (i, 0)), _const_spec(g.shape)],
        out_specs=pl.BlockSpec((TM, D_MODEL), lambda i: (i, 0)),
        out_shape=jax.ShapeDtypeStruct((t, D_MODEL), F32),
        compiler_params=_cparams(("parallel",)),
        name="moe_combine",
    )(x, o2, o2, gates, g)


def _rot_half(w):
    half = w.shape[-1] // 2
    return jnp.concatenate([-w[..., half:], w[..., :half]], axis=-1)


def _rope_tables(seq):
    inv = 1.0 / (ROPE_THETA ** (jnp.arange(0, MLA_ROPE, 2, dtype=F32) / MLA_ROPE))
    ang = jnp.arange(seq, dtype=F32)[:, None] * inv[None, :]
    cos, sin = jnp.cos(ang), jnp.sin(ang)
    pad = HEAD_PAD - MLA_NOPE - MLA_ROPE
    c = jnp.concatenate([jnp.ones((seq, MLA_NOPE), F32), cos, cos, jnp.zeros((seq, pad), F32)], axis=1)
    s = jnp.concatenate([jnp.zeros((seq, MLA_NOPE), F32), sin, sin, jnp.zeros((seq, pad), F32)], axis=1)
    scale = (MLA_NOPE + MLA_ROPE) ** -0.5 * math.log2(math.e)
    return c * scale, s * scale, c, s


def _even_weights(w_in, w_q_up, w_kv_up):
    d = w_in.shape[0]
    pad = HEAD_PAD - MLA_NOPE - MLA_ROPE
    kr = w_in[:, POOL_WIDTH + MLA_Q_RANK + MLA_KV_RANK:]
    place = lambda m: jnp.concatenate([jnp.zeros((d, MLA_NOPE), F32), m, jnp.zeros((d, pad), F32)], axis=1)
    win = jnp.concatenate([w_in[:, :POOL_WIDTH + MLA_Q_RANK + MLA_KV_RANK], place(kr), place(_rot_half(kr))], axis=1)
    wq = w_q_up.reshape(MLA_Q_RANK, MLA_HEADS, MLA_NOPE + MLA_ROPE)
    nope, rope = wq[..., :MLA_NOPE], wq[..., MLA_NOPE:]
    zq = lambda n: jnp.zeros((MLA_Q_RANK, MLA_HEADS, n), F32)
    wq_main = jnp.concatenate([nope, rope, zq(pad)], axis=-1).reshape(MLA_Q_RANK, -1)
    wq_rot = jnp.concatenate([zq(MLA_NOPE), _rot_half(rope), zq(pad)], axis=-1).reshape(MLA_Q_RANK, -1)
    wkv = w_kv_up.reshape(MLA_KV_RANK, MLA_HEADS, MLA_NOPE + MLA_V)
    k_pad = jnp.concatenate([wkv[..., :MLA_NOPE], jnp.zeros((MLA_KV_RANK, MLA_HEADS, HEAD_PAD - MLA_NOPE), F32)],
                            axis=-1).reshape(MLA_KV_RANK, -1)
    v_pairs = wkv[..., MLA_NOPE:].reshape(MLA_KV_RANK, MLA_HEADS // 2, 2, MLA_V)
    v_gap = jnp.zeros((MLA_KV_RANK, MLA_HEADS // 2, 2 * (HEAD_PAD - MLA_V)), F32)
    v_ext = jnp.concatenate([v_pairs[:, :, 0], v_gap, v_pairs[:, :, 1]], axis=-1).reshape(MLA_KV_RANK, -1)
    return (win.astype(BF16), jnp.concatenate([wq_main, wq_rot], axis=1).astype(BF16),
            jnp.concatenate([k_pad, v_ext], axis=1).astype(BF16))


def _v_ones():
    lane = jnp.arange(MLA_HEADS * HEAD_PAD) % (2 * HEAD_PAD)
    return jnp.logical_and(lane >= MLA_V, lane < 2 * HEAD_PAD - MLA_V).astype(F32).reshape(1, -1)


def _band_bias(rel_bias):
    band = BAND_BLOCKS * TQ
    qpos = jnp.arange(TQ)[:, None] + (BAND_BLOCKS - 1) * TQ
    kpos = jnp.arange(band)[None, :]
    span = band + TQ - 1
    rel = (BAND_BLOCKS - 1) * TQ + (TQ - 1) - jnp.arange(span)
    by_rel = rel_bias[:, jnp.clip(rel, -REL_CLIP, REL_CLIP) + REL_CLIP].astype(F32)
    rolled = jnp.roll(by_rel, -(TQ - 1), axis=1)
    bias = jnp.tile(rolled, (1, TQ))[:, :TQ * (span - 1)].reshape(-1, TQ, span - 1)[:, :, :band]
    dc = qpos // CHUNK - kpos // CHUNK
    valid = jnp.logical_and(dc >= 0, dc <= CA_LEFT_CHUNKS)
    started = kpos // TQ >= (BAND_BLOCKS - 1) - jnp.arange(BAND_BLOCKS)[:, None, None]
    ok = jnp.logical_and(valid[None], started)[:, None]
    return jnp.where(ok, bias[None] * math.log2(math.e), NEG)


def kernel(x, ev_norm_mix, ev_w_in, ev_pool_w, ev_pool_scale, ev_q_norm, ev_w_q_up, ev_kv_norm, ev_w_kv_up, ev_w_out, ev_norm_ffn, ev_ffn_w_gate, ev_ffn_w_up, ev_ffn_w_down, od_norm_mix, od_w_in, od_rel_bias, od_sgu_norm_g, od_sgu_norm_b, od_sgu_w, od_sgu_b, od_w_out, od_norm_ffn, od_router, od_moe_w_gate, od_moe_w_up, od_moe_w_down, final_norm):
    batch, seq, d = x.shape
    tokens = batch * seq
    assert d == D_MODEL and seq % TM == 0 and TM % SGU_CHUNK == 0 and TM % TQ == 0 and seq % TK == 0
    depth = ev_w_in.shape[0] + od_w_in.shape[0]
    xt = x.reshape(tokens, d)
    tabs = _rope_tables(seq)
    tri = jnp.triu(jnp.ones((TM, TM), F32), k=1).astype(BF16)
    causal = jnp.tril(jnp.ones((SGU_CHUNK, SGU_CHUNK), bool))
    vec = lambda v: v.reshape(1, -1).astype(F32)
    moe_wg, moe_wu, moe_wd = (w.astype(BF16) for w in (od_moe_w_gate, od_moe_w_up, od_moe_w_down))

    for layer in range(depth):
        i = layer // 2
        if layer % 2 == 0:
            win, wq, wkv = _even_weights(ev_w_in[i], ev_w_q_up[i], ev_w_kv_up[i])
            a, q, k, v = _even_in(xt, vec(ev_norm_mix[i]), win, vec(ev_q_norm[i]), wq, vec(ev_kv_norm[i]), wkv,
                                  _v_ones(), tabs, seq)
            att = _mla_attention(q, k, v, batch, seq)
            xt = _even_out(xt, a, att, ev_pool_w[i].astype(BF16), vec(ev_pool_scale[i]), ev_w_out[i].astype(BF16),
                           vec(ev_norm_ffn[i]), ev_ffn_w_gate[i].astype(BF16), ev_ffn_w_up[i].astype(BF16),
                           ev_ffn_w_down[i].astype(BF16), seq)
        else:
            w_in = od_w_in[i]
            win = jnp.concatenate([w_in[:, :CA_WIDTH] * (CA_HEAD_DIM ** -0.5), w_in[:, CA_WIDTH:]], axis=1).astype(BF16)
            sw = jnp.where(causal[None], od_sgu_w[i], 0.0).astype(BF16)
            sb = jnp.broadcast_to(od_sgu_b[i][:, :, None], (SGU_GROUPS, SGU_CHUNK, SGU_CH)).astype(F32)
            qkv, sg = _odd_in(xt, vec(od_norm_mix[i]), win, vec(od_sgu_norm_g[i]), vec(od_sgu_norm_b[i]), sw, sb)
            att = _band_attention(qkv, _band_bias(od_rel_bias[i]), batch, seq)
            wr = jnp.concatenate([od_router[i], jnp.zeros((d, ROUTE_LANES - N_EXPERTS), F32)], axis=1)
            wr_hi = wr.astype(BF16)
            wr = jnp.stack([wr_hi, (wr - wr_hi.astype(F32)).astype(BF16)])
            x1, h, route, counts = _odd_out(xt, att, sg, od_w_out[i].astype(BF16), vec(od_norm_ffn[i]), wr, tri)
            tile_e, n_used, src, dst = _moe_plan(route, counts, tokens)
            o2 = _moe(h, tile_e, n_used, src, dst, moe_wg, moe_wu, moe_wd, i)
            last = layer == depth - 1
            xt = _combine(x1, o2, route, vec(final_norm), last)
    return xt.reshape(batch, seq, d)
```

```python
import functools
import math

import jax
import jax.numpy as jnp
from jax import lax
from jax.experimental import pallas as pl
from jax.experimental.pallas import tpu as pltpu

F32 = jnp.float32
BF16 = jnp.bfloat16
I32 = jnp.int32

D_MODEL = 1024
CHUNK = 64
EPS = 1e-6

POOL_GROUPS = 4
POOL_CH = 128
POOL_WIDTH = 512
POOL_WINDOWS = (2, 4, 8, 16)
POOL_HALO = 16

MLA_HEADS = 8
MLA_NOPE = 64
MLA_ROPE = 32
MLA_V = 64
MLA_Q_RANK = 256
MLA_KV_RANK = 128
ROPE_THETA = 10000.0
HEAD_PAD = 128

CA_HEADS = 8
CA_HEAD_DIM = 64
CA_LEFT_CHUNKS = 8
REL_CLIP = 128
CA_WIDTH = CA_HEADS * CA_HEAD_DIM

SGU_GROUPS = 4
SGU_CH = 128
SGU_WIDTH = 512
SGU_CHUNK = 128

D_FF = 2816
N_EXPERTS = 8
EXPERT_FF = 3584

NEG = -1e30
LANES = 128

TM = 512
TQ = 256
TK = 512
MLA_TQ = 512
MOE_TF = 512
MOE_FF_STEPS = EXPERT_FF // MOE_TF
MOE_TM = 80 * MOE_FF_STEPS
VMEM_LIMIT = 56 * 1024 * 1024


def _cparams(sem):
    return pltpu.CompilerParams(dimension_semantics=sem, vmem_limit_bytes=VMEM_LIMIT)


def _rms(x, g):
    return x * lax.rsqrt(jnp.mean(x * x, axis=-1, keepdims=True) + EPS) * g


def _dot(a, b):
    return jnp.dot(a, b, preferred_element_type=F32)


def _dot_nt(a, b):
    return lax.dot_general(a, b, (((1,), (1,)), ((), ())), preferred_element_type=F32)


TOKEN_TILE_ROWS = D_MODEL // LANES


def _store_token_tiles(ref, x):
    n = x.shape[0]
    for s in range(TOKEN_TILE_ROWS):
        ref[pl.ds(s, n, stride=TOKEN_TILE_ROWS), :] = x[:, s * LANES:(s + 1) * LANES]


def _load_token_tiles(ref, n):
    return jnp.concatenate([ref[pl.ds(s, n, stride=TOKEN_TILE_ROWS), :] for s in range(TOKEN_TILE_ROWS)], axis=1)


def _const_spec(shape):
    nd = len(shape)
    return pl.BlockSpec(shape, lambda *_: (0,) * nd, pipeline_mode=pl.Buffered(1))


def _even_in_body(x_ref, g_ref, win_ref, qn_ref, wq_ref, kvn_ref, wkv_ref, vones_ref, cq_ref, sq_ref, ck_ref, sk_ref,
                  a_ref, q_ref, k_ref, v_ref):
    h = _rms(x_ref[...], g_ref[...]).astype(BF16)
    z = _dot(h, win_ref[...])
    a_ref[...] = z[:, :POOL_WIDTH]
    o = POOL_WIDTH
    cq = z[:, o:o + MLA_Q_RANK]
    o += MLA_Q_RANK
    ckv = z[:, o:o + MLA_KV_RANK]
    o += MLA_KV_RANK
    kr = z[:, o:o + HEAD_PAD]
    kr_rot = z[:, o + HEAD_PAD:o + 2 * HEAD_PAD]

    hw = MLA_HEADS * HEAD_PAD
    qq = _dot(_rms(cq, qn_ref[...]).astype(BF16), wq_ref[...])
    cq_t, sq_t = cq_ref[...], sq_ref[...]
    for h_i in range(MLA_HEADS):
        sl = slice(h_i * HEAD_PAD, (h_i + 1) * HEAD_PAD)
        sl_rot = slice(hw + h_i * HEAD_PAD, hw + (h_i + 1) * HEAD_PAD)
        q_ref[:, sl] = (qq[:, sl] * cq_t + qq[:, sl_rot] * sq_t).astype(BF16)

    kv = _dot(_rms(ckv, kvn_ref[...]).astype(BF16), wkv_ref[...])
    k_rope = kr * ck_ref[...] + kr_rot * sk_ref[...]
    for h_i in range(MLA_HEADS):
        sl = slice(h_i * HEAD_PAD, (h_i + 1) * HEAD_PAD)
        k_ref[:, sl] = (kv[:, sl] + k_rope).astype(BF16)
    v_ref[...] = (kv[:, hw:] + vones_ref[...]).astype(BF16)


def _even_in(x, g, win, qn, wq, kvn, wkv, vones, tabs, seq):
    t = x.shape[0]
    hw = MLA_HEADS * HEAD_PAD
    row = lambda w: pl.BlockSpec((TM, w), lambda i: (i, 0))
    tab = pl.BlockSpec((TM, HEAD_PAD), lambda i: (i % (seq // TM), 0))
    return pl.pallas_call(
        _even_in_body,
        grid=(t // TM,),
        in_specs=[row(D_MODEL), _const_spec(g.shape), _const_spec(win.shape), _const_spec(qn.shape),
                  _const_spec(wq.shape), _const_spec(kvn.shape), _const_spec(wkv.shape), _const_spec(vones.shape),
                  tab, tab, tab, tab],
        out_specs=[row(POOL_WIDTH), row(hw), row(hw), row(hw)],
        out_shape=[jax.ShapeDtypeStruct((t, POOL_WIDTH), F32), jax.ShapeDtypeStruct((t, hw), BF16),
                   jax.ShapeDtypeStruct((t, hw), BF16), jax.ShapeDtypeStruct((t, hw), BF16)],
        compiler_params=_cparams(("parallel",)),
        name="even_in",
    )(x, g, win, qn, wq, kvn, wkv, vones, *tabs)


def _mla_body(q_ref, k_ref, v_ref, o_ref, s_ref, mx_ref, acc_ref):
    qi = pl.program_id(2)
    assert MLA_TQ == TK
    last = qi
    halves = (slice(0, HEAD_PAD), slice(HEAD_PAD, 2 * HEAD_PAD))
    lane_blocks = [slice(c * LANES, (c + 1) * LANES) for c in range(TK // LANES)]
    q_chunk = lax.broadcasted_iota(I32, (MLA_TQ, TK), 0) // CHUNK + qi * (MLA_TQ // CHUNK)
    k_chunk = lax.broadcasted_iota(I32, (MLA_TQ, TK), 1) // CHUNK + last * (TK // CHUNK)
    visible = k_chunk <= q_chunk

    mx_ref[...] = jnp.full(mx_ref.shape, NEG, F32)
    acc_ref[...] = jnp.zeros_like(acc_ref)

    def scores(j):
        rows = pl.ds(pl.multiple_of(j * TK, TK), TK)
        for hh, hs in enumerate(halves):
            s = _dot_nt(q_ref[:, hs], k_ref[rows, hs])
            s_ref[hh, j] = s
            mx_ref[hh] = functools.reduce(jnp.maximum, [s[:, lb] for lb in lane_blocks], mx_ref[hh])

    def pairs_then_rest(count, body):
        def two(jj, c):
            body(2 * jj)
            body(2 * jj + 1)
            return c

        lax.fori_loop(0, count // 2, two, 0)

        @pl.when(count % 2 == 1)
        def _():
            body(count - 1)

    half = MLA_TQ // 2
    top, bot = slice(0, half), slice(half, MLA_TQ)
    half_blocks = lane_blocks[:len(lane_blocks) // 2]
    diag_rows = pl.ds(pl.multiple_of(last * TK, TK), TK)
    diag_first_rows = pl.ds(pl.multiple_of(last * TK, TK), half)

    def diag_scores():
        for hh, hs in enumerate(halves):
            s_top = jnp.where(visible[top, :half], _dot_nt(q_ref[top, hs], k_ref[diag_first_rows, hs]), NEG)
            s_bot = jnp.where(visible[bot], _dot_nt(q_ref[bot, hs], k_ref[diag_rows, hs]), NEG)
            s_ref[hh, last, top, :half] = s_top
            s_ref[hh, last, bot, :] = s_bot
            mx_ref[hh, top] = functools.reduce(jnp.maximum, [s_top[:, lb] for lb in half_blocks], mx_ref[hh, top])
            mx_ref[hh, bot] = functools.reduce(jnp.maximum, [s_bot[:, lb] for lb in lane_blocks], mx_ref[hh, bot])

    pairs_then_rest(last, scores)
    diag_scores()

    for hh in range(2):
        m = jnp.max(mx_ref[hh], axis=-1, keepdims=True)
        mx_ref[hh] = jnp.broadcast_to(m, (MLA_TQ, LANES))

    def weights(hh, j, row_slice, blocks):
        m = mx_ref[hh, row_slice]
        return jnp.concatenate([jnp.exp2(s_ref[hh, j, row_slice, lb] - m) for lb in blocks], axis=1).astype(BF16)

    def weighted(j):
        rows = pl.ds(pl.multiple_of(j * TK, TK), TK)
        for hh, hs in enumerate(halves):
            acc_ref[hh] += _dot(weights(hh, j, slice(None), lane_blocks), v_ref[rows, hs])

    def diag_weighted():
        for hh, hs in enumerate(halves):
            acc_ref[hh, top] += _dot(weights(hh, last, top, half_blocks), v_ref[diag_first_rows, hs])
            acc_ref[hh, bot] += _dot(weights(hh, last, bot, lane_blocks), v_ref[diag_rows, hs])

    pairs_then_rest(last, weighted)
    diag_weighted()

    even, odd = acc_ref[0], acc_ref[1]
    lane = lax.broadcasted_iota(I32, (MLA_TQ, 2 * MLA_V), 1)
    out = jnp.where(lane < MLA_V, even / even[:, MLA_V:MLA_V + 1], odd / odd[:, 0:1])
    o_ref[...] = out.astype(BF16)


def _mla_attention(q, k, v, batch, seq):
    t = q.shape[0]
    nq = seq // MLA_TQ
    pairs = MLA_HEADS // 2
    return pl.pallas_call(
        _mla_body,
        grid=(batch, pairs, nq),
        in_specs=[pl.BlockSpec((MLA_TQ, 2 * HEAD_PAD), lambda b, p, i: (b * nq + i, p)),
                  pl.BlockSpec((seq, 2 * HEAD_PAD), lambda b, p, i: (b, p)),
                  pl.BlockSpec((seq, 2 * HEAD_PAD), lambda b, p, i: (b, p))],
        out_specs=pl.BlockSpec((MLA_TQ, 2 * MLA_V), lambda b, p, i: (b * nq + i, p)),
        out_shape=jax.ShapeDtypeStruct((t, MLA_HEADS * MLA_V), BF16),
        scratch_shapes=[pltpu.VMEM((2, seq // TK, MLA_TQ, TK), F32), pltpu.VMEM((2, MLA_TQ, LANES), F32),
                        pltpu.VMEM((2, MLA_TQ, HEAD_PAD), F32)],
        compiler_params=_cparams(("parallel", "parallel", "arbitrary")),
        name="mla_attention",
    )(q, k, v)


def _even_out_body(x_ref, a_ref, halo_ref, att_ref, pw_ref, ps_ref, wo_ref, g_ref, wg_ref, wu_ref, wd_ref,
                   o_ref, ext_ref, *, seq):
    pos0 = (pl.program_id(0) * TM) % seq
    ext_ref[0:POOL_HALO, :] = jnp.where(pos0 == 0, 0.0, halo_ref[...])
    ext_ref[POOL_HALO:, :] = a_ref[...]
    pos = lax.broadcasted_iota(I32, (TM, 1), 0) + pos0
    y = x_ref[...] + _dot(att_ref[...], wo_ref[POOL_WIDTH:, :])
    for g_i, win in enumerate(POOL_WINDOWS):
        gs = slice(g_i * POOL_CH, (g_i + 1) * POOL_CH)
        a_g = ext_ref[POOL_HALO:, gs]
        ssum = a_g
        for j in range(1, win):
            ssum = ssum + ext_ref[POOL_HALO - j:POOL_HALO - j + TM, gs]
        count = jnp.minimum(pos + 1, win).astype(F32)
        pooled = (ssum / count - a_g).astype(BF16)
        mixed = (_dot(pooled, pw_ref[g_i]) * ps_ref[:, gs]).astype(BF16)
        y = y + _dot(mixed, wo_ref[gs, :])
    h = _rms(y, g_ref[...]).astype(BF16)
    gate = _dot(h, wg_ref[...])
    up = _dot(h, wu_ref[...])
    act = (gate * jax.nn.sigmoid(gate) * up).astype(BF16)
    o_ref[...] = y + _dot(act, wd_ref[...])


def _even_out(x, a, att, pw, ps, wo, g, wg, wu, wd, seq):
    t = x.shape[0]
    row = lambda w: pl.BlockSpec((TM, w), lambda i: (i, 0))
    halo = pl.BlockSpec((POOL_HALO, POOL_WIDTH), lambda i: (jnp.maximum(i * (TM // POOL_HALO) - 1, 0), 0))
    return pl.pallas_call(
        functools.partial(_even_out_body, seq=seq),
        grid=(t // TM,),
        in_specs=[row(D_MODEL), row(POOL_WIDTH), halo, row(MLA_HEADS * MLA_V), _const_spec(pw.shape),
                  _const_spec(ps.shape), _const_spec(wo.shape), _const_spec(g.shape), _const_spec(wg.shape),
                  _const_spec(wu.shape), _const_spec(wd.shape)],
        out_specs=row(D_MODEL),
        out_shape=jax.ShapeDtypeStruct((t, D_MODEL), F32),
        scratch_shapes=[pltpu.VMEM((TM + POOL_HALO, POOL_WIDTH), F32)],
        compiler_params=_cparams(("parallel",)),
        name="even_out",
    )(x, a, a, att, pw, ps, wo, g, wg, wu, wd)


def _odd_in_body(x_ref, g_ref, win_ref, ng_ref, nb_ref, sw_ref, sb_ref, qkv_ref, sg_ref):
    h = _rms(x_ref[...], g_ref[...]).astype(BF16)
    uv = jax.nn.gelu(_dot(h, win_ref[:, 3 * CA_WIDTH:]))
    z = _dot(h, win_ref[:, :3 * CA_WIDTH])
    qkv_ref[:, :CA_WIDTH] = (z[:, :CA_WIDTH] * math.log2(math.e)).astype(BF16)
    qkv_ref[:, CA_WIDTH:] = z[:, CA_WIDTH:].astype(BF16)
    for g_i in range(SGU_GROUPS):
        gs = slice(g_i * SGU_CH, (g_i + 1) * SGU_CH)
        u = uv[:, gs]
        gv = uv[:, SGU_WIDTH + g_i * SGU_CH:SGU_WIDTH + (g_i + 1) * SGU_CH]
        mu = jnp.mean(gv, axis=-1, keepdims=True)
        xc = gv - mu
        vn = xc * lax.rsqrt(jnp.mean(xc * xc, axis=-1, keepdims=True) + EPS) * ng_ref[:, gs] + nb_ref[:, gs]
        vn = vn.astype(BF16)
        w = sw_ref[g_i]
        for c_i in range(TM // SGU_CHUNK):
            rs = slice(c_i * SGU_CHUNK, (c_i + 1) * SGU_CHUNK)
            mixed = _dot(w, vn[rs, :]) + sb_ref[g_i]
            sg_ref[rs, gs] = (u[rs, :] * mixed).astype(BF16)


def _odd_in(x, g, win, ng, nb, sw, sb):
    t = x.shape[0]
    row = lambda w: pl.BlockSpec((TM, w), lambda i: (i, 0))
    return pl.pallas_call(
        _odd_in_body,
        grid=(t // TM,),
        in_specs=[row(D_MODEL), _const_spec(g.shape), _const_spec(win.shape), _const_spec(ng.shape),
                  _const_spec(nb.shape), _const_spec(sw.shape), _const_spec(sb.shape)],
        out_specs=[row(3 * CA_WIDTH), row(SGU_WIDTH)],
        out_shape=[jax.ShapeDtypeStruct((t, 3 * CA_WIDTH), BF16), jax.ShapeDtypeStruct((t, SGU_WIDTH), BF16)],
        compiler_params=_cparams(("parallel",)),
        name="odd_in",
    )(x, g, win, ng, nb, sw, sb)


BAND_BLOCKS = 3


def _band_body(q_ref, k0_ref, k1_ref, k2_ref, v0_ref, v1_ref, v2_ref, bias_ref, o_ref):
    pair_w = 2 * CA_HEAD_DIM
    low = lax.broadcasted_iota(I32, (TQ, pair_w), 1) < CA_HEAD_DIM
    k_refs = (k0_ref, k1_ref, k2_ref)
    v_refs = (v0_ref, v1_ref, v2_ref)
    add = lambda a, b: a + b
    per_tile = TQ // LANES
    for pair in range(CA_HEADS // 2):
        ps = slice(pair * pair_w, (pair + 1) * pair_w)
        q_pair = q_ref[:, ps]
        outs = []
        for hh in range(2):
            q = jnp.where(low if hh == 0 else jnp.logical_not(low), q_pair, jnp.zeros_like(q_pair))
            s = []
            for jj in range(BAND_BLOCKS):
                sj = _dot_nt(q, k_refs[jj][:, ps]) + bias_ref[2 * pair + hh, :, jj * TQ:(jj + 1) * TQ]
                s += [sj[:, c * LANES:(c + 1) * LANES] for c in range(per_tile)]
            m = jnp.max(functools.reduce(jnp.maximum, s), axis=-1, keepdims=True)
            mb = jnp.broadcast_to(m, (TQ, LANES))
            p = [jnp.exp2(x - mb) for x in s]
            l = jnp.sum(functools.reduce(add, p), axis=-1, keepdims=True)
            acc = functools.reduce(add, [
                _dot(jnp.concatenate(p[jj * per_tile:(jj + 1) * per_tile], axis=1).astype(BF16), v_refs[jj][:, ps])
                for jj in range(BAND_BLOCKS)])
            outs.append(acc / l)
        o_ref[:, ps] = jnp.where(low, outs[0], outs[1]).astype(BF16)


def _band_attention(qkv, bias, batch, seq):
    t = qkv.shape[0]
    nq = seq // TQ

    def kv_spec(jj, col):
        return pl.BlockSpec((TQ, CA_WIDTH), lambda b, i: (b * nq + jnp.maximum(i - (BAND_BLOCKS - 1) + jj, 0), col))

    return pl.pallas_call(
        _band_body,
        grid=(batch, nq),
        in_specs=[pl.BlockSpec((TQ, CA_WIDTH), lambda b, i: (b * nq + i, 0))]
                 + [kv_spec(jj, 1) for jj in range(BAND_BLOCKS)]
                 + [kv_spec(jj, 2) for jj in range(BAND_BLOCKS)]
                 + [pl.BlockSpec((None, CA_HEADS, TQ, BAND_BLOCKS * TQ),
                                 lambda b, i: (jnp.minimum(i, BAND_BLOCKS - 1), 0, 0, 0))],
        out_specs=pl.BlockSpec((TQ, CA_WIDTH), lambda b, i: (b * nq + i, 0)),
        out_shape=jax.ShapeDtypeStruct((t, CA_WIDTH), BF16),
        compiler_params=_cparams(("parallel", "arbitrary")),
        name="band_attention",
    )(qkv, qkv, qkv, qkv, qkv, qkv, qkv, bias)


ROUTE_LANES = 128
ROUTE_ROWS = 8


def _odd_out_body(x_ref, att_ref, sg_ref, wo_ref, g_ref, wr_ref, tri_ref, x1_ref, h_ref, route_ref, cnt_ref, run_ref):
    @pl.when(pl.program_id(0) == 0)
    def _():
        run_ref[...] = jnp.zeros_like(run_ref)

    y = x_ref[...] + _dot(att_ref[...], wo_ref[:CA_WIDTH, :]) + _dot(sg_ref[...], wo_ref[CA_WIDTH:, :])
    x1_ref[...] = y
    h = _rms(y, g_ref[...])
    _store_token_tiles(h_ref, h)
    h_hi = h.astype(BF16)
    h_lo = (h - h_hi.astype(F32)).astype(BF16)
    logits = _dot(h_hi, wr_ref[0]) + (_dot(h_lo, wr_ref[0]) + _dot(h_hi, wr_ref[1]))
    lt = logits.T[:N_EXPERTS]
    row = lax.broadcasted_iota(I32, (N_EXPERTS, TM), 0)
    m1 = jnp.max(lt, axis=0, keepdims=True)
    e1 = jnp.min(jnp.where(lt == m1, row, N_EXPERTS), axis=0, keepdims=True)
    rest = jnp.where(row == e1, NEG, lt)
    m2 = jnp.max(rest, axis=0, keepdims=True)
    e2 = jnp.min(jnp.where(rest == m2, row, N_EXPERTS), axis=0, keepdims=True)
    ex = jnp.exp(m2 - m1)
    w1 = 1.0 / (1.0 + ex)
    w2 = ex / (1.0 + ex)
    hot1 = (row == e1).astype(F32)
    hot2 = (row == e2).astype(F32)
    sel = hot1 + hot2
    before = _dot(sel.astype(BF16), tri_ref[...]) + run_ref[:, 0:1]
    rank1 = jnp.sum(hot1 * before, axis=0, keepdims=True)
    rank2 = jnp.sum(hot2 * before, axis=0, keepdims=True)
    unused = jnp.zeros((ROUTE_ROWS - 6, TM), F32)
    route_ref[...] = jnp.concatenate([e1.astype(F32), e2.astype(F32), rank1, rank2, w1, w2, unused], axis=0)
    total = run_ref[:, 0:1] + jnp.sum(sel, axis=1, keepdims=True)
    run_ref[...] = jnp.broadcast_to(total, run_ref.shape)
    cnt_ref[...] = jnp.broadcast_to(total, cnt_ref.shape)


def _odd_out(x, att, sg, wo, g, wr, tri):
    t = x.shape[0]
    row = lambda w: pl.BlockSpec((TM, w), lambda i: (i, 0))
    return pl.pallas_call(
        _odd_out_body,
        grid=(t // TM,),
        in_specs=[row(D_MODEL), row(CA_WIDTH), row(SGU_WIDTH), _const_spec(wo.shape), _const_spec(g.shape),
                  _const_spec(wr.shape), _const_spec(tri.shape)],
        out_specs=[row(D_MODEL), pl.BlockSpec((TM * TOKEN_TILE_ROWS, LANES), lambda i: (i, 0)),
                   pl.BlockSpec((ROUTE_ROWS, TM), lambda i: (0, i)),
                   pl.BlockSpec((N_EXPERTS, ROUTE_LANES), lambda i: (0, 0))],
        out_shape=[jax.ShapeDtypeStruct((t, D_MODEL), F32), jax.ShapeDtypeStruct((t * TOKEN_TILE_ROWS, LANES), F32),
                   jax.ShapeDtypeStruct((ROUTE_ROWS, t), F32), jax.ShapeDtypeStruct((N_EXPERTS, ROUTE_LANES), F32)],
        scratch_shapes=[pltpu.VMEM((N_EXPERTS, ROUTE_LANES), F32)],
        compiler_params=_cparams(("arbitrary",)),
        name="odd_out_router",
    )(x, att, sg, wo, g, wr, tri)


def _moe_body(tile_e_ref, n_used_ref, src_ref, dst_ref, h_hbm, wg_ref, wu_ref, wd_ref, o_hbm,
              xf_ref, xb_ref, acc_ref, y_ref, gsem, ssem):
    i = pl.program_id(0)
    j = pl.program_id(1)
    n_used = n_used_ref[0]
    rows_per_step = MOE_TM // MOE_FF_STEPS

    ttr = TOKEN_TILE_ROWS
    tile_rows = MOE_TM * ttr

    def token_rows(ref, first_row):
        return ref.at[pl.ds(pl.multiple_of(first_row, ttr), ttr), :]

    def gather_copy(tile, r, slot):
        return pltpu.make_async_copy(token_rows(h_hbm, src_ref[tile * MOE_TM + r]),
                                     token_rows(xf_ref.at[slot], r * ttr), gsem)

    def scatter_copy(tile, r, slot):
        return pltpu.make_async_copy(token_rows(y_ref.at[slot], r * ttr),
                                     token_rows(o_hbm, dst_ref[tile * MOE_TM + r]), ssem)

    def wait_tile_gather(slot):
        pltpu.make_async_copy(h_hbm.at[pl.ds(0, tile_rows), :], xf_ref.at[slot], gsem).wait()

    def wait_tile_scatter(slot):
        pltpu.make_async_copy(y_ref.at[slot], o_hbm.at[pl.ds(0, tile_rows), :], ssem).wait()

    def start_all(copy_fn):
        lax.fori_loop(0, MOE_TM, lambda r, c: (copy_fn(r).start(), c)[1], 0)

    @pl.when(i < n_used)
    def _():
        cur = lax.rem(i, 2)
        other = 1 - cur
        nxt_tile = jnp.minimum(i + 1, n_used - 1)
        prev_tile = jnp.maximum(i - 1, 0)

        @pl.when(jnp.logical_and(i == 0, j == 0))
        def _():
            y_ref[...] = jnp.zeros_like(y_ref)
            acc_ref[...] = jnp.zeros_like(acc_ref)
            spare0 = o_hbm.shape[0] - N_EXPERTS * tile_rows
            for e_i in range(N_EXPERTS):
                fill = pltpu.make_async_copy(y_ref.at[0], o_hbm.at[pl.ds(spare0 + e_i * tile_rows, tile_rows), :],
                                             ssem)
                fill.start()
                fill.wait()
            start_all(lambda r: gather_copy(0, r, 0))

        @pl.when(j == 0)
        def _():
            wait_tile_gather(cur)
            xb_ref[...] = _load_token_tiles(xf_ref.at[cur], MOE_TM).astype(BF16)

        base = pl.multiple_of(j * rows_per_step, 8)
        for rr in range(rows_per_step):
            gather_copy(nxt_tile, base + rr, other).start()
            scatter_copy(prev_tile, base + rr, other).start()
        xb = xb_ref[...]
        gate = _dot(xb, wg_ref[...])
        up = _dot(xb, wu_ref[...])
        act = (gate * jax.nn.sigmoid(gate) * up).astype(BF16)
        acc_ref[...] = jnp.where(j == 0, 0.0, acc_ref[...]) + _dot(act, wd_ref[...])

        @pl.when(j == MOE_FF_STEPS - 1)
        def _():
            wait_tile_scatter(other)
            _store_token_tiles(y_ref.at[cur], acc_ref[...])

        @pl.when(jnp.logical_and(i == n_used - 1, j == MOE_FF_STEPS - 1))
        def _():
            start_all(lambda r: scatter_copy(i, r, cur))
            wait_tile_scatter(cur)
            wait_tile_gather(other)


def _moe(h, tile_e, n_used, src, dst, wg, wu, wd, layer):
    tokens = h.shape[0] // TOKEN_TILE_ROWS
    n_tiles = tile_e.shape[0]
    last = MOE_FF_STEPS - 1

    def ff_idx(i, j, n_used_ref):
        return jnp.where(i < n_used_ref[0], j, last)

    grid_spec = pltpu.PrefetchScalarGridSpec(
        num_scalar_prefetch=4,
        grid=(n_tiles, MOE_FF_STEPS),
        in_specs=[pl.BlockSpec(memory_space=pl.ANY),
                  pl.BlockSpec((None, None, D_MODEL, MOE_TF),
                               lambda i, j, te, nu, sr, ds: (layer, te[i], 0, ff_idx(i, j, nu))),
                  pl.BlockSpec((None, None, D_MODEL, MOE_TF),
                               lambda i, j, te, nu, sr, ds: (layer, te[i], 0, ff_idx(i, j, nu))),
                  pl.BlockSpec((None, None, MOE_TF, D_MODEL),
                               lambda i, j, te, nu, sr, ds: (layer, te[i], ff_idx(i, j, nu), 0))],
        out_specs=pl.BlockSpec(memory_space=pl.ANY),
        scratch_shapes=[pltpu.VMEM((2, MOE_TM * TOKEN_TILE_ROWS, LANES), F32), pltpu.VMEM((MOE_TM, D_MODEL), BF16),
                        pltpu.VMEM((MOE_TM, D_MODEL), F32), pltpu.VMEM((2, MOE_TM * TOKEN_TILE_ROWS, LANES), F32),
                        pltpu.SemaphoreType.DMA, pltpu.SemaphoreType.DMA],
    )
    return pl.pallas_call(
        _moe_body,
        grid_spec=grid_spec,
        out_shape=jax.ShapeDtypeStruct(((2 * tokens + N_EXPERTS * MOE_TM) * TOKEN_TILE_ROWS, LANES), F32),
        compiler_params=_cparams(("arbitrary", "arbitrary")),
        name="moe_experts",
    )(tile_e, n_used, src, dst, h, wg, wu, wd)


def _row_order_body(pos_ref, cnt_ref, tile_end_ref, dst_ref, *, tokens):
    spare0 = 2 * tokens
    first_tile = 0
    for e_i in range(N_EXPERTS):
        end_tile = tile_end_ref[e_i]
        end_row = end_tile * MOE_TM
        spare_base = spare0 + e_i * MOE_TM - (end_row - MOE_TM)

        def pad(p, c, spare_base=spare_base):
            dst_ref[p] = spare_base + p
            return c

        lax.fori_loop(first_tile * MOE_TM + cnt_ref[e_i], end_row, pad, 0)
        first_tile = end_tile

    def unused(p, c):
        dst_ref[p] = spare0
        return c

    lax.fori_loop(first_tile * MOE_TM, dst_ref.shape[0], unused, 0)

    unroll = 16

    def place(block, c):
        for u in range(unroll):
            a = block * unroll + u
            dst_ref[pos_ref[a]] = a
        return c

    lax.fori_loop(0, 2 * tokens // unroll, place, 0)


def _row_order(pos, cnt, tile_end, tokens, n_tiles):
    smem = pl.BlockSpec(memory_space=pltpu.SMEM)
    return pl.pallas_call(
        functools.partial(_row_order_body, tokens=tokens),
        in_specs=[smem, smem, smem],
        out_specs=smem,
        out_shape=jax.ShapeDtypeStruct((n_tiles * MOE_TM,), I32),
        name="moe_row_order",
    )(pos, cnt, tile_end)


def _moe_plan(route, counts, tokens):
    n_tiles = (2 * tokens + N_EXPERTS * (MOE_TM - 1)) // MOE_TM
    experts = jnp.arange(N_EXPERTS, dtype=I32)
    cnt = counts[:, 0].astype(I32)
    e = route[0:2].astype(I32)
    rank = route[2:4].astype(I32)
    tiles_per_e = (cnt + MOE_TM - 1) // MOE_TM
    tile_end = jnp.cumsum(tiles_per_e)
    tile_start = tile_end - tiles_per_e
    row_start = jnp.sum(jnp.where(e[..., None] == experts, tile_start * MOE_TM, 0), axis=-1)
    pos = row_start + rank
    n_used = tile_end[-1]
    tile = jnp.minimum(jnp.arange(n_tiles, dtype=I32), n_used - 1)
    tile_e = jnp.sum((tile_end[None, :] <= tile[:, None]).astype(I32), axis=1)
    dst = _row_order(pos.reshape(-1), cnt, tile_end, tokens, n_tiles)
    src = jnp.where(dst < 2 * tokens, dst % tokens, 0)
    return tile_e, n_used.reshape(1), src * TOKEN_TILE_ROWS, dst * TOKEN_TILE_ROWS


def _combine_body(x_ref, oa_ref, ob_ref, gate_ref, g_ref, o_ref, *, final):
    w1 = gate_ref[:, 0:1]
    w2 = gate_ref[:, 1:2]
    y = x_ref[...] + (w1 * _load_token_tiles(oa_ref, TM) + w2 * _load_token_tiles(ob_ref, TM))
    o_ref[...] = _rms(y, g_ref[...]) if final else y


def _combine(x, o2, route, g, final):
    t = x.shape[0]
    nt = t // TM
    gates = route[4:6].T
    return pl.pallas_call(
        functools.partial(_combine_body, final=final),
        grid=(nt,),
        in_specs=[pl.BlockSpec((TM, D_MODEL), lambda i: (i, 0)),
                  pl.BlockSpec((TM * TOKEN_TILE_ROWS, LANES), lambda i: (i, 0)),
                  pl.BlockSpec((TM * TOKEN_TILE_ROWS, LANES), lambda i: (i + nt, 0)),
                  pl.BlockSpec((TM, 2), lambda i: (i, 0)), _const_spec(g.shape)],
        out_specs=pl.BlockSpec((TM, D_MODEL), lambda i: (i, 0)),
        out_shape=jax.ShapeDtypeStruct((t, D_MODEL), F32),
        compiler_params=_cparams(("parallel",)),
        name="moe_combine",
    )(x, o2, o2, gates, g)


def _rot_half(w):
    half = w.shape[-1] // 2
    return jnp.concatenate([-w[..., half:], w[..., :half]], axis=-1)


def _rope_tables(seq):
    inv = 1.0 / (ROPE_THETA ** (jnp.arange(0, MLA_ROPE, 2, dtype=F32) / MLA_ROPE))
    ang = jnp.arange(seq, dtype=F32)[:, None] * inv[None, :]
    cos, sin = jnp.cos(ang), jnp.sin(ang)
    pad = HEAD_PAD - MLA_NOPE - MLA_ROPE
    c = jnp.concatenate([jnp.ones((seq, MLA_NOPE), F32), cos, cos, jnp.zeros((seq, pad), F32)], axis=1)
    s = jnp.concatenate([jnp.zeros((seq, MLA_NOPE), F32), sin, sin, jnp.zeros((seq, pad), F32)], axis=1)
    scale = (MLA_NOPE + MLA_ROPE) ** -0.5 * math.log2(math.e)
    return c * scale, s * scale, c, s


def _even_weights(w_in, w_q_up, w_kv_up):
    d = w_in.shape[0]
    pad = HEAD_PAD - MLA_NOPE - MLA_ROPE
    kr = w_in[:, POOL_WIDTH + MLA_Q_RANK + MLA_KV_RANK:]
    place = lambda m: jnp.concatenate([jnp.zeros((d, MLA_NOPE), F32), m, jnp.zeros((d, pad), F32)], axis=1)
    win = jnp.concatenate([w_in[:, :POOL_WIDTH + MLA_Q_RANK + MLA_KV_RANK], place(kr), place(_rot_half(kr))], axis=1)
    wq = w_q_up.reshape(MLA_Q_RANK, MLA_HEADS, MLA_NOPE + MLA_ROPE)
    nope, rope = wq[..., :MLA_NOPE], wq[..., MLA_NOPE:]
    zq = lambda n: jnp.zeros((MLA_Q_RANK, MLA_HEADS, n), F32)
    wq_main = jnp.concatenate([nope, rope, zq(pad)], axis=-1).reshape(MLA_Q_RANK, -1)
    wq_rot = jnp.concatenate([zq(MLA_NOPE), _rot_half(rope), zq(pad)], axis=-1).reshape(MLA_Q_RANK, -1)
    wkv = w_kv_up.reshape(MLA_KV_RANK, MLA_HEADS, MLA_NOPE + MLA_V)
    k_pad = jnp.concatenate([wkv[..., :MLA_NOPE], jnp.zeros((MLA_KV_RANK, MLA_HEADS, HEAD_PAD - MLA_NOPE), F32)],
                            axis=-1).reshape(MLA_KV_RANK, -1)
    v_pairs = wkv[..., MLA_NOPE:].reshape(MLA_KV_RANK, MLA_HEADS // 2, 2, MLA_V)
    v_gap = jnp.zeros((MLA_KV_RANK, MLA_HEADS // 2, 2 * (HEAD_PAD - MLA_V)), F32)
    v_ext = jnp.concatenate([v_pairs[:, :, 0], v_gap, v_pairs[:, :, 1]], axis=-1).reshape(MLA_KV_RANK, -1)
    return (win.astype(BF16), jnp.concatenate([wq_main, wq_rot], axis=1).astype(BF16),
            jnp.concatenate([k_pad, v_ext], axis=1).astype(BF16))


def _v_ones():
    lane = jnp.arange(MLA_HEADS * HEAD_PAD) % (2 * HEAD_PAD)
    return jnp.logical_and(lane >= MLA_V, lane < 2 * HEAD_PAD - MLA_V).astype(F32).reshape(1, -1)


def _band_bias(rel_bias):
    band = BAND_BLOCKS * TQ
    qpos = jnp.arange(TQ)[:, None] + (BAND_BLOCKS - 1) * TQ
    kpos = jnp.arange(band)[None, :]
    span = band + TQ - 1
    rel = (BAND_BLOCKS - 1) * TQ + (TQ - 1) - jnp.arange(span)
    by_rel = rel_bias[:, jnp.clip(rel, -REL_CLIP, REL_CLIP) + REL_CLIP].astype(F32)
    rolled = jnp.roll(by_rel, -(TQ - 1), axis=1)
    bias = jnp.tile(rolled, (1, TQ))[:, :TQ * (span - 1)].reshape(-1, TQ, span - 1)[:, :, :band]
    dc = qpos // CHUNK - kpos // CHUNK
    valid = jnp.logical_and(dc >= 0, dc <= CA_LEFT_CHUNKS)
    started = kpos // TQ >= (BAND_BLOCKS - 1) - jnp.arange(BAND_BLOCKS)[:, None, None]
    ok = jnp.logical_and(valid[None], started)[:, None]
    return jnp.where(ok, bias[None] * math.log2(math.e), NEG)


def kernel(x, ev_norm_mix, ev_w_in, ev_pool_w, ev_pool_scale, ev_q_norm, ev_w_q_up, ev_kv_norm, ev_w_kv_up, ev_w_out, ev_norm_ffn, ev_ffn_w_gate, ev_ffn_w_up, ev_ffn_w_down, od_norm_mix, od_w_in, od_rel_bias, od_sgu_norm_g, od_sgu_norm_b, od_sgu_w, od_sgu_b, od_w_out, od_norm_ffn, od_router, od_moe_w_gate, od_moe_w_up, od_moe_w_down, final_norm):
    batch, seq, d = x.shape
    tokens = batch * seq
    assert d == D_MODEL and seq % TM == 0 and TM % SGU_CHUNK == 0 and TM % TQ == 0 and seq % TK == 0
    depth = ev_w_in.shape[0] + od_w_in.shape[0]
    xt = x.reshape(tokens, d)
    tabs = _rope_tables(seq)
    tri = jnp.triu(jnp.ones((TM, TM), F32), k=1).astype(BF16)
    causal = jnp.tril(jnp.ones((SGU_CHUNK, SGU_CHUNK), bool))
    vec = lambda v: v.reshape(1, -1).astype(F32)
    moe_wg, moe_wu, moe_wd = (w.astype(BF16) for w in (od_moe_w_gate, od_moe_w_up, od_moe_w_down))

    for layer in range(depth):
        i = layer // 2
        if layer % 2 == 0:
            win, wq, wkv = _even_weights(ev_w_in[i], ev_w_q_up[i], ev_w_kv_up[i])
            a, q, k, v = _even_in(xt, vec(ev_norm_mix[i]), win, vec(ev_q_norm[i]), wq, vec(ev_kv_norm[i]), wkv,
                                  _v_ones(), tabs, seq)
            att = _mla_attention(q, k, v, batch, seq)
            xt = _even_out(xt, a, att, ev_pool_w[i].astype(BF16), vec(ev_pool_scale[i]), ev_w_out[i].astype(BF16),
                           vec(ev_norm_ffn[i]), ev_ffn_w_gate[i].astype(BF16), ev_ffn_w_up[i].astype(BF16),
                           ev_ffn_w_down[i].astype(BF16), seq)
        else:
            w_in = od_w_in[i]
            win = jnp.concatenate([w_in[:, :CA_WIDTH] * (CA_HEAD_DIM ** -0.5), w_in[:, CA_WIDTH:]], axis=1).astype(BF16)
            sw = jnp.where(causal[None], od_sgu_w[i], 0.0).astype(BF16)
            sb = jnp.broadcast_to(od_sgu_b[i][:, :, None], (SGU_GROUPS, SGU_CHUNK, SGU_CH)).astype(F32)
            qkv, sg = _odd_in(xt, vec(od_norm_mix[i]), win, vec(od_sgu_norm_g[i]), vec(od_sgu_norm_b[i]), sw, sb)
            att = _band_attention(qkv, _band_bias(od_rel_bias[i]), batch, seq)
            wr = jnp.concatenate([od_router[i], jnp.zeros((d, ROUTE_LANES - N_EXPERTS), F32)], axis=1)
            wr_hi = wr.astype(BF16)
            wr = jnp.stack([wr_hi, (wr - wr_hi.astype(F32)).astype(BF16)])
            x1, h, route, counts = _odd_out(xt, att, sg, od_w_out[i].astype(BF16), vec(od_norm_ffn[i]), wr, tri)
            tile_e, n_used, src, dst = _moe_plan(route, counts, tokens)
            o2 = _moe(h, tile_e, n_used, src, dst, moe_wg, moe_wu, moe_wd, i)
            last = layer == depth - 1
            xt = _combine(x1, o2, route, vec(final_norm), last)
    return xt.reshape(batch, seq, d)
```

```python
import functools
import math

import jax
import jax.numpy as jnp
from jax import lax
from jax.experimental import pallas as pl
from jax.experimental.pallas import tpu as pltpu

F32 = jnp.float32
BF16 = jnp.bfloat16
I32 = jnp.int32

D_MODEL = 1024
CHUNK = 64
EPS = 1e-6

POOL_GROUPS = 4
POOL_CH = 128
POOL_WIDTH = 512
POOL_WINDOWS = (2, 4, 8, 16)
POOL_HALO = 16

MLA_HEADS = 8
MLA_NOPE = 64
MLA_ROPE = 32
MLA_V = 64
MLA_Q_RANK = 256
MLA_KV_RANK = 128
ROPE_THETA = 10000.0
HEAD_PAD = 128

CA_HEADS = 8
CA_HEAD_DIM = 64
CA_LEFT_CHUNKS = 8
REL_CLIP = 128
CA_WIDTH = CA_HEADS * CA_HEAD_DIM

SGU_GROUPS = 4
SGU_CH = 128
SGU_WIDTH = 512
SGU_CHUNK = 128

D_FF = 2816
N_EXPERTS = 8
EXPERT_FF = 3584

NEG = -1e30
LANES = 128

TM = 512
TQ = 256
TK = 512
MLA_TQ = 512
MOE_TF = 512
MOE_FF_STEPS = EXPERT_FF // MOE_TF
MOE_TM = 80 * MOE_FF_STEPS
VMEM_LIMIT = 56 * 1024 * 1024


def _cparams(sem):
    return pltpu.CompilerParams(dimension_semantics=sem, vmem_limit_bytes=VMEM_LIMIT)


def _rms(x, g):
    return x * lax.rsqrt(jnp.mean(x * x, axis=-1, keepdims=True) + EPS) * g


def _dot(a, b):
    return jnp.dot(a, b, preferred_element_type=F32)


def _dot_nt(a, b):
    return lax.dot_general(a, b, (((1,), (1,)), ((), ())), preferred_element_type=F32)


TOKEN_TILE_ROWS = D_MODEL // LANES


def _store_token_tiles(ref, x):
    n = x.shape[0]
    for s in range(TOKEN_TILE_ROWS):
        ref[pl.ds(s, n, stride=TOKEN_TILE_ROWS), :] = x[:, s * LANES:(s + 1) * LANES]


def _load_token_tiles(ref, n):
    return jnp.concatenate([ref[pl.ds(s, n, stride=TOKEN_TILE_ROWS), :] for s in range(TOKEN_TILE_ROWS)], axis=1)


def _const_spec(shape):
    nd = len(shape)
    return pl.BlockSpec(shape, lambda *_: (0,) * nd, pipeline_mode=pl.Buffered(1))


def _even_in_body(x_ref, g_ref, win_ref, qn_ref, wq_ref, kvn_ref, wkv_ref, vones_ref, cq_ref, sq_ref, ck_ref, sk_ref,
                  a_ref, q_ref, k_ref, v_ref):
    h = _rms(x_ref[...], g_ref[...]).astype(BF16)
    z = _dot(h, win_ref[...])
    a_ref[...] = z[:, :POOL_WIDTH]
    o = POOL_WIDTH
    cq = z[:, o:o + MLA_Q_RANK]
    o += MLA_Q_RANK
    ckv = z[:, o:o + MLA_KV_RANK]
    o += MLA_KV_RANK
    kr = z[:, o:o + HEAD_PAD]
    kr_rot = z[:, o + HEAD_PAD:o + 2 * HEAD_PAD]

    hw = MLA_HEADS * HEAD_PAD
    qq = _dot(_rms(cq, qn_ref[...]).astype(BF16), wq_ref[...])
    cq_t, sq_t = cq_ref[...], sq_ref[...]
    for h_i in range(MLA_HEADS):
        sl = slice(h_i * HEAD_PAD, (h_i + 1) * HEAD_PAD)
        sl_rot = slice(hw + h_i * HEAD_PAD, hw + (h_i + 1) * HEAD_PAD)
        q_ref[:, sl] = (qq[:, sl] * cq_t + qq[:, sl_rot] * sq_t).astype(BF16)

    kv = _dot(_rms(ckv, kvn_ref[...]).astype(BF16), wkv_ref[...])
    k_rope = kr * ck_ref[...] + kr_rot * sk_ref[...]
    for h_i in range(MLA_HEADS):
        sl = slice(h_i * HEAD_PAD, (h_i + 1) * HEAD_PAD)
        k_ref[:, sl] = (kv[:, sl] + k_rope).astype(BF16)
    v_ref[...] = (kv[:, hw:] + vones_ref[...]).astype(BF16)


def _even_in(x, g, win, qn, wq, kvn, wkv, vones, tabs, seq):
    t = x.shape[0]
    hw = MLA_HEADS * HEAD_PAD
    row = lambda w: pl.BlockSpec((TM, w), lambda i: (i, 0))
    tab = pl.BlockSpec((TM, HEAD_PAD), lambda i: (i % (seq // TM), 0))
    return pl.pallas_call(
        _even_in_body,
        grid=(t // TM,),
        in_specs=[row(D_MODEL), _const_spec(g.shape), _const_spec(win.shape), _const_spec(qn.shape),
                  _const_spec(wq.shape), _const_spec(kvn.shape), _const_spec(wkv.shape), _const_spec(vones.shape),
                  tab, tab, tab, tab],
        out_specs=[row(POOL_WIDTH), row(hw), row(hw), row(hw)],
        out_shape=[jax.ShapeDtypeStruct((t, POOL_WIDTH), F32), jax.ShapeDtypeStruct((t, hw), BF16),
                   jax.ShapeDtypeStruct((t, hw), BF16), jax.ShapeDtypeStruct((t, hw), BF16)],
        compiler_params=_cparams(("parallel",)),
        name="even_in",
    )(x, g, win, qn, wq, kvn, wkv, vones, *tabs)


def _mla_body(q_ref, k_ref, v_ref, o_ref, s_ref, mx_ref, acc_ref):
    qi = pl.program_id(2)
    assert MLA_TQ == TK
    last = qi
    halves = (slice(0, HEAD_PAD), slice(HEAD_PAD, 2 * HEAD_PAD))
    lane_blocks = [slice(c * LANES, (c + 1) * LANES) for c in range(TK // LANES)]
    q_chunk = lax.broadcasted_iota(I32, (MLA_TQ, TK), 0) // CHUNK + qi * (MLA_TQ // CHUNK)
    k_chunk = lax.broadcasted_iota(I32, (MLA_TQ, TK), 1) // CHUNK + last * (TK // CHUNK)
    visible = k_chunk <= q_chunk

    mx_ref[...] = jnp.full(mx_ref.shape, NEG, F32)
    acc_ref[...] = jnp.zeros_like(acc_ref)

    def scores(j):
        rows = pl.ds(pl.multiple_of(j * TK, TK), TK)
        for hh, hs in enumerate(halves):
            s = _dot_nt(q_ref[:, hs], k_ref[rows, hs])
            s_ref[hh, j] = s
            mx_ref[hh] = functools.reduce(jnp.maximum, [s[:, lb] for lb in lane_blocks], mx_ref[hh])

    def pairs_then_rest(count, body):
        def two(jj, c):
            body(2 * jj)
            body(2 * jj + 1)
            return c

        lax.fori_loop(0, count // 2, two, 0)

        @pl.when(count % 2 == 1)
        def _():
            body(count - 1)

    half = MLA_TQ // 2
    top, bot = slice(0, half), slice(half, MLA_TQ)
    half_blocks = lane_blocks[:len(lane_blocks) // 2]
    diag_rows = pl.ds(pl.multiple_of(last * TK, TK), TK)
    diag_first_rows = pl.ds(pl.multiple_of(last * TK, TK), half)

    def diag_scores():
        for hh, hs in enumerate(halves):
            s_top = jnp.where(visible[top, :half], _dot_nt(q_ref[top, hs], k_ref[diag_first_rows, hs]), NEG)
            s_bot = jnp.where(visible[bot], _dot_nt(q_ref[bot, hs], k_ref[diag_rows, hs]), NEG)
            s_ref[hh, last, top, :half] = s_top
            s_ref[hh, last, bot, :] = s_bot
            mx_ref[hh, top] = functools.reduce(jnp.maximum, [s_top[:, lb] for lb in half_blocks], mx_ref[hh, top])
            mx_ref[hh, bot] = functools.reduce(jnp.maximum, [s_bot[:, lb] for lb in lane_blocks], mx_ref[hh, bot])

    pairs_then_rest(last, scores)
    diag_scores()

    for hh in range(2):
        m = jnp.max(mx_ref[hh], axis=-1, keepdims=True)
        mx_ref[hh] = jnp.broadcast_to(m, (MLA_TQ, LANES))

    def weights(hh, j, row_slice, blocks):
        m = mx_ref[hh, row_slice]
        return jnp.concatenate([jnp.exp2(s_ref[hh, j, row_slice, lb] - m) for lb in blocks], axis=1).astype(BF16)

    def weighted(j):
        rows = pl.ds(pl.multiple_of(j * TK, TK), TK)
        for hh, hs in enumerate(halves):
            acc_ref[hh] += _dot(weights(hh, j, slice(None), lane_blocks), v_ref[rows, hs])

    def diag_weighted():
        for hh, hs in enumerate(halves):
            acc_ref[hh, top] += _dot(weights(hh, last, top, half_blocks), v_ref[diag_first_rows, hs])
            acc_ref[hh, bot] += _dot(weights(hh, last, bot, lane_blocks), v_ref[diag_rows, hs])

    pairs_then_rest(last, weighted)
    diag_weighted()

    even, odd = acc_ref[0], acc_ref[1]
    lane = lax.broadcasted_iota(I32, (MLA_TQ, 2 * MLA_V), 1)
    out = jnp.where(lane < MLA_V, even / even[:, MLA_V:MLA_V + 1], odd / odd[:, 0:1])
    o_ref[...] = out.astype(BF16)


def _mla_attention(q, k, v, batch, seq):
    t = q.shape[0]
    nq = seq // MLA_TQ
    pairs = MLA_HEADS // 2
    return pl.pallas_call(
        _mla_body,
        grid=(batch, pairs, nq),
        in_specs=[pl.BlockSpec((MLA_TQ, 2 * HEAD_PAD), lambda b, p, i: (b * nq + i, p)),
                  pl.BlockSpec((seq, 2 * HEAD_PAD), lambda b, p, i: (b, p)),
                  pl.BlockSpec((seq, 2 * HEAD_PAD), lambda b, p, i: (b, p))],
        out_specs=pl.BlockSpec((MLA_TQ, 2 * MLA_V), lambda b, p, i: (b * nq + i, p)),
        out_shape=jax.ShapeDtypeStruct((t, MLA_HEADS * MLA_V), BF16),
        scratch_shapes=[pltpu.VMEM((2, seq // TK, MLA_TQ, TK), F32), pltpu.VMEM((2, MLA_TQ, LANES), F32),
                        pltpu.VMEM((2, MLA_TQ, HEAD_PAD), F32)],
        compiler_params=_cparams(("parallel", "parallel", "arbitrary")),
        name="mla_attention",
    )(q, k, v)


def _even_out_body(x_ref, a_ref, halo_ref, att_ref, pw_ref, ps_ref, wo_ref, g_ref, wg_ref, wu_ref, wd_ref,
                   o_ref, ext_ref, *, seq):
    pos0 = (pl.program_id(0) * TM) % seq
    ext_ref[0:POOL_HALO, :] = jnp.where(pos0 == 0, 0.0, halo_ref[...])
    ext_ref[POOL_HALO:, :] = a_ref[...]
    pos = lax.broadcasted_iota(I32, (TM, 1), 0) + pos0
    y = x_ref[...] + _dot(att_ref[...], wo_ref[POOL_WIDTH:, :])
    for g_i, win in enumerate(POOL_WINDOWS):
        gs = slice(g_i * POOL_CH, (g_i + 1) * POOL_CH)
        a_g = ext_ref[POOL_HALO:, gs]
        ssum = a_g
        for j in range(1, win):
            ssum = ssum + ext_ref[POOL_HALO - j:POOL_HALO - j + TM, gs]
        count = jnp.minimum(pos + 1, win).astype(F32)
        pooled = (ssum / count - a_g).astype(BF16)
        mixed = (_dot(pooled, pw_ref[g_i]) * ps_ref[:, gs]).astype(BF16)
        y = y + _dot(mixed, wo_ref[gs, :])
    h = _rms(y, g_ref[...]).astype(BF16)
    gate = _dot(h, wg_ref[...])
    up = _dot(h, wu_ref[...])
    act = (gate * jax.nn.sigmoid(gate) * up).astype(BF16)
    o_ref[...] = y + _dot(act, wd_ref[...])


def _even_out(x, a, att, pw, ps, wo, g, wg, wu, wd, seq):
    t = x.shape[0]
    row = lambda w: pl.BlockSpec((TM, w), lambda i: (i, 0))
    halo = pl.BlockSpec((POOL_HALO, POOL_WIDTH), lambda i: (jnp.maximum(i * (TM // POOL_HALO) - 1, 0), 0))
    return pl.pallas_call(
        functools.partial(_even_out_body, seq=seq),
        grid=(t // TM,),
        in_specs=[row(D_MODEL), row(POOL_WIDTH), halo, row(MLA_HEADS * MLA_V), _const_spec(pw.shape),
                  _const_spec(ps.shape), _const_spec(wo.shape), _const_spec(g.shape), _const_spec(wg.shape),
                  _const_spec(wu.shape), _const_spec(wd.shape)],
        out_specs=row(D_MODEL),
        out_shape=jax.ShapeDtypeStruct((t, D_MODEL), F32),
        scratch_shapes=[pltpu.VMEM((TM + POOL_HALO, POOL_WIDTH), F32)],
        compiler_params=_cparams(("parallel",)),
        name="even_out",
    )(x, a, a, att, pw, ps, wo, g, wg, wu, wd)


def _odd_in_body(x_ref, g_ref, win_ref, ng_ref, nb_ref, sw_ref, sb_ref, qkv_ref, sg_ref):
    h = _rms(x_ref[...], g_ref[...]).astype(BF16)
    uv = jax.nn.gelu(_dot(h, win_ref[:, 3 * CA_WIDTH:]))
    z = _dot(h, win_ref[:, :3 * CA_WIDTH])
    qkv_ref[:, :CA_WIDTH] = (z[:, :CA_WIDTH] * math.log2(math.e)).astype(BF16)
    qkv_ref[:, CA_WIDTH:] = z[:, CA_WIDTH:].astype(BF16)
    for g_i in range(SGU_GROUPS):
        gs = slice(g_i * SGU_CH, (g_i + 1) * SGU_CH)
        u = uv[:, gs]
        gv = uv[:, SGU_WIDTH + g_i * SGU_CH:SGU_WIDTH + (g_i + 1) * SGU_CH]
        mu = jnp.mean(gv, axis=-1, keepdims=True)
        xc = gv - mu
        vn = xc * lax.rsqrt(jnp.mean(xc * xc, axis=-1, keepdims=True) + EPS) * ng_ref[:, gs] + nb_ref[:, gs]
        vn = vn.astype(BF16)
        w = sw_ref[g_i]
        for c_i in range(TM // SGU_CHUNK):
            rs = slice(c_i * SGU_CHUNK, (c_i + 1) * SGU_CHUNK)
            mixed = _dot(w, vn[rs, :]) + sb_ref[g_i]
            sg_ref[rs, gs] = (u[rs, :] * mixed).astype(BF16)


def _odd_in(x, g, win, ng, nb, sw, sb):
    t = x.shape[0]
    row = lambda w: pl.BlockSpec((TM, w), lambda i: (i, 0))
    return pl.pallas_call(
        _odd_in_body,
        grid=(t // TM,),
        in_specs=[row(D_MODEL), _const_spec(g.shape), _const_spec(win.shape), _const_spec(ng.shape),
                  _const_spec(nb.shape), _const_spec(sw.shape), _const_spec(sb.shape)],
        out_specs=[row(3 * CA_WIDTH), row(SGU_WIDTH)],
        out_shape=[jax.ShapeDtypeStruct((t, 3 * CA_WIDTH), BF16), jax.ShapeDtypeStruct((t, SGU_WIDTH), BF16)],
        compiler_params=_cparams(("parallel",)),
        name="odd_in",
    )(x, g, win, ng, nb, sw, sb)


BAND_BLOCKS = 3


def _band_body(q_ref, k0_ref, k1_ref, k2_ref, v0_ref, v1_ref, v2_ref, bias_ref, o_ref):
    pair_w = 2 * CA_HEAD_DIM
    low = lax.broadcasted_iota(I32, (TQ, pair_w), 1) < CA_HEAD_DIM
    k_refs = (k0_ref, k1_ref, k2_ref)
    v_refs = (v0_ref, v1_ref, v2_ref)
    add = lambda a, b: a + b
    per_tile = TQ // LANES
    for pair in range(CA_HEADS // 2):
        ps = slice(pair * pair_w, (pair + 1) * pair_w)
        q_pair = q_ref[:, ps]
        outs = []
        for hh in range(2):
            q = jnp.where(low if hh == 0 else jnp.logical_not(low), q_pair, jnp.zeros_like(q_pair))
            s = []
            for jj in range(BAND_BLOCKS):
                sj = _dot_nt(q, k_refs[jj][:, ps]) + bias_ref[2 * pair + hh, :, jj * TQ:(jj + 1) * TQ]
                s += [sj[:, c * LANES:(c + 1) * LANES] for c in range(per_tile)]
            m = jnp.max(functools.reduce(jnp.maximum, s), axis=-1, keepdims=True)
            mb = jnp.broadcast_to(m, (TQ, LANES))
            p = [jnp.exp2(x - mb) for x in s]
            l = jnp.sum(functools.reduce(add, p), axis=-1, keepdims=True)
            acc = functools.reduce(add, [
                _dot(jnp.concatenate(p[jj * per_tile:(jj + 1) * per_tile], axis=1).astype(BF16), v_refs[jj][:, ps])
                for jj in range(BAND_BLOCKS)])
            outs.append(acc / l)
        o_ref[:, ps] = jnp.where(low, outs[0], outs[1]).astype(BF16)


def _band_attention(qkv, bias, batch, seq):
    t = qkv.shape[0]
    nq = seq // TQ

    def kv_spec(jj, col):
        return pl.BlockSpec((TQ, CA_WIDTH), lambda b, i: (b * nq + jnp.maximum(i - (BAND_BLOCKS - 1) + jj, 0), col))

    return pl.pallas_call(
        _band_body,
        grid=(batch, nq),
        in_specs=[pl.BlockSpec((TQ, CA_WIDTH), lambda b, i: (b * nq + i, 0))]
                 + [kv_spec(jj, 1) for jj in range(BAND_BLOCKS)]
                 + [kv_spec(jj, 2) for jj in range(BAND_BLOCKS)]
                 + [pl.BlockSpec((None, CA_HEADS, TQ, BAND_BLOCKS * TQ),
                                 lambda b, i: (jnp.minimum(i, BAND_BLOCKS - 1), 0, 0, 0))],
        out_specs=pl.BlockSpec((TQ, CA_WIDTH), lambda b, i: (b * nq + i, 0)),
        out_shape=jax.ShapeDtypeStruct((t, CA_WIDTH), BF16),
        compiler_params=_cparams(("parallel", "arbitrary")),
        name="band_attention",
    )(qkv, qkv, qkv, qkv, qkv, qkv, qkv, bias)


ROUTE_LANES = 128
ROUTE_ROWS = 8


def _odd_out_body(x_ref, att_ref, sg_ref, wo_ref, g_ref, wr_ref, tri_ref, x1_ref, h_ref, route_ref, cnt_ref, run_ref):
    @pl.when(pl.program_id(0) == 0)
    def _():
        run_ref[...] = jnp.zeros_like(run_ref)

    y = x_ref[...] + _dot(att_ref[...], wo_ref[:CA_WIDTH, :]) + _dot(sg_ref[...], wo_ref[CA_WIDTH:, :])
    x1_ref[...] = y
    h = _rms(y, g_ref[...])
    _store_token_tiles(h_ref, h)
    h_hi = h.astype(BF16)
    h_lo = (h - h_hi.astype(F32)).astype(BF16)
    logits = _dot(h_hi, wr_ref[0]) + (_dot(h_lo, wr_ref[0]) + _dot(h_hi, wr_ref[1]))
    lt = logits.T[:N_EXPERTS]
    row = lax.broadcasted_iota(I32, (N_EXPERTS, TM), 0)
    m1 = jnp.max(lt, axis=0, keepdims=True)
    e1 = jnp.min(jnp.where(lt == m1, row, N_EXPERTS), axis=0, keepdims=True)
    rest = jnp.where(row == e1, NEG, lt)
    m2 = jnp.max(rest, axis=0, keepdims=True)
    e2 = jnp.min(jnp.where(rest == m2, row, N_EXPERTS), axis=0, keepdims=True)
    ex = jnp.exp(m2 - m1)
    w1 = 1.0 / (1.0 + ex)
    w2 = ex / (1.0 + ex)
    hot1 = (row == e1).astype(F32)
    hot2 = (row == e2).astype(F32)
    sel = hot1 + hot2
    before = _dot(sel.astype(BF16), tri_ref[...]) + run_ref[:, 0:1]
    rank1 = jnp.sum(hot1 * before, axis=0, keepdims=True)
    rank2 = jnp.sum(hot2 * before, axis=0, keepdims=True)
    unused = jnp.zeros((ROUTE_ROWS - 6, TM), F32)
    route_ref[...] = jnp.concatenate([e1.astype(F32), e2.astype(F32), rank1, rank2, w1, w2, unused], axis=0)
    total = run_ref[:, 0:1] + jnp.sum(sel, axis=1, keepdims=True)
    run_ref[...] = jnp.broadcast_to(total, run_ref.shape)
    cnt_ref[...] = jnp.broadcast_to(total, cnt_ref.shape)


def _odd_out(x, att, sg, wo, g, wr, tri):
    t = x.shape[0]
    row = lambda w: pl.BlockSpec((TM, w), lambda i: (i, 0))
    return pl.pallas_call(
        _odd_out_body,
        grid=(t // TM,),
        in_specs=[row(D_MODEL), row(CA_WIDTH), row(SGU_WIDTH), _const_spec(wo.shape), _const_spec(g.shape),
                  _const_spec(wr.shape), _const_spec(tri.shape)],
        out_specs=[row(D_MODEL), pl.BlockSpec((TM * TOKEN_TILE_ROWS, LANES), lambda i: (i, 0)),
                   pl.BlockSpec((ROUTE_ROWS, TM), lambda i: (0, i)),
                   pl.BlockSpec((N_EXPERTS, ROUTE_LANES), lambda i: (0, 0))],
        out_shape=[jax.ShapeDtypeStruct((t, D_MODEL), F32), jax.ShapeDtypeStruct((t * TOKEN_TILE_ROWS, LANES), F32),
                   jax.ShapeDtypeStruct((ROUTE_ROWS, t), F32), jax.ShapeDtypeStruct((N_EXPERTS, ROUTE_LANES), F32)],
        scratch_shapes=[pltpu.VMEM((N_EXPERTS, ROUTE_LANES), F32)],
        compiler_params=_cparams(("arbitrary",)),
        name="odd_out_router",
    )(x, att, sg, wo, g, wr, tri)


def _moe_body(tile_e_ref, n_used_ref, src_ref, dst_ref, h_hbm, wg_ref, wu_ref, wd_ref, o_hbm,
              xf_ref, xb_ref, acc_ref, y_ref, gsem, ssem):
    i = pl.program_id(0)
    j = pl.program_id(1)
    n_used = n_used_ref[0]
    rows_per_step = MOE_TM // MOE_FF_STEPS

    ttr = TOKEN_TILE_ROWS
    tile_rows = MOE_TM * ttr

    def token_rows(ref, first_row):
        return ref.at[pl.ds(pl.multiple_of(first_row, ttr), ttr), :]

    def gather_copy(tile, r, slot):
        return pltpu.make_async_copy(token_rows(h_hbm, src_ref[tile * MOE_TM + r]),
                                     token_rows(xf_ref.at[slot], r * ttr), gsem)

    def scatter_copy(tile, r, slot):
        return pltpu.make_async_copy(token_rows(y_ref.at[slot], r * ttr),
                                     token_rows(o_hbm, dst_ref[tile * MOE_TM + r]), ssem)

    def wait_tile_gather(slot):
        pltpu.make_async_copy(h_hbm.at[pl.ds(0, tile_rows), :], xf_ref.at[slot], gsem).wait()

    def wait_tile_scatter(slot):
        pltpu.make_async_copy(y_ref.at[slot], o_hbm.at[pl.ds(0, tile_rows), :], ssem).wait()

    def start_all(copy_fn):
        lax.fori_loop(0, MOE_TM, lambda r, c: (copy_fn(r).start(), c)[1], 0)

    @pl.when(i < n_used)
    def _():
        cur = lax.rem(i, 2)
        other = 1 - cur
        nxt_tile = jnp.minimum(i + 1, n_used - 1)
        prev_tile = jnp.maximum(i - 1, 0)

        @pl.when(jnp.logical_and(i == 0, j == 0))
        def _():
            y_ref[...] = jnp.zeros_like(y_ref)
            acc_ref[...] = jnp.zeros_like(acc_ref)
            spare0 = o_hbm.shape[0] - N_EXPERTS * tile_rows
            for e_i in range(N_EXPERTS):
                fill = pltpu.make_async_copy(y_ref.at[0], o_hbm.at[pl.ds(spare0 + e_i * tile_rows, tile_rows), :],
                                             ssem)
                fill.start()
                fill.wait()
            start_all(lambda r: gather_copy(0, r, 0))

        @pl.when(j == 0)
        def _():
            wait_tile_gather(cur)
            xb_ref[...] = _load_token_tiles(xf_ref.at[cur], MOE_TM).astype(BF16)

        base = pl.multiple_of(j * rows_per_step, 8)
        for rr in range(rows_per_step):
            gather_copy(nxt_tile, base + rr, other).start(priority=rr % 2)
            scatter_copy(prev_tile, base + rr, other).start(priority=rr % 2)
        xb = xb_ref[...]
        gate = _dot(xb, wg_ref[...])
        up = _dot(xb, wu_ref[...])
        act = (gate * jax.nn.sigmoid(gate) * up).astype(BF16)
        acc_ref[...] = jnp.where(j == 0, 0.0, acc_ref[...]) + _dot(act, wd_ref[...])

        @pl.when(j == MOE_FF_STEPS - 1)
        def _():
            wait_tile_scatter(other)
            _store_token_tiles(y_ref.at[cur], acc_ref[...])

        @pl.when(jnp.logical_and(i == n_used - 1, j == MOE_FF_STEPS - 1))
        def _():
            start_all(lambda r: scatter_copy(i, r, cur))
            wait_tile_scatter(cur)
            wait_tile_gather(other)


def _moe(h, tile_e, n_used, src, dst, wg, wu, wd, layer):
    tokens = h.shape[0] // TOKEN_TILE_ROWS
    n_tiles = tile_e.shape[0]
    last = MOE_FF_STEPS - 1

    def ff_idx(i, j, n_used_ref):
        return jnp.where(i < n_used_ref[0], j, last)

    grid_spec = pltpu.PrefetchScalarGridSpec(
        num_scalar_prefetch=4,
        grid=(n_tiles, MOE_FF_STEPS),
        in_specs=[pl.BlockSpec(memory_space=pl.ANY),
                  pl.BlockSpec((None, None, D_MODEL, MOE_TF),
                               lambda i, j, te, nu, sr, ds: (layer, te[i], 0, ff_idx(i, j, nu))),
                  pl.BlockSpec((None, None, D_MODEL, MOE_TF),
                               lambda i, j, te, nu, sr, ds: (layer, te[i], 0, ff_idx(i, j, nu))),
                  pl.BlockSpec((None, None, MOE_TF, D_MODEL),
                               lambda i, j, te, nu, sr, ds: (layer, te[i], ff_idx(i, j, nu), 0))],
        out_specs=pl.BlockSpec(memory_space=pl.ANY),
        scratch_shapes=[pltpu.VMEM((2, MOE_TM * TOKEN_TILE_ROWS, LANES), F32), pltpu.VMEM((MOE_TM, D_MODEL), BF16),
                        pltpu.VMEM((MOE_TM, D_MODEL), F32), pltpu.VMEM((2, MOE_TM * TOKEN_TILE_ROWS, LANES), F32),
                        pltpu.SemaphoreType.DMA, pltpu.SemaphoreType.DMA],
    )
    return pl.pallas_call(
        _moe_body,
        grid_spec=grid_spec,
        out_shape=jax.ShapeDtypeStruct(((2 * tokens + N_EXPERTS * MOE_TM) * TOKEN_TILE_ROWS, LANES), F32),
        compiler_params=_cparams(("arbitrary", "arbitrary")),
        name="moe_experts",
    )(tile_e, n_used, src, dst, h, wg, wu, wd)


def _row_order_body(pos_ref, cnt_ref, tile_end_ref, dst_ref, *, tokens):
    spare0 = 2 * tokens
    first_tile = 0
    for e_i in range(N_EXPERTS):
        end_tile = tile_end_ref[e_i]
        end_row = end_tile * MOE_TM
        spare_base = spare0 + e_i * MOE_TM - (end_row - MOE_TM)

        def pad(p, c, spare_base=spare_base):
            dst_ref[p] = spare_base + p
            return c

        lax.fori_loop(first_tile * MOE_TM + cnt_ref[e_i], end_row, pad, 0)
        first_tile = end_tile

    def unused(p, c):
        dst_ref[p] = spare0
        return c

    lax.fori_loop(first_tile * MOE_TM, dst_ref.shape[0], unused, 0)

    unroll = 16

    def place(block, c):
        for u in range(unroll):
            a = block * unroll + u
            dst_ref[pos_ref[a]] = a
        return c

    lax.fori_loop(0, 2 * tokens // unroll, place, 0)


def _row_order(pos, cnt, tile_end, tokens, n_tiles):
    smem = pl.BlockSpec(memory_space=pltpu.SMEM)
    return pl.pallas_call(
        functools.partial(_row_order_body, tokens=tokens),
        in_specs=[smem, smem, smem],
        out_specs=smem,
        out_shape=jax.ShapeDtypeStruct((n_tiles * MOE_TM,), I32),
        name="moe_row_order",
    )(pos, cnt, tile_end)


def _moe_plan(route, counts, tokens):
    n_tiles = (2 * tokens + N_EXPERTS * (MOE_TM - 1)) // MOE_TM
    experts = jnp.arange(N_EXPERTS, dtype=I32)
    cnt = counts[:, 0].astype(I32)
    e = route[0:2].astype(I32)
    rank = route[2:4].astype(I32)
    tiles_per_e = (cnt + MOE_TM - 1) // MOE_TM
    tile_end = jnp.cumsum(tiles_per_e)
    tile_start = tile_end - tiles_per_e
    row_start = jnp.sum(jnp.where(e[..., None] == experts, tile_start * MOE_TM, 0), axis=-1)
    pos = row_start + rank
    n_used = tile_end[-1]
    tile = jnp.minimum(jnp.arange(n_tiles, dtype=I32), n_used - 1)
    tile_e = jnp.sum((tile_end[None, :] <= tile[:, None]).astype(I32), axis=1)
    dst = _row_order(pos.reshape(-1), cnt, tile_end, tokens, n_tiles)
    src = jnp.where(dst < 2 * tokens, dst % tokens, 0)
    return tile_e, n_used.reshape(1), src * TOKEN_TILE_ROWS, dst * TOKEN_TILE_ROWS


def _combine_body(x_ref, oa_ref, ob_ref, gate_ref, g_ref, o_ref, *, final):
    w1 = gate_ref[:, 0:1]
    w2 = gate_ref[:, 1:2]
    y = x_ref[...] + (w1 * _load_token_tiles(oa_ref, TM) + w2 * _load_token_tiles(ob_ref, TM))
    o_ref[...] = _rms(y, g_ref[...]) if final else y


def _combine(x, o2, route, g, final):
    t = x.shape[0]
    nt = t // TM
    gates = route[4:6].T
    return pl.pallas_call(
        functools.partial(_combine_body, final=final),
        grid=(nt,),
        in_specs=[pl.BlockSpec((TM, D_MODEL), lambda i: (i, 0)),
                  pl.BlockSpec((TM * TOKEN_TILE_ROWS, LANES), lambda i: (i, 0)),
                  pl.BlockSpec((TM * TOKEN_TILE_ROWS, LANES), lambda i: (i + nt, 0)),
                  pl.BlockSpec((TM, 2), lambda i: (i, 0)), _const_spec(g.shape)],
        out_specs=pl.BlockSpec((TM, D_MODEL), lambda i: (i, 0)),
        out_shape=jax.ShapeDtypeStruct((t, D_MODEL), F32),
        compiler_params=_cparams(("parallel",)),
        name="moe_combine",
    )(x, o2, o2, gates, g)


def _rot_half(w):
    half = w.shape[-1] // 2
    return jnp.concatenate([-w[..., half:], w[..., :half]], axis=-1)


def _rope_tables(seq):
    inv = 1.0 / (ROPE_THETA ** (jnp.arange(0, MLA_ROPE, 2, dtype=F32) / MLA_ROPE))
    ang = jnp.arange(seq, dtype=F32)[:, None] * inv[None, :]
    cos, sin = jnp.cos(ang), jnp.sin(ang)
    pad = HEAD_PAD - MLA_NOPE - MLA_ROPE
    c = jnp.concatenate([jnp.ones((seq, MLA_NOPE), F32), cos, cos, jnp.zeros((seq, pad), F32)], axis=1)
    s = jnp.concatenate([jnp.zeros((seq, MLA_NOPE), F32), sin, sin, jnp.zeros((seq, pad), F32)], axis=1)
    scale = (MLA_NOPE + MLA_ROPE) ** -0.5 * math.log2(math.e)
    return c * scale, s * scale, c, s


def _even_weights(w_in, w_q_up, w_kv_up):
    d = w_in.shape[0]
    pad = HEAD_PAD - MLA_NOPE - MLA_ROPE
    kr = w_in[:, POOL_WIDTH + MLA_Q_RANK + MLA_KV_RANK:]
    place = lambda m: jnp.concatenate([jnp.zeros((d, MLA_NOPE), F32), m, jnp.zeros((d, pad), F32)], axis=1)
    win = jnp.concatenate([w_in[:, :POOL_WIDTH + MLA_Q_RANK + MLA_KV_RANK], place(kr), place(_rot_half(kr))], axis=1)
    wq = w_q_up.reshape(MLA_Q_RANK, MLA_HEADS, MLA_NOPE + MLA_ROPE)
    nope, rope = wq[..., :MLA_NOPE], wq[..., MLA_NOPE:]
    zq = lambda n: jnp.zeros((MLA_Q_RANK, MLA_HEADS, n), F32)
    wq_main = jnp.concatenate([nope, rope, zq(pad)], axis=-1).reshape(MLA_Q_RANK, -1)
    wq_rot = jnp.concatenate([zq(MLA_NOPE), _rot_half(rope), zq(pad)], axis=-1).reshape(MLA_Q_RANK, -1)
    wkv = w_kv_up.reshape(MLA_KV_RANK, MLA_HEADS, MLA_NOPE + MLA_V)
    k_pad = jnp.concatenate([wkv[..., :MLA_NOPE], jnp.zeros((MLA_KV_RANK, MLA_HEADS, HEAD_PAD - MLA_NOPE), F32)],
                            axis=-1).reshape(MLA_KV_RANK, -1)
    v_pairs = wkv[..., MLA_NOPE:].reshape(MLA_KV_RANK, MLA_HEADS // 2, 2, MLA_V)
    v_gap = jnp.zeros((MLA_KV_RANK, MLA_HEADS // 2, 2 * (HEAD_PAD - MLA_V)), F32)
    v_ext = jnp.concatenate([v_pairs[:, :, 0], v_gap, v_pairs[:, :, 1]], axis=-1).reshape(MLA_KV_RANK, -1)
    return (win.astype(BF16), jnp.concatenate([wq_main, wq_rot], axis=1).astype(BF16),
            jnp.concatenate([k_pad, v_ext], axis=1).astype(BF16))


def _v_ones():
    lane = jnp.arange(MLA_HEADS * HEAD_PAD) % (2 * HEAD_PAD)
    return jnp.logical_and(lane >= MLA_V, lane < 2 * HEAD_PAD - MLA_V).astype(F32).reshape(1, -1)


def _band_bias(rel_bias):
    band = BAND_BLOCKS * TQ
    qpos = jnp.arange(TQ)[:, None] + (BAND_BLOCKS - 1) * TQ
    kpos = jnp.arange(band)[None, :]
    span = band + TQ - 1
    rel = (BAND_BLOCKS - 1) * TQ + (TQ - 1) - jnp.arange(span)
    by_rel = rel_bias[:, jnp.clip(rel, -REL_CLIP, REL_CLIP) + REL_CLIP].astype(F32)
    rolled = jnp.roll(by_rel, -(TQ - 1), axis=1)
    bias = jnp.tile(rolled, (1, TQ))[:, :TQ * (span - 1)].reshape(-1, TQ, span - 1)[:, :, :band]
    dc = qpos // CHUNK - kpos // CHUNK
    valid = jnp.logical_and(dc >= 0, dc <= CA_LEFT_CHUNKS)
    started = kpos // TQ >= (BAND_BLOCKS - 1) - jnp.arange(BAND_BLOCKS)[:, None, None]
    ok = jnp.logical_and(valid[None], started)[:, None]
    return jnp.where(ok, bias[None] * math.log2(math.e), NEG)


def kernel(x, ev_norm_mix, ev_w_in, ev_pool_w, ev_pool_scale, ev_q_norm, ev_w_q_up, ev_kv_norm, ev_w_kv_up, ev_w_out, ev_norm_ffn, ev_ffn_w_gate, ev_ffn_w_up, ev_ffn_w_down, od_norm_mix, od_w_in, od_rel_bias, od_sgu_norm_g, od_sgu_norm_b, od_sgu_w, od_sgu_b, od_w_out, od_norm_ffn, od_router, od_moe_w_gate, od_moe_w_up, od_moe_w_down, final_norm):
    batch, seq, d = x.shape
    tokens = batch * seq
    assert d == D_MODEL and seq % TM == 0 and TM % SGU_CHUNK == 0 and TM % TQ == 0 and seq % TK == 0
    depth = ev_w_in.shape[0] + od_w_in.shape[0]
    xt = x.reshape(tokens, d)
    tabs = _rope_tables(seq)
    tri = jnp.triu(jnp.ones((TM, TM), F32), k=1).astype(BF16)
    causal = jnp.tril(jnp.ones((SGU_CHUNK, SGU_CHUNK), bool))
    vec = lambda v: v.reshape(1, -1).astype(F32)
    moe_wg, moe_wu, moe_wd = (w.astype(BF16) for w in (od_moe_w_gate, od_moe_w_up, od_moe_w_down))

    for layer in range(depth):
        i = layer // 2
        if layer % 2 == 0:
            win, wq, wkv = _even_weights(ev_w_in[i], ev_w_q_up[i], ev_w_kv_up[i])
            a, q, k, v = _even_in(xt, vec(ev_norm_mix[i]), win, vec(ev_q_norm[i]), wq, vec(ev_kv_norm[i]), wkv,
                                  _v_ones(), tabs, seq)
            att = _mla_attention(q, k, v, batch, seq)
            xt = _even_out(xt, a, att, ev_pool_w[i].astype(BF16), vec(ev_pool_scale[i]), ev_w_out[i].astype(BF16),
                           vec(ev_norm_ffn[i]), ev_ffn_w_gate[i].astype(BF16), ev_ffn_w_up[i].astype(BF16),
                           ev_ffn_w_down[i].astype(BF16), seq)
        else:
            w_in = od_w_in[i]
            win = jnp.concatenate([w_in[:, :CA_WIDTH] * (CA_HEAD_DIM ** -0.5), w_in[:, CA_WIDTH:]], axis=1).astype(BF16)
            sw = jnp.where(causal[None], od_sgu_w[i], 0.0).astype(BF16)
            sb = jnp.broadcast_to(od_sgu_b[i][:, :, None], (SGU_GROUPS, SGU_CHUNK, SGU_CH)).astype(F32)
            qkv, sg = _odd_in(xt, vec(od_norm_mix[i]), win, vec(od_sgu_norm_g[i]), vec(od_sgu_norm_b[i]), sw, sb)
            att = _band_attention(qkv, _band_bias(od_rel_bias[i]), batch, seq)
            wr = jnp.concatenate([od_router[i], jnp.zeros((d, ROUTE_LANES - N_EXPERTS), F32)], axis=1)
            wr_hi = wr.astype(BF16)
            wr = jnp.stack([wr_hi, (wr - wr_hi.astype(F32)).astype(BF16)])
            x1, h, route, counts = _odd_out(xt, att, sg, od_w_out[i].astype(BF16), vec(od_norm_ffn[i]), wr, tri)
            tile_e, n_used, src, dst = _moe_plan(route, counts, tokens)
            o2 = _moe(h, tile_e, n_used, src, dst, moe_wg, moe_wu, moe_wd, i)
            last = layer == depth - 1
            xt = _combine(x1, o2, route, vec(final_norm), last)
    return xt.reshape(batch, seq, d)
```
